```python
import math
import jax, jax.numpy as jnp
from jax import lax
import numpy as np

D_MODEL = 1024
BATCH = 8
SEQ = 2048
DEPTH = 2
DEC_BATCH = 128
DEC_SEQ = 8
PAST_LEN = 16384
PAGE_SIZE = 128

N_MIXERS = 2
N_SSM_LAYERS = (DEPTH + 1) // 2
N_POOL_LAYERS = DEPTH // 2
EXPAND = 2
D_INNER = EXPAND * D_MODEL
SSM_HEADDIM = 64
SSM_HEADS = D_INNER // SSM_HEADDIM
SSM_GROUPS = 8
SSM_HEADS_PER_GROUP = SSM_HEADS // SSM_GROUPS
D_STATE = 128
D_CONV = 4
CONV_DIM = D_INNER + 2 * SSM_GROUPS * D_STATE
IN_PROJ_DIM = D_INNER + CONV_DIM + SSM_HEADS
SSD_CHUNK = 128
POOL_WINDOWS = (2, 4, 8, 16)
POOL_GROUPS = len(POOL_WINDOWS)
POOL_GROUP_DIM = D_MODEL // POOL_GROUPS
POOL_BUF = max(POOL_WINDOWS) - 1
N_MEM = 256
XATTN_HEADS = 4
XATTN_HEAD_DIM = D_MODEL // XATTN_HEADS
D_FF = 4 * D_MODEL
EPS = 1e-5

kernel_name = 'hybrid_ssd_pool_memory_decoder_step'


def rmsnorm(x, g):
    xf = x.astype(jnp.float32)
    y = xf * lax.rsqrt(jnp.mean(xf * xf, axis=-1, keepdims=True) + EPS)
    return (y * g.astype(jnp.float32)).astype(x.dtype)


def ssd_scan(xs, dt, a, bm, cm, h0):
    f32 = jnp.float32
    b, L = xs.shape[0], xs.shape[1]
    q = min(SSD_CHUNK, L)
    nc = -(-L // q)
    pad = nc * q - L
    xs, dt, bm, cm = xs.astype(f32), dt.astype(f32), bm.astype(f32), cm.astype(f32)
    if pad:
        pw = lambda t: jnp.pad(t, [(0, 0), (0, pad)] + [(0, 0)] * (t.ndim - 2))
        xs, dt, bm, cm = pw(xs), pw(dt), pw(bm), pw(cm)
    G, R = SSM_GROUPS, SSM_HEADS_PER_GROUP
    x = xs.reshape(b, nc, q, G, R, SSM_HEADDIM)
    d = dt.reshape(b, nc, q, G, R)
    B = bm.reshape(b, nc, q, G, D_STATE)
    C = cm.reshape(b, nc, q, G, D_STATE)
    acs = jnp.cumsum(d * a.astype(f32).reshape(G, R), axis=2)
    causal = jnp.tril(jnp.ones((q, q), dtype=bool))
    seg = acs[:, :, :, None] - acs[:, :, None, :]
    decay = jnp.exp(jnp.where(causal[:, :, None, None], seg, -jnp.inf))
    xdt = x * d[..., None]
    cb = jnp.einsum('bclgn,bcsgn->bclsg', C, B)
    y_diag = jnp.einsum('bclsgr,bcsgrp->bclgrp', cb[..., None] * decay, xdt)
    decay_end = jnp.exp(acs[:, :, -1:] - acs)
    chunk_states = jnp.einsum('bcsgn,bcsgr,bcsgrp->bcgrpn', B, decay_end, xdt)
    chunk_decay = jnp.exp(acs[:, :, -1])

    def step(h, inp):
        st, dec = inp
        return h * dec[..., None, None] + st, h

    h_init = h0.astype(f32).reshape(b, G, R, SSM_HEADDIM, D_STATE)
    h_final, h_prev = lax.scan(step, h_init,
                               (jnp.moveaxis(chunk_states, 1, 0), jnp.moveaxis(chunk_decay, 1, 0)))
    h_prev = jnp.moveaxis(h_prev, 0, 1)
    y_off = jnp.einsum('bclgn,bcgrpn,bclgr->bclgrp', C, h_prev, jnp.exp(acs))
    y = (y_diag + y_off).reshape(b, nc * q, SSM_HEADS, SSM_HEADDIM)[:, :L]
    return y, h_final.reshape(b, SSM_HEADS, SSM_HEADDIM, D_STATE)


def mamba_mixer(u, conv_buf, ssm_state, w_in, conv_w, conv_b, dt_bias, a_log, d_skip, norm_gated, w_out):
    b, L, _ = u.shape
    zxbcdt = u @ w_in
    z = zxbcdt[..., :D_INNER]
    xbc = zxbcdt[..., D_INNER:D_INNER + CONV_DIM]
    dt_raw = zxbcdt[..., D_INNER + CONV_DIM:]
    xpad = jnp.concatenate([conv_buf.astype(xbc.dtype), xbc], axis=1)
    conv = conv_b + sum(xpad[:, k:k + L] * conv_w[k] for k in range(D_CONV))
    xbc_act = jax.nn.silu(conv)
    new_conv = xpad[:, L:]
    xs = xbc_act[..., :D_INNER].reshape(b, L, SSM_HEADS, SSM_HEADDIM)
    bm = xbc_act[..., D_INNER:D_INNER + SSM_GROUPS * D_STATE].reshape(b, L, SSM_GROUPS, D_STATE)
    cm = xbc_act[..., D_INNER + SSM_GROUPS * D_STATE:].reshape(b, L, SSM_GROUPS, D_STATE)
    dt = jax.nn.softplus(dt_raw.astype(jnp.float32) + dt_bias.astype(jnp.float32))
    a = -jnp.exp(a_log.astype(jnp.float32))
    y, new_state = ssd_scan(xs, dt, a, bm, cm, ssm_state)
    y = y + d_skip.astype(jnp.float32)[:, None] * xs.astype(jnp.float32)
    yg = y.reshape(b, L, D_INNER) * jax.nn.silu(z.astype(jnp.float32))
    yg = yg.reshape(b, L, SSM_GROUPS, D_INNER // SSM_GROUPS)
    yg = yg * lax.rsqrt(jnp.mean(yg * yg, axis=-1, keepdims=True) + EPS)
    yg = yg.reshape(b, L, D_INNER) * norm_gated.astype(jnp.float32)
    out = yg.astype(u.dtype) @ w_out
    return out, new_conv, new_state.astype(ssm_state.dtype)


def pool_mixer(u, buf, pos0, w_pool, pool_scale):
    b, L, D = u.shape
    uf = u.astype(jnp.float32)
    ext = jnp.concatenate([buf.astype(jnp.float32), uf], axis=1)
    cs = jnp.concatenate([jnp.zeros((b, 1, D), jnp.float32), jnp.cumsum(ext, axis=1)], axis=1)
    pos = pos0 + jnp.arange(L)
    W = POOL_BUF + 1
    outs = []
    for g, w in enumerate(POOL_WINDOWS):
        sl = slice(g * POOL_GROUP_DIM, (g + 1) * POOL_GROUP_DIM)
        win_sum = cs[:, W:W + L, sl] - cs[:, W - w:W - w + L, sl]
        cnt = jnp.minimum(pos + 1, w).astype(jnp.float32)[None, :, None]
        outs.append(win_sum / cnt)
    pooled = jnp.concatenate(outs, axis=-1) - uf
    mixed = jnp.einsum('blgc,gcd->blgd', pooled.reshape(b, L, POOL_GROUPS, POOL_GROUP_DIM),
                       w_pool.astype(jnp.float32)).reshape(b, L, D)
    out = (mixed * pool_scale.astype(jnp.float32)).astype(u.dtype)
    return out, ext[:, L:].astype(u.dtype)


def mem_kv(mem, g_mem, w_k, w_v):
    b, m, _ = mem.shape
    mn = rmsnorm(mem, g_mem)
    k = (mn @ w_k).reshape(b, m, XATTN_HEADS, XATTN_HEAD_DIM)
    v = (mn @ w_v).reshape(b, m, XATTN_HEADS, XATTN_HEAD_DIM)
    return k, v


def cross_attn(h, k, v, w_q, w_o):
    b, L, _ = h.shape
    q = (h @ w_q).reshape(b, L, XATTN_HEADS, XATTN_HEAD_DIM)
    s = jnp.einsum('blhd,bmhd->bhlm', q.astype(jnp.float32), k.astype(jnp.float32)) * (XATTN_HEAD_DIM ** -0.5)
    p = jax.nn.softmax(s, axis=-1)
    o = jnp.einsum('bhlm,bmhd->blhd', p, v.astype(jnp.float32)).reshape(b, L, D_MODEL)
    return o.astype(h.dtype) @ w_o


def sq_relu_mlp(h, w_up, w_down):
    a = jax.nn.relu(h @ w_up)
    return (a * a) @ w_down


def setup_inputs(seed: int = 0) -> dict:
    key = jax.random.key(seed)
    ks = jax.random.split(key, 32)
    f32 = jnp.float32
    nrm = lambda k, shape, s: jax.random.normal(k, shape, f32) * s
    gain = lambda k, shape: 1.0 + 0.1 * jax.random.normal(k, shape, f32)
    kv_shape = (DEPTH, DEC_BATCH, N_MEM, XATTN_HEADS, XATTN_HEAD_DIM)
    dt0 = jnp.exp(jax.random.uniform(ks[16], (N_SSM_LAYERS, SSM_HEADS), f32, math.log(1e-3), math.log(1e-1)))
    return {
        'x_prompt': nrm(ks[0], (BATCH, SEQ, D_MODEL), 1.0),
        'x_sample': nrm(ks[1], (DEC_BATCH, DEC_SEQ, D_MODEL), 1.0),
        'cache_mem_k': nrm(ks[2], kv_shape, 1.0),
        'cache_mem_v': nrm(ks[3], kv_shape, 1.0),
        'state_ssm': nrm(ks[4], (N_SSM_LAYERS, DEC_BATCH, SSM_HEADS, SSM_HEADDIM, D_STATE), 0.1),
        'state_conv': nrm(ks[5], (N_SSM_LAYERS, DEC_BATCH, D_CONV - 1, CONV_DIM), 1.0),
        'state_pool': nrm(ks[6], (N_POOL_LAYERS, DEC_BATCH, POOL_BUF, D_MODEL), 1.0),
        'mem_prompt': nrm(ks[7], (BATCH, N_MEM, D_MODEL), 1.0),
        'norm_mix': gain(ks[8], (DEPTH, D_MODEL)),
        'norm_xattn': gain(ks[9], (DEPTH, D_MODEL)),
        'norm_mem': gain(ks[10], (DEPTH, D_MODEL)),
        'norm_mlp': gain(ks[11], (DEPTH, D_MODEL)),
        'norm_final': gain(ks[12], (D_MODEL,)),
        'w_in': nrm(ks[13], (N_SSM_LAYERS, D_MODEL, IN_PROJ_DIM), D_MODEL ** -0.5),
        'conv_w': nrm(ks[14], (N_SSM_LAYERS, D_CONV, CONV_DIM), D_CONV ** -0.5),
        'conv_b': nrm(ks[15], (N_SSM_LAYERS, CONV_DIM), 0.01),
        'dt_bias': dt0 + jnp.log(-jnp.expm1(-dt0)),
        'a_log': jnp.log(jax.random.uniform(ks[17], (N_SSM_LAYERS, SSM_HEADS), f32, 1.0, 16.0)),
        'd_skip': gain(ks[18], (N_SSM_LAYERS, SSM_HEADS)),
        'norm_gated': gain(ks[19], (N_SSM_LAYERS, D_INNER)),
        'w_out': nrm(ks[20], (N_SSM_LAYERS, D_INNER, D_MODEL), D_INNER ** -0.5),
        'w_pool': nrm(ks[21], (N_POOL_LAYERS, POOL_GROUPS, POOL_GROUP_DIM, POOL_GROUP_DIM), POOL_GROUP_DIM ** -0.5),
        'pool_scale': gain(ks[22], (N_POOL_LAYERS, D_MODEL)),
        'w_xq': nrm(ks[23], (DEPTH, D_MODEL, D_MODEL), D_MODEL ** -0.5),
        'w_xk': nrm(ks[24], (DEPTH, D_MODEL, D_MODEL), D_MODEL ** -0.5),
        'w_xv': nrm(ks[25], (DEPTH, D_MODEL, D_MODEL), D_MODEL ** -0.5),
        'w_xo': nrm(ks[26], (DEPTH, D_MODEL, D_MODEL), D_MODEL ** -0.5),
        'w_up': nrm(ks[27], (DEPTH, D_MODEL, D_FF), D_MODEL ** -0.5),
        'w_down': nrm(ks[28], (DEPTH, D_FF, D_MODEL), D_FF ** -0.5),
    }


def reference(x_prompt, x_sample, cache_mem_k, cache_mem_v, state_ssm, state_conv, state_pool, mem_prompt,
              norm_mix, norm_xattn, norm_mem, norm_mlp, norm_final,
              w_in, conv_w, conv_b, dt_bias, a_log, d_skip, norm_gated, w_out,
              w_pool, pool_scale, w_xq, w_xk, w_xv, w_xo, w_up, w_down):
    xp, xs = x_prompt, x_sample
    bp = x_prompt.shape[0]
    mk_p, mv_p = [], []
    ssm_p, conv_p, pool_p = [], [], []
    ssm_s, conv_s, pool_s = [], [], []
    for i in range(DEPTH):
        j = i // N_MIXERS
        hp = rmsnorm(xp, norm_mix[i])
        hs = rmsnorm(xs, norm_mix[i])
        if i % N_MIXERS == 0:
            prm = (w_in[j], conv_w[j], conv_b[j], dt_bias[j], a_log[j], d_skip[j], norm_gated[j], w_out[j])
            zc = jnp.zeros((bp, D_CONV - 1, CONV_DIM), x_prompt.dtype)
            zs = jnp.zeros((bp, SSM_HEADS, SSM_HEADDIM, D_STATE), state_ssm.dtype)
            op, cp, sp = mamba_mixer(hp, zc, zs, *prm)
            osm, csm, ssm = mamba_mixer(hs, state_conv[j], state_ssm[j], *prm)
            conv_p.append(cp); ssm_p.append(sp)
            conv_s.append(csm); ssm_s.append(ssm)
        else:
            zb = jnp.zeros((bp, POOL_BUF, D_MODEL), x_prompt.dtype)
            op, pbp = pool_mixer(hp, zb, 0, w_pool[j], pool_scale[j])
            osm, pbs = pool_mixer(hs, state_pool[j], PAST_LEN, w_pool[j], pool_scale[j])
            pool_p.append(pbp); pool_s.append(pbs)
        xp = xp + op
        xs = xs + osm
        k_p, v_p = mem_kv(mem_prompt, norm_mem[i], w_xk[i], w_xv[i])
        mk_p.append(k_p); mv_p.append(v_p)
        xp = xp + cross_attn(rmsnorm(xp, norm_xattn[i]), k_p, v_p, w_xq[i], w_xo[i])
        xs = xs + cross_attn(rmsnorm(xs, norm_xattn[i]), cache_mem_k[i], cache_mem_v[i], w_xq[i], w_xo[i])
        xp = xp + sq_relu_mlp(rmsnorm(xp, norm_mlp[i]), w_up[i], w_down[i])
        xs = xs + sq_relu_mlp(rmsnorm(xs, norm_mlp[i]), w_up[i], w_down[i])
    y_prompt = rmsnorm(xp, norm_final)
    y_sample = rmsnorm(xs, norm_final)
    new_mem_k_p = jnp.stack(mk_p)
    new_mem_v_p = jnp.stack(mv_p)
    new_ssm_p = jnp.stack(ssm_p)
    new_conv_p = jnp.stack(conv_p)
    new_pool_p = jnp.stack(pool_p)
    new_ssm_s = jnp.stack(ssm_s)
    new_conv_s = jnp.stack(conv_s)
    new_pool_s = jnp.stack(pool_s)
    return (y_prompt, y_sample, new_mem_k_p, new_mem_v_p, new_ssm_p, new_conv_p, new_pool_p, new_ssm_s, new_conv_s, new_pool_s)
```

```python
import functools

import jax
import jax.numpy as jnp
from jax import lax
from jax.experimental import pallas as pl
from jax.experimental.pallas import tpu as pltpu

F32 = jnp.float32
BF16 = jnp.bfloat16

D_MODEL = 1024
D_INNER = 2048
SSM_HEADS = 32
SSM_HEADDIM = 64
SSM_GROUPS = 8
HEADS_PER_GROUP = 4
GROUP_X = HEADS_PER_GROUP * SSM_HEADDIM
D_STATE = 128
D_CONV = 4
CONV_DIM = D_INNER + 2 * SSM_GROUPS * D_STATE
GROUP_CONV = GROUP_X + 2 * D_STATE
GROUP_IN = GROUP_X + GROUP_CONV
POOL_WINDOWS = (2, 4, 8, 16)
POOL_GROUP_DIM = 256
POOL_BUF = 15
N_MEM = 256
XATTN_HEADS = 4
XATTN_HEAD_DIM = 256
D_FF = 4096
EPS = 1e-5
PAST_LEN = 16384

LANES = 128
SUBLANES = 8
Q = 128
FF_CHUNK = 1024
VMEM_LIMIT = 56 * 1024 * 1024

_NT = (((1,), (1,)), ((), ()))


def _params(n_grid):
    return pltpu.CompilerParams(dimension_semantics=("arbitrary",) * n_grid, vmem_limit_bytes=VMEM_LIMIT)


def _const_spec(shape):
    nd = len(shape)
    return pl.BlockSpec(shape, lambda *_: (0,) * nd, pipeline_mode=pl.Buffered(1))


def _dot(a, b):
    return jnp.dot(a, b, preferred_element_type=F32)


def _dot_nt(a, b):
    return lax.dot_general(a, b, _NT, preferred_element_type=F32)


def _rms(x, g):
    return x * lax.rsqrt(jnp.mean(x * x, axis=-1, keepdims=True) + EPS) * g


def _silu(x):
    return x * (1.0 / (1.0 + jnp.exp(-x)))


def _kv_body(mem_ref, g_ref, wk_ref, wv_ref, k_ref, v_ref, kb_ref, vb_ref):
    mn = _rms(mem_ref[...], g_ref[0]).astype(BF16)
    k = _dot(mn, wk_ref[0])
    v = _dot(mn, wv_ref[0])
    k_ref[0] = k
    v_ref[0] = v
    kb_ref[0] = k.astype(BF16)
    vb_ref[0] = v.astype(BF16)


def _kv_call(mem, g_mem, wk, wv, tm=512):
    n = mem.shape[0]
    depth = wk.shape[0]
    row = pl.BlockSpec((tm, D_MODEL), lambda l, t: (t, 0))
    wspec = pl.BlockSpec((1, D_MODEL, D_MODEL), lambda l, t: (l, 0, 0))
    ospec = pl.BlockSpec((1, tm, D_MODEL), lambda l, t: (l, t, 0))
    return pl.pallas_call(
        _kv_body,
        grid=(depth, n // tm),
        in_specs=[row, pl.BlockSpec((1, 1, D_MODEL), lambda l, t: (l, 0, 0)), wspec, wspec],
        out_specs=[ospec, ospec, ospec, ospec],
        out_shape=[jax.ShapeDtypeStruct((depth, n, D_MODEL), F32)] * 2
        + [jax.ShapeDtypeStruct((depth, n, D_MODEL), BF16)] * 2,
        compiler_params=_params(2),
        name="mem_kv",
    )(mem, g_mem, wk, wv)


def _attn_heads(q, k_of, v_of):
    outs = []
    for h in range(XATTN_HEADS):
        sl = slice(h * XATTN_HEAD_DIM, (h + 1) * XATTN_HEAD_DIM)
        s = _dot_nt(q[:, sl].astype(BF16), k_of(h)) * (XATTN_HEAD_DIM ** -0.5)
        p = jnp.exp(s - jnp.max(s, axis=-1, keepdims=True))
        p = p * (1.0 / jnp.sum(p, axis=-1, keepdims=True))
        outs.append(_dot(p.astype(BF16), v_of(h)))
    return jnp.concatenate(outs, axis=-1)


def _mlp(x, g, wup_ref, wdn_ref):
    hn = _rms(x, g).astype(BF16)
    acc = x
    for c in range(D_FF // FF_CHUNK):
        a = jnp.maximum(_dot(hn, wup_ref[:, c * FF_CHUNK:(c + 1) * FF_CHUNK]), 0.0)
        acc = acc + _dot((a * a).astype(BF16), wdn_ref[c * FF_CHUNK:(c + 1) * FF_CHUNK, :])
    return acc


def _pool_windows(load, u, pos, n):
    outs = []
    for gi, w in enumerate(POOL_WINDOWS):
        cols = slice(gi * POOL_GROUP_DIM, (gi + 1) * POOL_GROUP_DIM)
        acc = load(0, cols)
        for k in range(1, w):
            acc = acc + load(k, cols)
        cnt = jnp.minimum(pos + 1, w).astype(F32)
        outs.append(acc / cnt - u[:, cols])
    return outs


def _pool_mix(pooled, wpool_ref, scale):
    outs = [_dot(p.astype(BF16), wpool_ref[gi]) for gi, p in enumerate(pooled)]
    return jnp.concatenate(outs, axis=-1) * scale


def _prompt_layer_body(*refs, tq, with_pool, with_final):
    it = iter(refs)
    x_ref = next(it)
    if with_pool:
        gmix_ref, wpool_ref, pscale_ref = next(it), next(it), next(it)
    gx_ref, wq_ref, kb_ref, vb_ref, wo_ref = next(it), next(it), next(it), next(it), next(it)
    gm_ref, wup_ref, wdn_ref = next(it), next(it), next(it)
    if with_final:
        gf_ref = next(it)
    y_ref = next(it)
    if with_pool:
        pool_ref, ext_ref = next(it), next(it)

    x = x_ref[0]
    if with_pool:
        j = pl.program_id(1)

        @pl.when(j == 0)
        def _():
            ext_ref[0:2 * SUBLANES, :] = jnp.zeros((2 * SUBLANES, D_MODEL), F32)

        u = _rms(x, gmix_ref[...])
        ext_ref[pl.ds(2 * SUBLANES, tq), :] = u
        pos = j * tq + lax.broadcasted_iota(jnp.int32, (tq, 1), 0)
        pooled = _pool_windows(lambda k, cols: ext_ref[pl.ds(2 * SUBLANES - k, tq), cols], u, pos, tq)
        x = x + _pool_mix(pooled, wpool_ref, pscale_ref[...])
        pool_ref[0] = ext_ref[pl.ds(tq + 1, POOL_BUF), :]
        ext_ref[0:2 * SUBLANES, :] = ext_ref[pl.ds(tq, 2 * SUBLANES), :]

    q = _dot(_rms(x, gx_ref[...]).astype(BF16), wq_ref[...])
    o = _attn_heads(
        q,
        lambda h: kb_ref[0, 0, :, h * XATTN_HEAD_DIM:(h + 1) * XATTN_HEAD_DIM],
        lambda h: vb_ref[0, 0, :, h * XATTN_HEAD_DIM:(h + 1) * XATTN_HEAD_DIM],
    )
    x = x + _dot(o.astype(BF16), wo_ref[...])
    x = _mlp(x, gm_ref[...], wup_ref, wdn_ref)
    if with_final:
        x = _rms(x, gf_ref[...])
    y_ref[0] = x


def _prompt_layer_call(x, layer, kb, vb, gx, wq, wo, gm, wup, wdn, pool=None, gfinal=None, tq=512):
    b, seq, _ = x.shape
    with_pool, with_final = pool is not None, gfinal is not None
    xspec = pl.BlockSpec((1, tq, D_MODEL), lambda i, j: (i, j, 0))
    vec = _const_spec((1, D_MODEL))
    kvspec = pl.BlockSpec((1, 1, N_MEM, D_MODEL), lambda i, j: (layer, i, 0, 0))
    args, specs = [x], [xspec]
    if with_pool:
        gmix, wpool, pscale = pool
        args += [gmix, wpool, pscale]
        specs += [vec, _const_spec(wpool.shape), vec]
    args += [gx, wq, kb, vb, wo, gm, wup, wdn]
    specs += [vec, _const_spec(wq.shape), kvspec, kvspec, _const_spec(wo.shape), vec,
              _const_spec(wup.shape), _const_spec(wdn.shape)]
    if with_final:
        args.append(gfinal)
        specs.append(vec)
    out_shape = [jax.ShapeDtypeStruct(x.shape, F32)]
    out_specs = [xspec]
    scratch = []
    if with_pool:
        out_shape.append(jax.ShapeDtypeStruct((b, POOL_BUF, D_MODEL), F32))
        out_specs.append(pl.BlockSpec((1, POOL_BUF, D_MODEL), lambda i, j: (i, 0, 0)))
        scratch.append(pltpu.VMEM((tq + 2 * SUBLANES, D_MODEL), F32))
    outs = pl.pallas_call(
        functools.partial(_prompt_layer_body, tq=tq, with_pool=with_pool, with_final=with_final),
        grid=(b, seq // tq),
        in_specs=specs,
        out_specs=out_specs,
        out_shape=out_shape,
        scratch_shapes=scratch,
        compiler_params=_params(2),
        name="prompt_layer_pool" if with_pool else "prompt_layer",
    )(*args)
    return outs if with_pool else outs[0]


def _dt_stage(xn_ref, wdt_ref, dtb_ref, alog_ref, dt_ref, acs_ref, acsT_ref, mask, n_chunks):
    z = _dot(xn_ref[...], wdt_ref[...]) + dtb_ref[...]
    dt_ref[...] = jnp.maximum(z, 0.0) + jnp.log1p(jnp.exp(-jnp.abs(z)))
    a = -jnp.exp(alog_ref[...])
    tri = mask.astype(F32)
    for c in range(n_chunks):
        rows = slice(c * Q, (c + 1) * Q)
        acs = jnp.dot(tri, dt_ref[rows, :] * a, precision=lax.Precision.HIGHEST, preferred_element_type=F32)
        acs_ref[rows, :] = acs
        acsT_ref[c] = acs.T


def _conv_rows(load_shifted, cw_ref, cb_ref, g):
    acc = cb_ref[g]
    for k in range(D_CONV):
        acc = acc + cw_ref[g, k:k + 1, :] * load_shifted(k)
    return _silu(acc)


def _expand_heads(cols):
    half = lax.broadcasted_iota(jnp.int32, (Q, LANES), 1) < SSM_HEADDIM
    return jnp.concatenate([jnp.where(half, cols[0], cols[1]), jnp.where(half, cols[2], cols[3])], axis=1)


def _ssd_group(g, xg, bg, cg, dt_c, acs_c, acsT_c, seg_last, dskip_g, mask):
    heads = [HEADS_PER_GROUP * g + r for r in range(HEADS_PER_GROUP)]
    acs_cols = [jnp.broadcast_to(acs_c[:, h:h + 1], (Q, LANES)) for h in heads]
    dt_cols = [jnp.broadcast_to(dt_c[:, h:h + 1], (Q, LANES)) for h in heads]
    acs_x = _expand_heads(acs_cols)
    xdt = xg * _expand_heads(dt_cols)
    xdt_b = xdt.astype(BF16)
    cb = _dot_nt(cg, bg)
    lane_head = _div_pow2(lax.broadcasted_iota(jnp.int32, (Q, GROUP_X), 1), SSM_HEADDIM)
    yd = None
    for r, h in enumerate(heads):
        dec = jnp.exp(jnp.where(mask, acs_cols[r] - acsT_c[h:h + 1, :], -jnp.inf))
        yr = _dot((cb * dec).astype(BF16), xdt_b)
        yd = yr if r == 0 else jnp.where(lane_head == r, yr, yd)
    ydx = yd + dskip_g * xg
    xdd = xdt * jnp.exp(seg_last(acs_x) - acs_x)
    return ydx, jnp.exp(acs_x), xdd


def _gate_norm(y, z, ng):
    yg = y * _silu(z)
    return yg * lax.rsqrt(jnp.mean(yg * yg, axis=-1, keepdims=True) + EPS) * ng


def _div_pow2(v, d):
    assert d & (d - 1) == 0
    return lax.shift_right_logical(v, d.bit_length() - 1)


def _causal_mask(block):
    li = lax.broadcasted_iota(jnp.int32, (Q, Q), 0)
    si = lax.broadcasted_iota(jnp.int32, (Q, Q), 1)
    m = li >= si
    if block is not None:
        m = jnp.logical_and(m, _div_pow2(li, block) == _div_pow2(si, block))
    return m


def _head_rows(vals):
    return jnp.concatenate([jnp.broadcast_to(v, (SSM_HEADDIM, LANES)) for v in vals], axis=0)


def _mamba_prompt_body(x_ref, gmix_ref, wg_ref, wdt_ref, cw_ref, cb_ref, dtb_ref, alog_ref, dsk_ref, ng_ref, wout_ref,
                       y_ref, ssm_ref, conv_ref,
                       xn_ref, dt_ref, acs_ref, acsT_ref, zxg_ref, xpad_ref, act_ref, hist_ref, h_ref, yg_ref, *, tq):
    j = pl.program_id(1)
    n_chunks = tq // Q

    @pl.when(j == 0)
    def _():
        hist_ref[...] = jnp.zeros_like(hist_ref)
        h_ref[...] = jnp.zeros_like(h_ref)

    mask = _causal_mask(None)
    xn_ref[...] = _rms(x_ref[0], gmix_ref[...]).astype(BF16)
    _dt_stage(xn_ref, wdt_ref, dtb_ref, alog_ref, dt_ref, acs_ref, acsT_ref, mask, n_chunks)

    for g in range(SSM_GROUPS):
        zxg_ref[...] = _dot(xn_ref[...], wg_ref[g])
        xpad_ref[0:SUBLANES, :] = hist_ref[g]
        xpad_ref[pl.ds(SUBLANES, tq), :] = zxg_ref[:, GROUP_X:GROUP_IN]

        for c in range(n_chunks):
            act_ref[c * Q:(c + 1) * Q, :] = _conv_rows(
                lambda k: xpad_ref[pl.ds(c * Q + SUBLANES - (D_CONV - 1) + k, Q), :], cw_ref, cb_ref, g)
        hist_ref[g] = xpad_ref[pl.ds(tq, SUBLANES), :]

        def ssd_chunk(c, carry):
            rows = pl.ds(pl.multiple_of(c * Q, Q), Q)
            xg = act_ref[rows, 0:GROUP_X]
            bg = act_ref[rows, GROUP_X:GROUP_X + D_STATE].astype(BF16)
            cg = act_ref[rows, GROUP_X + D_STATE:GROUP_CONV].astype(BF16)
            acs_c = acs_ref[rows, :]
            acsT_c = acsT_ref[c]
            ydx, eacs, xdd = _ssd_group(
                g, xg, bg, cg, dt_ref[rows, :], acs_c, acsT_c,
                lambda v: v[Q - 1:Q, :], dsk_ref[:, g * GROUP_X:(g + 1) * GROUP_X], mask)
            hg = h_ref[g]
            y = ydx + eacs * _dot_nt(cg, hg.astype(BF16))
            yg_ref[rows, g * GROUP_X:(g + 1) * GROUP_X] = _gate_norm(
                y, zxg_ref[rows, 0:GROUP_X], ng_ref[:, g * GROUP_X:(g + 1) * GROUP_X]).astype(BF16)
            scale = _head_rows([jnp.exp(acsT_c[HEADS_PER_GROUP * g + r:HEADS_PER_GROUP * g + r + 1, Q - 1:Q])
                                for r in range(HEADS_PER_GROUP)])
            h_ref[g] = scale * hg + _dot(xdd.T.astype(BF16), bg)
            return carry

        lax.fori_loop(0, n_chunks, ssd_chunk, 0)

        hist = hist_ref[g]
        conv_ref[0, :, g * GROUP_X:(g + 1) * GROUP_X] = hist[SUBLANES - (D_CONV - 1):, 0:GROUP_X]
        conv_ref[0, :, D_INNER + g * D_STATE:D_INNER + (g + 1) * D_STATE] = (
            hist[SUBLANES - (D_CONV - 1):, GROUP_X:GROUP_X + D_STATE])
        conv_ref[0, :, D_INNER + (SSM_GROUPS + g) * D_STATE:D_INNER + (SSM_GROUPS + g + 1) * D_STATE] = (
            hist[SUBLANES - (D_CONV - 1):, GROUP_X + D_STATE:GROUP_CONV])

    y_ref[0] = x_ref[0] + _dot(yg_ref[...], wout_ref[...])

    @pl.when(j == pl.num_programs(1) - 1)
    def _():
        ssm_ref[0] = h_ref[...]


def _mamba_weight_specs(w):
    return [_const_spec(a.shape) for a in w]


def _mamba_prompt_call(x, gmix, w, tq=256):
    b, seq, _ = x.shape
    xspec = pl.BlockSpec((1, tq, D_MODEL), lambda i, j: (i, j, 0))
    n_chunks = tq // Q
    return pl.pallas_call(
        functools.partial(_mamba_prompt_body, tq=tq),
        grid=(b, seq // tq),
        in_specs=[xspec, _const_spec((1, D_MODEL))] + _mamba_weight_specs(w),
        out_specs=[xspec,
                   pl.BlockSpec((1, SSM_GROUPS, GROUP_X, D_STATE), lambda i, j: (i, 0, 0, 0)),
                   pl.BlockSpec((1, D_CONV - 1, CONV_DIM), lambda i, j: (i, 0, 0))],
        out_shape=[jax.ShapeDtypeStruct(x.shape, F32),
                   jax.ShapeDtypeStruct((b, SSM_GROUPS, GROUP_X, D_STATE), F32),
                   jax.ShapeDtypeStruct((b, D_CONV - 1, CONV_DIM), F32)],
        scratch_shapes=[
            pltpu.VMEM((tq, D_MODEL), BF16),
            pltpu.VMEM((tq, LANES), F32),
            pltpu.VMEM((tq, LANES), F32),
            pltpu.VMEM((n_chunks, LANES, Q), F32),
            pltpu.VMEM((tq, GROUP_IN), F32),
            pltpu.VMEM((tq + SUBLANES, GROUP_CONV), F32),
            pltpu.VMEM((tq, GROUP_CONV), F32),
            pltpu.VMEM((SSM_GROUPS, SUBLANES, GROUP_CONV), F32),
            pltpu.VMEM((SSM_GROUPS, GROUP_X, D_STATE), F32),
            pltpu.VMEM((tq, D_INNER), BF16),
        ],
        compiler_params=_params(2),
        name="mamba_prompt",
    )(x, gmix, *w)


def _mamba_sample_body(x_ref, gmix_ref, wg_ref, wdt_ref, cw_ref, cb_ref, dtb_ref, alog_ref, dsk_ref, ng_ref, wout_ref,
                       cst_ref, sst_ref, gx_ref, wq_ref,
                       y_ref, q_ref, ssm_ref, conv_ref,
                       xn_ref, dt_ref, acs_ref, acsT_ref, zxg_ref, xpad_ref,
                       ydx_ref, eacs_ref, xddT_ref, bc_ref, z_ref, scale_ref, yg_ref, *, seq, sub_b):
    s = pl.program_id(1)
    nb = Q // seq
    mask = _causal_mask(seq)

    @pl.when(s == 0)
    def _():
        xn_ref[...] = _rms(x_ref[...], gmix_ref[...]).astype(BF16)
        _dt_stage(xn_ref, wdt_ref, dtb_ref, alog_ref, dt_ref, acs_ref, acsT_ref, mask, 1)
        acsT = acsT_ref[0]
        for bi in range(nb):
            scale_ref[bi] = jnp.broadcast_to(jnp.exp(acsT[:, (bi + 1) * seq - 1:(bi + 1) * seq]), (LANES, LANES))

        def seg_last(v):
            v3 = v.reshape(nb, seq, v.shape[-1])
            return jnp.broadcast_to(v3[:, seq - 1:seq, :], v3.shape).reshape(v.shape)

        for g in range(SSM_GROUPS):
            zxg_ref[...] = _dot(xn_ref[...], wg_ref[g])
            col_slices = (
                (slice(g * GROUP_X, (g + 1) * GROUP_X), slice(0, GROUP_X)),
                (slice(D_INNER + g * D_STATE, D_INNER + (g + 1) * D_STATE), slice(GROUP_X, GROUP_X + D_STATE)),
                (slice(D_INNER + (SSM_GROUPS + g) * D_STATE, D_INNER + (SSM_GROUPS + g + 1) * D_STATE),
                 slice(GROUP_X + D_STATE, GROUP_CONV)),
            )
            for src, dst in col_slices:
                xpad_ref[:, SUBLANES - (D_CONV - 1):SUBLANES, dst] = cst_ref[:, :, src]
            xpad_ref[:, SUBLANES:SUBLANES + seq, :] = zxg_ref[:, GROUP_X:GROUP_IN].reshape(nb, seq, GROUP_CONV)
            act = _conv_rows(
                lambda k: xpad_ref[:, pl.ds(SUBLANES - (D_CONV - 1) + k, seq), :].reshape(Q, GROUP_CONV),
                cw_ref, cb_ref, g)
            for src, dst in col_slices:
                conv_ref[:, :, src] = xpad_ref[:, pl.ds(SUBLANES + seq - (D_CONV - 1), D_CONV - 1), dst]
            xg = act[:, 0:GROUP_X]
            bg = act[:, GROUP_X:GROUP_X + D_STATE].astype(BF16)
            cg = act[:, GROUP_X + D_STATE:GROUP_CONV].astype(BF16)
            ydx, eacs, xdd = _ssd_group(
                g, xg, bg, cg, dt_ref[...], acs_ref[...], acsT, seg_last,
                dsk_ref[:, g * GROUP_X:(g + 1) * GROUP_X], mask)
            ydx_ref[g] = ydx
            eacs_ref[g] = eacs
            xddT_ref[g] = xdd.T
            bc_ref[g, 0] = bg
            bc_ref[g, 1] = cg
            z_ref[g] = zxg_ref[:, 0:GROUP_X]

    rows_per = sub_b * seq
    r0 = pl.multiple_of(s * rows_per, rows_per)
    rows = pl.ds(r0, rows_per)
    row_b = _div_pow2(lax.broadcasted_iota(jnp.int32, (rows_per, GROUP_X), 0), seq)
    lane_b = _div_pow2(lax.broadcasted_iota(jnp.int32, (GROUP_X, Q), 1), seq)
    for g in range(SSM_GROUPS):
        cg = bc_ref[g, 1, rows, :]
        bg = bc_ref[g, 0]
        xddT = xddT_ref[g]
        yoff = jnp.zeros((rows_per, GROUP_X), F32)
        for bi in range(sub_b):
            hg = sst_ref[bi, g]
            yoff = jnp.where(row_b == bi, _dot_nt(cg, hg.astype(BF16)), yoff)
            sc = scale_ref[s * sub_b + bi]
            scale = _head_rows([sc[HEADS_PER_GROUP * g + r:HEADS_PER_GROUP * g + r + 1, :]
                                for r in range(HEADS_PER_GROUP)])
            xm = jnp.where(lane_b == s * sub_b + bi, xddT, 0.0).astype(BF16)
            ssm_ref[bi, g] = scale * hg + _dot(xm, bg)
        y = ydx_ref[g, rows, :] + eacs_ref[g, rows, :] * yoff
        yg_ref[rows, g * GROUP_X:(g + 1) * GROUP_X] = _gate_norm(
            y, z_ref[g, rows, :], ng_ref[:, g * GROUP_X:(g + 1) * GROUP_X]).astype(BF16)

    @pl.when(s == pl.num_programs(1) - 1)
    def _():
        x1 = x_ref[...] + _dot(yg_ref[...], wout_ref[...])
        y_ref[...] = x1
        q_ref[...] = _dot(_rms(x1, gx_ref[...]).astype(BF16), wq_ref[...])


def _mamba_sample_call(x2d, gmix, w, conv_state, ssm_state, gx, wq, seq, sub_b=4):
    n = x2d.shape[0]
    batch = n // seq
    nb = Q // seq
    n_sub = nb // sub_b
    xspec = pl.BlockSpec((Q, D_MODEL), lambda i, s: (i, 0))
    stspec = pl.BlockSpec((sub_b, SSM_GROUPS, GROUP_X, D_STATE), lambda i, s: (i * n_sub + s, 0, 0, 0))
    cspec = pl.BlockSpec((nb, D_CONV - 1, CONV_DIM), lambda i, s: (i, 0, 0))
    return pl.pallas_call(
        functools.partial(_mamba_sample_body, seq=seq, sub_b=sub_b),
        grid=(n // Q, n_sub),
        in_specs=[xspec, _const_spec((1, D_MODEL))] + _mamba_weight_specs(w)
        + [cspec, stspec, _const_spec((1, D_MODEL)), _const_spec(wq.shape)],
        out_specs=[xspec, xspec, stspec, cspec],
        out_shape=[jax.ShapeDtypeStruct(x2d.shape, F32), jax.ShapeDtypeStruct(x2d.shape, F32),
                   jax.ShapeDtypeStruct((batch, SSM_GROUPS, GROUP_X, D_STATE), F32),
                   jax.ShapeDtypeStruct((batch, D_CONV - 1, CONV_DIM), F32)],
        scratch_shapes=[
            pltpu.VMEM((Q, D_MODEL), BF16),
            pltpu.VMEM((Q, LANES), F32),
            pltpu.VMEM((Q, LANES), F32),
            pltpu.VMEM((1, LANES, Q), F32),
            pltpu.VMEM((Q, GROUP_IN), F32),
            pltpu.VMEM((nb, SUBLANES + seq, GROUP_CONV), F32),
            pltpu.VMEM((SSM_GROUPS, Q, GROUP_X), F32),
            pltpu.VMEM((SSM_GROUPS, Q, GROUP_X), F32),
            pltpu.VMEM((SSM_GROUPS, GROUP_X, Q), F32),
            pltpu.VMEM((SSM_GROUPS, 2, Q, D_STATE), BF16),
            pltpu.VMEM((SSM_GROUPS, Q, GROUP_X), F32),
            pltpu.VMEM((nb, LANES, LANES), F32),
            pltpu.VMEM((Q, D_INNER), BF16),
        ],
        compiler_params=_params(2),
        name="mamba_sample",
    )(x2d, gmix, *w, conv_state, ssm_state, gx, wq)


def _pool_sample_body(x_ref, st_ref, gmix_ref, wpool_ref, pscale_ref, gx_ref, wq_ref,
                      y_ref, q_ref, pool_ref, ext_ref, *, seq, pos0):
    nb = x_ref.shape[0]
    n = nb * seq
    x = x_ref[...].reshape(n, D_MODEL)
    u = _rms(x, gmix_ref[...])
    hist0 = 2 * SUBLANES - POOL_BUF
    ext_ref[:, pl.ds(hist0, POOL_BUF), :] = st_ref[...]
    ext_ref[:, 2 * SUBLANES:2 * SUBLANES + seq, :] = u.reshape(nb, seq, D_MODEL)
    pos = pos0 + lax.broadcasted_iota(jnp.int32, (nb, seq, 1), 1).reshape(n, 1)
    pooled = _pool_windows(
        lambda k, cols: ext_ref[:, pl.ds(2 * SUBLANES - k, seq), cols].reshape(n, POOL_GROUP_DIM), u, pos, n)
    x1 = x + _pool_mix(pooled, wpool_ref, pscale_ref[...])
    pool_ref[...] = ext_ref[:, pl.ds(hist0 + seq, POOL_BUF), :]
    y_ref[...] = x1.reshape(nb, seq, D_MODEL)
    q_ref[...] = _dot(_rms(x1, gx_ref[...]).astype(BF16), wq_ref[...]).reshape(nb, seq, D_MODEL)


def _pool_sample_call(x, pool_state, gmix, wpool, pscale, gx, wq, pos0, nb=16):
    batch, seq, _ = x.shape
    xspec = pl.BlockSpec((nb, seq, D_MODEL), lambda i: (i, 0, 0))
    pspec = pl.BlockSpec((nb, POOL_BUF, D_MODEL), lambda i: (i, 0, 0))
    vec = _const_spec((1, D_MODEL))
    return pl.pallas_call(
        functools.partial(_pool_sample_body, seq=seq, pos0=pos0),
        grid=(batch // nb,),
        in_specs=[xspec, pspec, vec, _const_spec(wpool.shape), vec, vec, _const_spec(wq.shape)],
        out_specs=[xspec, xspec, pspec],
        out_shape=[jax.ShapeDtypeStruct(x.shape, F32), jax.ShapeDtypeStruct(x.shape, F32),
                   jax.ShapeDtypeStruct(pool_state.shape, F32)],
        scratch_shapes=[pltpu.VMEM((nb, 2 * SUBLANES + seq, D_MODEL), F32)],
        compiler_params=_params(1),
        name="pool_sample",
    )(x, pool_state, gmix, wpool, pscale, gx, wq)


def _attn_sample_body(q_ref, k_ref, v_ref, o_ref):
    nb = q_ref.shape[0]
    for bi in range(nb):
        o_ref[bi] = _attn_heads(
            q_ref[bi],
            lambda h: k_ref[0, bi, :, h * XATTN_HEAD_DIM:(h + 1) * XATTN_HEAD_DIM].astype(BF16),
            lambda h: v_ref[0, bi, :, h * XATTN_HEAD_DIM:(h + 1) * XATTN_HEAD_DIM].astype(BF16),
        )


def _attn_sample_call(q, k_cache, v_cache, layer, nb=4):
    batch, seq, _ = q.shape
    qspec = pl.BlockSpec((nb, seq, D_MODEL), lambda i: (i, 0, 0))
    kvspec = pl.BlockSpec((1, nb, N_MEM, D_MODEL), lambda i: (layer, i, 0, 0))
    return pl.pallas_call(
        _attn_sample_body,
        grid=(batch // nb,),
        in_specs=[qspec, kvspec, kvspec],
        out_specs=qspec,
        out_shape=jax.ShapeDtypeStruct(q.shape, F32),
        compiler_params=_params(1),
        name="attn_sample",
    )(q, k_cache, v_cache)


def _out_mlp_body(*refs, with_final):
    it = iter(refs)
    x_ref, o_ref, wo_ref, gm_ref, wup_ref, wdn_ref = (next(it) for _ in range(6))
    if with_final:
        gf_ref = next(it)
    y_ref = next(it)
    x = x_ref[...] + _dot(o_ref[...].astype(BF16), wo_ref[...])
    x = _mlp(x, gm_ref[...], wup_ref, wdn_ref)
    if with_final:
        x = _rms(x, gf_ref[...])
    y_ref[...] = x


def _out_mlp_call(x2d, o2d, wo, gm, wup, wdn, gfinal=None, tm=512):
    n = x2d.shape[0]
    tm = min(tm, n)
    row = pl.BlockSpec((tm, D_MODEL), lambda i: (i, 0))
    vec = _const_spec((1, D_MODEL))
    args = [x2d, o2d, wo, gm, wup, wdn]
    specs = [row, row, _const_spec(wo.shape), vec, _const_spec(wup.shape), _const_spec(wdn.shape)]
    if gfinal is not None:
        args.append(gfinal)
        specs.append(vec)
    return pl.pallas_call(
        functools.partial(_out_mlp_body, with_final=gfinal is not None),
        grid=(n // tm,),
        in_specs=specs,
        out_specs=row,
        out_shape=jax.ShapeDtypeStruct(x2d.shape, F32),
        compiler_params=_params(1),
        name="out_mlp",
    )(*args)


def _mamba_weights(w_in, conv_w, conv_b, dt_bias, a_log, d_skip, norm_gated, w_out):
    z, x = w_in[:, :D_INNER], w_in[:, D_INNER:2 * D_INNER]
    bm = w_in[:, 2 * D_INNER:2 * D_INNER + SSM_GROUPS * D_STATE]
    cm = w_in[:, 2 * D_INNER + SSM_GROUPS * D_STATE:D_INNER + CONV_DIM]
    wdt = w_in[:, D_INNER + CONV_DIM:]

    def grp(a, width):
        return a.reshape(a.shape[0], SSM_GROUPS, width)

    wg = jnp.concatenate([grp(z, GROUP_X), grp(x, GROUP_X), grp(bm, D_STATE), grp(cm, D_STATE)], axis=-1)
    wg = wg.transpose(1, 0, 2).astype(BF16)
    pad = LANES - SSM_HEADS
    wdt = jnp.pad(wdt, ((0, 0), (0, pad))).astype(BF16)

    def conv_grp(a):
        return jnp.concatenate([grp(a[:, :D_INNER], GROUP_X), grp(a[:, D_INNER:D_INNER + SSM_GROUPS * D_STATE], D_STATE),
                                grp(a[:, D_INNER + SSM_GROUPS * D_STATE:], D_STATE)], axis=-1).transpose(1, 0, 2)

    cw = conv_grp(conv_w)
    cb = conv_grp(conv_b[None, :])
    dtb = jnp.pad(dt_bias, (0, pad))[None, :]
    alog = jnp.pad(a_log, (0, pad))[None, :]
    dsk = jnp.repeat(d_skip, SSM_HEADDIM)[None, :]
    return (wg, wdt, cw, cb, dtb, alog, dsk, norm_gated[None, :], w_out.astype(BF16))


def kernel(x_prompt, x_sample, cache_mem_k, cache_mem_v, state_ssm, state_conv, state_pool, mem_prompt, norm_mix, norm_xattn, norm_mem, norm_mlp, norm_final, w_in, conv_w, conv_b, dt_bias, a_log, d_skip, norm_gated, w_out, w_pool, pool_scale, w_xq, w_xk, w_xv, w_xo, w_up, w_down):
    bp, seq_p, _ = x_prompt.shape
    bs, seq_s, _ = x_sample.shape
    depth = w_xq.shape[0]
    row = lambda a: a[None, :]

    wq, wk, wv, wo = (a.astype(BF16) for a in (w_xq, w_xk, w_xv, w_xo))
    wup, wdn = w_up.astype(BF16), w_down.astype(BF16)
    wpool = w_pool.astype(BF16)
    mw = _mamba_weights(w_in[0], conv_w[0], conv_b[0], dt_bias[0], a_log[0], d_skip[0], norm_gated[0], w_out[0])

    k_p, v_p, kb, vb = _kv_call(mem_prompt.reshape(bp * N_MEM, D_MODEL), norm_mem[:, None, :], wk, wv)
    kv_shape = (depth, bp, N_MEM, XATTN_HEADS, XATTN_HEAD_DIM)
    kb = kb.reshape(depth, bp, N_MEM, D_MODEL)
    vb = vb.reshape(depth, bp, N_MEM, D_MODEL)

    xp, ssm_p, conv_p = _mamba_prompt_call(x_prompt, row(norm_mix[0]), mw)
    xp = _prompt_layer_call(xp, 0, kb, vb, row(norm_xattn[0]), wq[0], wo[0], row(norm_mlp[0]), wup[0], wdn[0])
    y_prompt, pool_p = _prompt_layer_call(
        xp, 1, kb, vb, row(norm_xattn[1]), wq[1], wo[1], row(norm_mlp[1]), wup[1], wdn[1],
        pool=(row(norm_mix[1]), wpool[0], row(pool_scale[0])), gfinal=row(norm_final))

    kc = cache_mem_k.reshape(depth, bs, N_MEM, D_MODEL)
    vc = cache_mem_v.reshape(depth, bs, N_MEM, D_MODEL)
    xs = x_sample.reshape(bs * seq_s, D_MODEL)
    ssm_in = state_ssm[0].reshape(bs, SSM_GROUPS, GROUP_X, D_STATE)
    xs, q, ssm_s, conv_s = _mamba_sample_call(xs, row(norm_mix[0]), mw, state_conv[0], ssm_in,
                                              row(norm_xattn[0]), wq[0], seq_s)
    o = _attn_sample_call(q.reshape(bs, seq_s, D_MODEL), kc, vc, 0)
    xs = _out_mlp_call(xs, o.reshape(bs * seq_s, D_MODEL), wo[0], row(norm_mlp[0]), wup[0], wdn[0])
    xs, q, pool_s = _pool_sample_call(xs.reshape(bs, seq_s, D_MODEL), state_pool[0], row(norm_mix[1]), wpool[0],
                                      row(pool_scale[0]), row(norm_xattn[1]), wq[1], PAST_LEN)
    o = _attn_sample_call(q, kc, vc, 1)
    y_sample = _out_mlp_call(xs.reshape(bs * seq_s, D_MODEL), o.reshape(bs * seq_s, D_MODEL), wo[1],
                             row(norm_mlp[1]), wup[1], wdn[1], gfinal=row(norm_final))

    ssm_shape = (1, -1, SSM_HEADS, SSM_HEADDIM, D_STATE)
    return (y_prompt, y_sample.reshape(bs, seq_s, D_MODEL),
            k_p.reshape(kv_shape), v_p.reshape(kv_shape),
            ssm_p.reshape(ssm_shape), conv_p[None], pool_p[None],
            ssm_s.reshape(ssm_shape), conv_s[None], pool_s[None])
```

```python
import functools

import jax
import jax.numpy as jnp
from jax import lax
from jax.experimental import pallas as pl
from jax.experimental.pallas import tpu as pltpu

F32 = jnp.float32
BF16 = jnp.bfloat16

D_MODEL = 1024
D_INNER = 2048
SSM_HEADS = 32
SSM_HEADDIM = 64
SSM_GROUPS = 8
HEADS_PER_GROUP = 4
GROUP_X = HEADS_PER_GROUP * SSM_HEADDIM
D_STATE = 128
D_CONV = 4
CONV_DIM = D_INNER + 2 * SSM_GROUPS * D_STATE
GROUP_CONV = GROUP_X + 2 * D_STATE
GROUP_IN = GROUP_X + GROUP_CONV
POOL_WINDOWS = (2, 4, 8, 16)
POOL_GROUP_DIM = 256
POOL_BUF = 15
N_MEM = 256
XATTN_HEADS = 4
XATTN_HEAD_DIM = 256
D_FF = 4096
EPS = 1e-5
PAST_LEN = 16384

LANES = 128
SUBLANES = 8
Q = 128
FF_CHUNK = 1024
VMEM_LIMIT = 56 * 1024 * 1024

_NT = (((1,), (1,)), ((), ()))


def _params(n_grid):
    return pltpu.CompilerParams(dimension_semantics=("arbitrary",) * n_grid, vmem_limit_bytes=VMEM_LIMIT)


def _const_spec(shape):
    nd = len(shape)
    return pl.BlockSpec(shape, lambda *_: (0,) * nd, pipeline_mode=pl.Buffered(1))


def _dot(a, b):
    return jnp.dot(a, b, preferred_element_type=F32)


def _dot_nt(a, b):
    return lax.dot_general(a, b, _NT, preferred_element_type=F32)


def _rms(x, g):
    return x * lax.rsqrt(jnp.mean(x * x, axis=-1, keepdims=True) + EPS) * g


def _silu(x):
    return x * (1.0 / (1.0 + jnp.exp(-x)))


def _kv_body(mem_ref, g_ref, wk_ref, wv_ref, k_ref, v_ref, kb_ref, vb_ref):
    mn = _rms(mem_ref[...], g_ref[0]).astype(BF16)
    k = _dot(mn, wk_ref[0])
    v = _dot(mn, wv_ref[0])
    for h in range(XATTN_HEADS):
        sl = slice(h * XATTN_HEAD_DIM, (h + 1) * XATTN_HEAD_DIM)
        k_ref[0, :, h, :] = k[:, sl]
        v_ref[0, :, h, :] = v[:, sl]
    kb_ref[0] = k.astype(BF16)
    vb_ref[0] = v.astype(BF16)


def _kv_call(mem, g_mem, wk, wv, tm=512):
    n = mem.shape[0]
    depth = wk.shape[0]
    row = pl.BlockSpec((tm, D_MODEL), lambda l, t: (t, 0))
    wspec = pl.BlockSpec((1, D_MODEL, D_MODEL), lambda l, t: (l, 0, 0))
    ospec = pl.BlockSpec((1, tm, D_MODEL), lambda l, t: (l, t, 0))
    hspec = pl.BlockSpec((1, tm, XATTN_HEADS, XATTN_HEAD_DIM), lambda l, t: (l, t, 0, 0))
    return pl.pallas_call(
        _kv_body,
        grid=(depth, n // tm),
        in_specs=[row, pl.BlockSpec((1, 1, D_MODEL), lambda l, t: (l, 0, 0)), wspec, wspec],
        out_specs=[hspec, hspec, ospec, ospec],
        out_shape=[jax.ShapeDtypeStruct((depth, n, XATTN_HEADS, XATTN_HEAD_DIM), F32)] * 2
        + [jax.ShapeDtypeStruct((depth, n, D_MODEL), BF16)] * 2,
        compiler_params=_params(2),
        name="mem_kv",
    )(mem, g_mem, wk, wv)


def _attn_heads(q, k_of, v_of):
    outs = []
    for h in range(XATTN_HEADS):
        sl = slice(h * XATTN_HEAD_DIM, (h + 1) * XATTN_HEAD_DIM)
        s = _dot_nt(q[:, sl].astype(BF16), k_of(h)) * (XATTN_HEAD_DIM ** -0.5)
        p = jnp.exp(s - jnp.max(s, axis=-1, keepdims=True))
        p = p * (1.0 / jnp.sum(p, axis=-1, keepdims=True))
        outs.append(_dot(p.astype(BF16), v_of(h)))
    return jnp.concatenate(outs, axis=-1)


def _mlp(x, g, wup_ref, wdn_ref):
    hn = _rms(x, g).astype(BF16)
    acc = x
    for c in range(D_FF // FF_CHUNK):
        a = jnp.maximum(_dot(hn, wup_ref[:, c * FF_CHUNK:(c + 1) * FF_CHUNK]), 0.0)
        acc = acc + _dot((a * a).astype(BF16), wdn_ref[c * FF_CHUNK:(c + 1) * FF_CHUNK, :])
    return acc


def _pool_windows(load, u, pos, n):
    outs = []
    for gi, w in enumerate(POOL_WINDOWS):
        cols = slice(gi * POOL_GROUP_DIM, (gi + 1) * POOL_GROUP_DIM)
        acc = load(0, cols)
        for k in range(1, w):
            acc = acc + load(k, cols)
        cnt = jnp.minimum(pos + 1, w).astype(F32)
        outs.append(acc / cnt - u[:, cols])
    return outs


def _pool_mix(pooled, wpool_ref, scale):
    outs = [_dot(p.astype(BF16), wpool_ref[gi]) for gi, p in enumerate(pooled)]
    return jnp.concatenate(outs, axis=-1) * scale


def _prompt_layer_body(*refs, tq, with_pool, with_final):
    it = iter(refs)
    x_ref = next(it)
    if with_pool:
        gmix_ref, wpool_ref, pscale_ref = next(it), next(it), next(it)
    gx_ref, wq_ref, kb_ref, vb_ref, wo_ref = next(it), next(it), next(it), next(it), next(it)
    gm_ref, wup_ref, wdn_ref = next(it), next(it), next(it)
    if with_final:
        gf_ref = next(it)
    y_ref = next(it)
    if with_pool:
        pool_ref, ext_ref = next(it), next(it)

    x = x_ref[0]
    if with_pool:
        j = pl.program_id(1)

        @pl.when(j == 0)
        def _():
            ext_ref[0:2 * SUBLANES, :] = jnp.zeros((2 * SUBLANES, D_MODEL), F32)

        u = _rms(x, gmix_ref[...])
        ext_ref[pl.ds(2 * SUBLANES, tq), :] = u
        pos = j * tq + lax.broadcasted_iota(jnp.int32, (tq, 1), 0)
        pooled = _pool_windows(lambda k, cols: ext_ref[pl.ds(2 * SUBLANES - k, tq), cols], u, pos, tq)
        x = x + _pool_mix(pooled, wpool_ref, pscale_ref[...])
        pool_ref[0] = ext_ref[pl.ds(tq + 1, POOL_BUF), :]
        ext_ref[0:2 * SUBLANES, :] = ext_ref[pl.ds(tq, 2 * SUBLANES), :]

    q = _dot(_rms(x, gx_ref[...]).astype(BF16), wq_ref[...])
    o = _attn_heads(
        q,
        lambda h: kb_ref[0, 0, :, h * XATTN_HEAD_DIM:(h + 1) * XATTN_HEAD_DIM],
        lambda h: vb_ref[0, 0, :, h * XATTN_HEAD_DIM:(h + 1) * XATTN_HEAD_DIM],
    )
    x = x + _dot(o.astype(BF16), wo_ref[...])
    x = _mlp(x, gm_ref[...], wup_ref, wdn_ref)
    if with_final:
        x = _rms(x, gf_ref[...])
    y_ref[0] = x


def _prompt_layer_call(x, layer, kb, vb, gx, wq, wo, gm, wup, wdn, pool=None, gfinal=None, tq=512):
    b, seq, _ = x.shape
    with_pool, with_final = pool is not None, gfinal is not None
    xspec = pl.BlockSpec((1, tq, D_MODEL), lambda i, j: (i, j, 0))
    vec = _const_spec((1, D_MODEL))
    kvspec = pl.BlockSpec((1, 1, N_MEM, D_MODEL), lambda i, j: (layer, i, 0, 0))
    args, specs = [x], [xspec]
    if with_pool:
        gmix, wpool, pscale = pool
        args += [gmix, wpool, pscale]
        specs += [vec, _const_spec(wpool.shape), vec]
    args += [gx, wq, kb, vb, wo, gm, wup, wdn]
    specs += [vec, _const_spec(wq.shape), kvspec, kvspec, _const_spec(wo.shape), vec,
              _const_spec(wup.shape), _const_spec(wdn.shape)]
    if with_final:
        args.append(gfinal)
        specs.append(vec)
    out_shape = [jax.ShapeDtypeStruct(x.shape, F32)]
    out_specs = [xspec]
    scratch = []
    if with_pool:
        out_shape.append(jax.ShapeDtypeStruct((b, POOL_BUF, D_MODEL), F32))
        out_specs.append(pl.BlockSpec((1, POOL_BUF, D_MODEL), lambda i, j: (i, 0, 0)))
        scratch.append(pltpu.VMEM((tq + 2 * SUBLANES, D_MODEL), F32))
    outs = pl.pallas_call(
        functools.partial(_prompt_layer_body, tq=tq, with_pool=with_pool, with_final=with_final),
        grid=(b, seq // tq),
        in_specs=specs,
        out_specs=out_specs,
        out_shape=out_shape,
        scratch_shapes=scratch,
        compiler_params=_params(2),
        name="prompt_layer_pool" if with_pool else "prompt_layer",
    )(*args)
    return outs if with_pool else outs[0]


def _dt_stage(xn_ref, wdt_ref, dtb_ref, alog_ref, dt_ref, acs_ref, acsT_ref, mask, n_chunks):
    z = _dot(xn_ref[...], wdt_ref[...]) + dtb_ref[...]
    dt_ref[...] = jnp.maximum(z, 0.0) + jnp.log1p(jnp.exp(-jnp.abs(z)))
    a = -jnp.exp(alog_ref[...])
    tri = mask.astype(F32)
    for c in range(n_chunks):
        rows = slice(c * Q, (c + 1) * Q)
        acs = jnp.dot(tri, dt_ref[rows, :] * a, precision=lax.Precision.HIGHEST, preferred_element_type=F32)
        acs_ref[rows, :] = acs
        acsT_ref[c] = acs.T


def _conv_rows(load_shifted, cw_ref, cb_ref, g):
    acc = cb_ref[g]
    for k in range(D_CONV):
        acc = acc + cw_ref[g, k:k + 1, :] * load_shifted(k)
    return _silu(acc)


def _expand_heads(cols):
    half = lax.broadcasted_iota(jnp.int32, (Q, LANES), 1) < SSM_HEADDIM
    return jnp.concatenate([jnp.where(half, cols[0], cols[1]), jnp.where(half, cols[2], cols[3])], axis=1)


def _ssd_group(g, xg, bg, cg, dt_c, acs_c, acsT_c, seg_last, dskip_g, mask):
    heads = [HEADS_PER_GROUP * g + r for r in range(HEADS_PER_GROUP)]
    acs_cols = [jnp.broadcast_to(acs_c[:, h:h + 1], (Q, LANES)) for h in heads]
    dt_cols = [jnp.broadcast_to(dt_c[:, h:h + 1], (Q, LANES)) for h in heads]
    acs_x = _expand_heads(acs_cols)
    xdt = xg * _expand_heads(dt_cols)
    xdt_b = xdt.astype(BF16)
    cb = _dot_nt(cg, bg)
    lane_head = _div_pow2(lax.broadcasted_iota(jnp.int32, (Q, GROUP_X), 1), SSM_HEADDIM)
    yd = None
    for r, h in enumerate(heads):
        dec = jnp.exp(jnp.where(mask, acs_cols[r] - acsT_c[h:h + 1, :], -jnp.inf))
        yr = _dot((cb * dec).astype(BF16), xdt_b)
        yd = yr if r == 0 else jnp.where(lane_head == r, yr, yd)
    ydx = yd + dskip_g * xg
    xdd = xdt * jnp.exp(seg_last(acs_x) - acs_x)
    return ydx, jnp.exp(acs_x), xdd


def _gate_norm(y, z, ng):
    yg = y * _silu(z)
    return yg * lax.rsqrt(jnp.mean(yg * yg, axis=-1, keepdims=True) + EPS) * ng


def _div_pow2(v, d):
    assert d & (d - 1) == 0
    return lax.shift_right_logical(v, d.bit_length() - 1)


def _causal_mask(block):
    li = lax.broadcasted_iota(jnp.int32, (Q, Q), 0)
    si = lax.broadcasted_iota(jnp.int32, (Q, Q), 1)
    m = li >= si
    if block is not None:
        m = jnp.logical_and(m, _div_pow2(li, block) == _div_pow2(si, block))
    return m


def _head_rows(vals):
    return jnp.concatenate([jnp.broadcast_to(v, (SSM_HEADDIM, LANES)) for v in vals], axis=0)


def _mamba_prompt_body(x_ref, gmix_ref, wg_ref, wdt_ref, cw_ref, cb_ref, dtb_ref, alog_ref, dsk_ref, ng_ref, wout_ref,
                       y_ref, ssm_ref, conv_ref,
                       xn_ref, dt_ref, acs_ref, acsT_ref, zxg_ref, xpad_ref, act_ref, hist_ref, h_ref, yg_ref, *, tq):
    j = pl.program_id(1)
    n_chunks = tq // Q

    @pl.when(j == 0)
    def _():
        hist_ref[...] = jnp.zeros_like(hist_ref)
        h_ref[...] = jnp.zeros_like(h_ref)

    mask = _causal_mask(None)
    xn_ref[...] = _rms(x_ref[0], gmix_ref[...]).astype(BF16)
    _dt_stage(xn_ref, wdt_ref, dtb_ref, alog_ref, dt_ref, acs_ref, acsT_ref, mask, n_chunks)

    for g in range(SSM_GROUPS):
        zxg_ref[...] = _dot(xn_ref[...], wg_ref[g])
        xpad_ref[0:SUBLANES, :] = hist_ref[g]
        xpad_ref[pl.ds(SUBLANES, tq), :] = zxg_ref[:, GROUP_X:GROUP_IN]

        for c in range(n_chunks):
            act_ref[c * Q:(c + 1) * Q, :] = _conv_rows(
                lambda k: xpad_ref[pl.ds(c * Q + SUBLANES - (D_CONV - 1) + k, Q), :], cw_ref, cb_ref, g)
        hist_ref[g] = xpad_ref[pl.ds(tq, SUBLANES), :]

        def ssd_chunk(c, carry):
            rows = pl.ds(pl.multiple_of(c * Q, Q), Q)
            xg = act_ref[rows, 0:GROUP_X]
            bg = act_ref[rows, GROUP_X:GROUP_X + D_STATE].astype(BF16)
            cg = act_ref[rows, GROUP_X + D_STATE:GROUP_CONV].astype(BF16)
            acs_c = acs_ref[rows, :]
            acsT_c = acsT_ref[c]
            ydx, eacs, xdd = _ssd_group(
                g, xg, bg, cg, dt_ref[rows, :], acs_c, acsT_c,
                lambda v: v[Q - 1:Q, :], dsk_ref[:, g * GROUP_X:(g + 1) * GROUP_X], mask)
            hg = h_ref[g]
            y = ydx + eacs * _dot_nt(cg, hg.astype(BF16))
            yg_ref[rows, g * GROUP_X:(g + 1) * GROUP_X] = _gate_norm(
                y, zxg_ref[rows, 0:GROUP_X], ng_ref[:, g * GROUP_X:(g + 1) * GROUP_X]).astype(BF16)
            scale = _head_rows([jnp.exp(acsT_c[HEADS_PER_GROUP * g + r:HEADS_PER_GROUP * g + r + 1, Q - 1:Q])
                                for r in range(HEADS_PER_GROUP)])
            h_ref[g] = scale * hg + _dot(xdd.T.astype(BF16), bg)
            return carry

        lax.fori_loop(0, n_chunks, ssd_chunk, 0)

        hist = hist_ref[g]
        conv_ref[0, :, g * GROUP_X:(g + 1) * GROUP_X] = hist[SUBLANES - (D_CONV - 1):, 0:GROUP_X]
        conv_ref[0, :, D_INNER + g * D_STATE:D_INNER + (g + 1) * D_STATE] = (
            hist[SUBLANES - (D_CONV - 1):, GROUP_X:GROUP_X + D_STATE])
        conv_ref[0, :, D_INNER + (SSM_GROUPS + g) * D_STATE:D_INNER + (SSM_GROUPS + g + 1) * D_STATE] = (
            hist[SUBLANES - (D_CONV - 1):, GROUP_X + D_STATE:GROUP_CONV])

    y_ref[0] = x_ref[0] + _dot(yg_ref[...], wout_ref[...])

    @pl.when(j == pl.num_programs(1) - 1)
    def _():
        ssm_ref[0] = h_ref[...]


def _mamba_weight_specs(w):
    return [_const_spec(a.shape) for a in w]


def _mamba_prompt_call(x, gmix, w, tq=256):
    b, seq, _ = x.shape
    xspec = pl.BlockSpec((1, tq, D_MODEL), lambda i, j: (i, j, 0))
    n_chunks = tq // Q
    return pl.pallas_call(
        functools.partial(_mamba_prompt_body, tq=tq),
        grid=(b, seq // tq),
        in_specs=[xspec, _const_spec((1, D_MODEL))] + _mamba_weight_specs(w),
        out_specs=[xspec,
                   pl.BlockSpec((1, SSM_GROUPS, GROUP_X, D_STATE), lambda i, j: (i, 0, 0, 0)),
                   pl.BlockSpec((1, D_CONV - 1, CONV_DIM), lambda i, j: (i, 0, 0))],
        out_shape=[jax.ShapeDtypeStruct(x.shape, F32),
                   jax.ShapeDtypeStruct((b, SSM_GROUPS, GROUP_X, D_STATE), F32),
                   jax.ShapeDtypeStruct((b, D_CONV - 1, CONV_DIM), F32)],
        scratch_shapes=[
            pltpu.VMEM((tq, D_MODEL), BF16),
            pltpu.VMEM((tq, LANES), F32),
            pltpu.VMEM((tq, LANES), F32),
            pltpu.VMEM((n_chunks, LANES, Q), F32),
            pltpu.VMEM((tq, GROUP_IN), F32),
            pltpu.VMEM((tq + SUBLANES, GROUP_CONV), F32),
            pltpu.VMEM((tq, GROUP_CONV), F32),
            pltpu.VMEM((SSM_GROUPS, SUBLANES, GROUP_CONV), F32),
            pltpu.VMEM((SSM_GROUPS, GROUP_X, D_STATE), F32),
            pltpu.VMEM((tq, D_INNER), BF16),
        ],
        compiler_params=_params(2),
        name="mamba_prompt",
    )(x, gmix, *w)


def _mamba_sample_body(x_ref, gmix_ref, wg_ref, wdt_ref, cw_ref, cb_ref, dtb_ref, alog_ref, dsk_ref, ng_ref, wout_ref,
                       cst_ref, sst_ref, gx_ref, wq_ref,
                       y_ref, q_ref, ssm_ref, conv_ref,
                       xn_ref, dt_ref, acs_ref, acsT_ref, zxg_ref, xpad_ref,
                       ydx_ref, eacs_ref, xddT_ref, bc_ref, z_ref, scale_ref, yg_ref, *, seq, sub_b):
    s = pl.program_id(1)
    nb = Q // seq
    mask = _causal_mask(seq)

    @pl.when(s == 0)
    def _():
        xn_ref[...] = _rms(x_ref[...], gmix_ref[...]).astype(BF16)
        _dt_stage(xn_ref, wdt_ref, dtb_ref, alog_ref, dt_ref, acs_ref, acsT_ref, mask, 1)
        acsT = acsT_ref[0]
        for bi in range(nb):
            scale_ref[bi] = jnp.broadcast_to(jnp.exp(acsT[:, (bi + 1) * seq - 1:(bi + 1) * seq]), (LANES, LANES))

        def seg_last(v):
            v3 = v.reshape(nb, seq, v.shape[-1])
            return jnp.broadcast_to(v3[:, seq - 1:seq, :], v3.shape).reshape(v.shape)

        for g in range(SSM_GROUPS):
            zxg_ref[...] = _dot(xn_ref[...], wg_ref[g])
            col_slices = (
                (slice(g * GROUP_X, (g + 1) * GROUP_X), slice(0, GROUP_X)),
                (slice(D_INNER + g * D_STATE, D_INNER + (g + 1) * D_STATE), slice(GROUP_X, GROUP_X + D_STATE)),
                (slice(D_INNER + (SSM_GROUPS + g) * D_STATE, D_INNER + (SSM_GROUPS + g + 1) * D_STATE),
                 slice(GROUP_X + D_STATE, GROUP_CONV)),
            )
            for src, dst in col_slices:
                xpad_ref[:, SUBLANES - (D_CONV - 1):SUBLANES, dst] = cst_ref[:, :, src]
            xpad_ref[:, SUBLANES:SUBLANES + seq, :] = zxg_ref[:, GROUP_X:GROUP_IN].reshape(nb, seq, GROUP_CONV)
            act = _conv_rows(
                lambda k: xpad_ref[:, pl.ds(SUBLANES - (D_CONV - 1) + k, seq), :].reshape(Q, GROUP_CONV),
                cw_ref, cb_ref, g)
            for src, dst in col_slices:
                conv_ref[:, :, src] = xpad_ref[:, pl.ds(SUBLANES + seq - (D_CONV - 1), D_CONV - 1), dst]
            xg = act[:, 0:GROUP_X]
            bg = act[:, GROUP_X:GROUP_X + D_STATE].astype(BF16)
            cg = act[:, GROUP_X + D_STATE:GROUP_CONV].astype(BF16)
            ydx, eacs, xdd = _ssd_group(
                g, xg, bg, cg, dt_ref[...], acs_ref[...], acsT, seg_last,
                dsk_ref[:, g * GROUP_X:(g + 1) * GROUP_X], mask)
            ydx_ref[g] = ydx
            eacs_ref[g] = eacs
            xddT_ref[g] = xdd.T
            bc_ref[g, 0] = bg
            bc_ref[g, 1] = cg
            z_ref[g] = zxg_ref[:, 0:GROUP_X]

    rows_per = sub_b * seq
    r0 = pl.multiple_of(s * rows_per, rows_per)
    rows = pl.ds(r0, rows_per)
    row_b = _div_pow2(lax.broadcasted_iota(jnp.int32, (rows_per, GROUP_X), 0), seq)
    lane_b = _div_pow2(lax.broadcasted_iota(jnp.int32, (GROUP_X, Q), 1), seq)
    for g in range(SSM_GROUPS):
        cg = bc_ref[g, 1, rows, :]
        bg = bc_ref[g, 0]
        xddT = xddT_ref[g]
        yoff = jnp.zeros((rows_per, GROUP_X), F32)
        for bi in range(sub_b):
            hg = sst_ref[bi, g]
            yoff = jnp.where(row_b == bi, _dot_nt(cg, hg.astype(BF16)), yoff)
            sc = scale_ref[s * sub_b + bi]
            scale = _head_rows([sc[HEADS_PER_GROUP * g + r:HEADS_PER_GROUP * g + r + 1, :]
                                for r in range(HEADS_PER_GROUP)])
            xm = jnp.where(lane_b == s * sub_b + bi, xddT, 0.0).astype(BF16)
            ssm_ref[bi, g] = scale * hg + _dot(xm, bg)
        y = ydx_ref[g, rows, :] + eacs_ref[g, rows, :] * yoff
        yg_ref[rows, g * GROUP_X:(g + 1) * GROUP_X] = _gate_norm(
            y, z_ref[g, rows, :], ng_ref[:, g * GROUP_X:(g + 1) * GROUP_X]).astype(BF16)

    @pl.when(s == pl.num_programs(1) - 1)
    def _():
        x1 = x_ref[...] + _dot(yg_ref[...], wout_ref[...])
        y_ref[...] = x1
        q_ref[...] = _dot(_rms(x1, gx_ref[...]).astype(BF16), wq_ref[...])


def _mamba_sample_call(x2d, gmix, w, conv_state, ssm_state, gx, wq, seq, sub_b=4):
    n = x2d.shape[0]
    batch = n // seq
    nb = Q // seq
    n_sub = nb // sub_b
    xspec = pl.BlockSpec((Q, D_MODEL), lambda i, s: (i, 0))
    stspec = pl.BlockSpec((sub_b, SSM_GROUPS, GROUP_X, D_STATE), lambda i, s: (i * n_sub + s, 0, 0, 0))
    cspec = pl.BlockSpec((nb, D_CONV - 1, CONV_DIM), lambda i, s: (i, 0, 0))
    return pl.pallas_call(
        functools.partial(_mamba_sample_body, seq=seq, sub_b=sub_b),
        grid=(n // Q, n_sub),
        in_specs=[xspec, _const_spec((1, D_MODEL))] + _mamba_weight_specs(w)
        + [cspec, stspec, _const_spec((1, D_MODEL)), _const_spec(wq.shape)],
        out_specs=[xspec, xspec, stspec, cspec],
        out_shape=[jax.ShapeDtypeStruct(x2d.shape, F32), jax.ShapeDtypeStruct(x2d.shape, F32),
                   jax.ShapeDtypeStruct((batch, SSM_GROUPS, GROUP_X, D_STATE), F32),
                   jax.ShapeDtypeStruct((batch, D_CONV - 1, CONV_DIM), F32)],
        scratch_shapes=[
            pltpu.VMEM((Q, D_MODEL), BF16),
            pltpu.VMEM((Q, LANES), F32),
            pltpu.VMEM((Q, LANES), F32),
            pltpu.VMEM((1, LANES, Q), F32),
            pltpu.VMEM((Q, GROUP_IN), F32),
            pltpu.VMEM((nb, SUBLANES + seq, GROUP_CONV), F32),
            pltpu.VMEM((SSM_GROUPS, Q, GROUP_X), F32),
            pltpu.VMEM((SSM_GROUPS, Q, GROUP_X), F32),
            pltpu.VMEM((SSM_GROUPS, GROUP_X, Q), F32),
            pltpu.VMEM((SSM_GROUPS, 2, Q, D_STATE), BF16),
            pltpu.VMEM((SSM_GROUPS, Q, GROUP_X), F32),
            pltpu.VMEM((nb, LANES, LANES), F32),
            pltpu.VMEM((Q, D_INNER), BF16),
        ],
        compiler_params=_params(2),
        name="mamba_sample",
    )(x2d, gmix, *w, conv_state, ssm_state, gx, wq)


def _pool_sample_body(x_ref, st_ref, gmix_ref, wpool_ref, pscale_ref, gx_ref, wq_ref,
                      y_ref, q_ref, pool_ref, ext_ref, *, seq, pos0):
    nb = x_ref.shape[0]
    n = nb * seq
    x = x_ref[...].reshape(n, D_MODEL)
    u = _rms(x, gmix_ref[...])
    hist0 = 2 * SUBLANES - POOL_BUF
    ext_ref[:, pl.ds(hist0, POOL_BUF), :] = st_ref[...]
    ext_ref[:, 2 * SUBLANES:2 * SUBLANES + seq, :] = u.reshape(nb, seq, D_MODEL)
    pos = pos0 + lax.broadcasted_iota(jnp.int32, (nb, seq, 1), 1).reshape(n, 1)
    pooled = _pool_windows(
        lambda k, cols: ext_ref[:, pl.ds(2 * SUBLANES - k, seq), cols].reshape(n, POOL_GROUP_DIM), u, pos, n)
    x1 = x + _pool_mix(pooled, wpool_ref, pscale_ref[...])
    pool_ref[...] = ext_ref[:, pl.ds(hist0 + seq, POOL_BUF), :]
    y_ref[...] = x1.reshape(nb, seq, D_MODEL)
    q_ref[...] = _dot(_rms(x1, gx_ref[...]).astype(BF16), wq_ref[...]).reshape(nb, seq, D_MODEL)


def _pool_sample_call(x, pool_state, gmix, wpool, pscale, gx, wq, pos0, nb=16):
    batch, seq, _ = x.shape
    xspec = pl.BlockSpec((nb, seq, D_MODEL), lambda i: (i, 0, 0))
    pspec = pl.BlockSpec((nb, POOL_BUF, D_MODEL), lambda i: (i, 0, 0))
    vec = _const_spec((1, D_MODEL))
    return pl.pallas_call(
        functools.partial(_pool_sample_body, seq=seq, pos0=pos0),
        grid=(batch // nb,),
        in_specs=[xspec, pspec, vec, _const_spec(wpool.shape), vec, vec, _const_spec(wq.shape)],
        out_specs=[xspec, xspec, pspec],
        out_shape=[jax.ShapeDtypeStruct(x.shape, F32), jax.ShapeDtypeStruct(x.shape, F32),
                   jax.ShapeDtypeStruct(pool_state.shape, F32)],
        scratch_shapes=[pltpu.VMEM((nb, 2 * SUBLANES + seq, D_MODEL), F32)],
        compiler_params=_params(1),
        name="pool_sample",
    )(x, pool_state, gmix, wpool, pscale, gx, wq)


def _attn_sample_body(q_ref, k_ref, v_ref, o_ref):
    nb = q_ref.shape[0]
    for bi in range(nb):
        o_ref[bi] = _attn_heads(
            q_ref[bi],
            lambda h: k_ref[0, bi, :, h, :].astype(BF16),
            lambda h: v_ref[0, bi, :, h, :].astype(BF16),
        )


def _attn_sample_call(q, k_cache, v_cache, layer, nb=4):
    batch, seq, _ = q.shape
    qspec = pl.BlockSpec((nb, seq, D_MODEL), lambda i: (i, 0, 0))
    kvspec = pl.BlockSpec((1, nb, N_MEM, XATTN_HEADS, XATTN_HEAD_DIM), lambda i: (layer, i, 0, 0, 0))
    return pl.pallas_call(
        _attn_sample_body,
        grid=(batch // nb,),
        in_specs=[qspec, kvspec, kvspec],
        out_specs=qspec,
        out_shape=jax.ShapeDtypeStruct(q.shape, F32),
        compiler_params=_params(1),
        name="attn_sample",
    )(q, k_cache, v_cache)


def _out_mlp_body(*refs, with_final):
    it = iter(refs)
    x_ref, o_ref, wo_ref, gm_ref, wup_ref, wdn_ref = (next(it) for _ in range(6))
    if with_final:
        gf_ref = next(it)
    y_ref = next(it)
    x = x_ref[...] + _dot(o_ref[...].astype(BF16), wo_ref[...])
    x = _mlp(x, gm_ref[...], wup_ref, wdn_ref)
    if with_final:
        x = _rms(x, gf_ref[...])
    y_ref[...] = x


def _out_mlp_call(x2d, o2d, wo, gm, wup, wdn, gfinal=None, tm=512):
    n = x2d.shape[0]
    tm = min(tm, n)
    row = pl.BlockSpec((tm, D_MODEL), lambda i: (i, 0))
    vec = _const_spec((1, D_MODEL))
    args = [x2d, o2d, wo, gm, wup, wdn]
    specs = [row, row, _const_spec(wo.shape), vec, _const_spec(wup.shape), _const_spec(wdn.shape)]
    if gfinal is not None:
        args.append(gfinal)
        specs.append(vec)
    return pl.pallas_call(
        functools.partial(_out_mlp_body, with_final=gfinal is not None),
        grid=(n // tm,),
        in_specs=specs,
        out_specs=row,
        out_shape=jax.ShapeDtypeStruct(x2d.shape, F32),
        compiler_params=_params(1),
        name="out_mlp",
    )(*args)


def _mamba_weights(w_in, conv_w, conv_b, dt_bias, a_log, d_skip, norm_gated, w_out):
    z, x = w_in[:, :D_INNER], w_in[:, D_INNER:2 * D_INNER]
    bm = w_in[:, 2 * D_INNER:2 * D_INNER + SSM_GROUPS * D_STATE]
    cm = w_in[:, 2 * D_INNER + SSM_GROUPS * D_STATE:D_INNER + CONV_DIM]
    wdt = w_in[:, D_INNER + CONV_DIM:]

    def grp(a, width):
        return a.reshape(a.shape[0], SSM_GROUPS, width)

    wg = jnp.concatenate([grp(z, GROUP_X), grp(x, GROUP_X), grp(bm, D_STATE), grp(cm, D_STATE)], axis=-1)
    wg = wg.transpose(1, 0, 2).astype(BF16)
    pad = LANES - SSM_HEADS
    wdt = jnp.pad(wdt, ((0, 0), (0, pad))).astype(BF16)

    def conv_grp(a):
        return jnp.concatenate([grp(a[:, :D_INNER], GROUP_X), grp(a[:, D_INNER:D_INNER + SSM_GROUPS * D_STATE], D_STATE),
                                grp(a[:, D_INNER + SSM_GROUPS * D_STATE:], D_STATE)], axis=-1).transpose(1, 0, 2)

    cw = conv_grp(conv_w)
    cb = conv_grp(conv_b[None, :])
    dtb = jnp.pad(dt_bias, (0, pad))[None, :]
    alog = jnp.pad(a_log, (0, pad))[None, :]
    dsk = jnp.repeat(d_skip, SSM_HEADDIM)[None, :]
    return (wg, wdt, cw, cb, dtb, alog, dsk, norm_gated[None, :], w_out.astype(BF16))


def kernel(x_prompt, x_sample, cache_mem_k, cache_mem_v, state_ssm, state_conv, state_pool, mem_prompt, norm_mix, norm_xattn, norm_mem, norm_mlp, norm_final, w_in, conv_w, conv_b, dt_bias, a_log, d_skip, norm_gated, w_out, w_pool, pool_scale, w_xq, w_xk, w_xv, w_xo, w_up, w_down):
    bp, seq_p, _ = x_prompt.shape
    bs, seq_s, _ = x_sample.shape
    depth = w_xq.shape[0]
    row = lambda a: a[None, :]

    wq, wk, wv, wo = (a.astype(BF16) for a in (w_xq, w_xk, w_xv, w_xo))
    wup, wdn = w_up.astype(BF16), w_down.astype(BF16)
    wpool = w_pool.astype(BF16)
    mw = _mamba_weights(w_in[0], conv_w[0], conv_b[0], dt_bias[0], a_log[0], d_skip[0], norm_gated[0], w_out[0])

    k_p, v_p, kb, vb = _kv_call(mem_prompt.reshape(bp * N_MEM, D_MODEL), norm_mem[:, None, :], wk, wv)
    kv_shape = (depth, bp, N_MEM, XATTN_HEADS, XATTN_HEAD_DIM)
    kb = kb.reshape(depth, bp, N_MEM, D_MODEL)
    vb = vb.reshape(depth, bp, N_MEM, D_MODEL)

    xp, ssm_p, conv_p = _mamba_prompt_call(x_prompt, row(norm_mix[0]), mw)
    xp = _prompt_layer_call(xp, 0, kb, vb, row(norm_xattn[0]), wq[0], wo[0], row(norm_mlp[0]), wup[0], wdn[0])
    y_prompt, pool_p = _prompt_layer_call(
        xp, 1, kb, vb, row(norm_xattn[1]), wq[1], wo[1], row(norm_mlp[1]), wup[1], wdn[1],
        pool=(row(norm_mix[1]), wpool[0], row(pool_scale[0])), gfinal=row(norm_final))

    kc, vc = cache_mem_k, cache_mem_v
    xs = x_sample.reshape(bs * seq_s, D_MODEL)
    ssm_in = state_ssm[0].reshape(bs, SSM_GROUPS, GROUP_X, D_STATE)
    xs, q, ssm_s, conv_s = _mamba_sample_call(xs, row(norm_mix[0]), mw, state_conv[0], ssm_in,
                                              row(norm_xattn[0]), wq[0], seq_s)
    o = _attn_sample_call(q.reshape(bs, seq_s, D_MODEL), kc, vc, 0)
    xs = _out_mlp_call(xs, o.reshape(bs * seq_s, D_MODEL), wo[0], row(norm_mlp[0]), wup[0], wdn[0])
    xs, q, pool_s = _pool_sample_call(xs.reshape(bs, seq_s, D_MODEL), state_pool[0], row(norm_mix[1]), wpool[0],
                                      row(pool_scale[0]), row(norm_xattn[1]), wq[1], PAST_LEN)
    o = _attn_sample_call(q, kc, vc, 1)
    y_sample = _out_mlp_call(xs.reshape(bs * seq_s, D_MODEL), o.reshape(bs * seq_s, D_MODEL), wo[1],
                             row(norm_mlp[1]), wup[1], wdn[1], gfinal=row(norm_final))

    ssm_shape = (1, -1, SSM_HEADS, SSM_HEADDIM, D_STATE)
    return (y_prompt, y_sample.reshape(bs, seq_s, D_MODEL),
            k_p.reshape(kv_shape), v_p.reshape(kv_shape),
            ssm_p.reshape(ssm_shape), conv_p[None], pool_p[None],
            ssm_s.reshape(ssm_shape), conv_s[None], pool_s[None])
```

```python
import functools

import jax
import jax.numpy as jnp
from jax import lax
from jax.experimental import pallas as pl
from jax.experimental.pallas import tpu as pltpu

F32 = jnp.float32
BF16 = jnp.bfloat16

D_MODEL = 1024
D_INNER = 2048
SSM_HEADS = 32
SSM_HEADDIM = 64
SSM_GROUPS = 8
HEADS_PER_GROUP = 4
GROUP_X = HEADS_PER_GROUP * SSM_HEADDIM
D_STATE = 128
D_CONV = 4
CONV_DIM = D_INNER + 2 * SSM_GROUPS * D_STATE
GROUP_CONV = GROUP_X + 2 * D_STATE
GROUP_IN = GROUP_X + GROUP_CONV
POOL_WINDOWS = (2, 4, 8, 16)
POOL_GROUP_DIM = 256
POOL_BUF = 15
N_MEM = 256
XATTN_HEADS = 4
XATTN_HEAD_DIM = 256
D_FF = 4096
EPS = 1e-5
PAST_LEN = 16384

LANES = 128
SUBLANES = 8
Q = 128
FF_CHUNK = 1024
VMEM_LIMIT = 56 * 1024 * 1024

_NT = (((1,), (1,)), ((), ()))


def _params(n_grid):
    return pltpu.CompilerParams(dimension_semantics=("arbitrary",) * n_grid, vmem_limit_bytes=VMEM_LIMIT)


def _const_spec(shape):
    nd = len(shape)
    return pl.BlockSpec(shape, lambda *_: (0,) * nd, pipeline_mode=pl.Buffered(1))


def _dot(a, b):
    return jnp.dot(a, b, preferred_element_type=F32)


def _dot_nt(a, b):
    return lax.dot_general(a, b, _NT, preferred_element_type=F32)


def _rms(x, g):
    return x * lax.rsqrt(jnp.mean(x * x, axis=-1, keepdims=True) + EPS) * g


def _silu(x):
    return x * (1.0 / (1.0 + jnp.exp(-x)))


def _kv_body(mem_ref, g_ref, wk_ref, wv_ref, k_ref, v_ref, kb_ref, vb_ref):
    mn = _rms(mem_ref[...], g_ref[0]).astype(BF16)
    k = _dot(mn, wk_ref[0])
    v = _dot(mn, wv_ref[0])
    for h in range(XATTN_HEADS):
        sl = slice(h * XATTN_HEAD_DIM, (h + 1) * XATTN_HEAD_DIM)
        k_ref[0, :, h, :] = k[:, sl]
        v_ref[0, :, h, :] = v[:, sl]
    kb_ref[0] = k.astype(BF16)
    vb_ref[0] = v.astype(BF16)


def _kv_call(mem, g_mem, wk, wv, tm=512):
    n = mem.shape[0]
    depth = wk.shape[0]
    row = pl.BlockSpec((tm, D_MODEL), lambda l, t: (t, 0))
    wspec = pl.BlockSpec((1, D_MODEL, D_MODEL), lambda l, t: (l, 0, 0))
    ospec = pl.BlockSpec((1, tm, D_MODEL), lambda l, t: (l, t, 0))
    hspec = pl.BlockSpec((1, tm, XATTN_HEADS, XATTN_HEAD_DIM), lambda l, t: (l, t, 0, 0))
    return pl.pallas_call(
        _kv_body,
        grid=(depth, n // tm),
        in_specs=[row, pl.BlockSpec((1, 1, D_MODEL), lambda l, t: (l, 0, 0)), wspec, wspec],
        out_specs=[hspec, hspec, ospec, ospec],
        out_shape=[jax.ShapeDtypeStruct((depth, n, XATTN_HEADS, XATTN_HEAD_DIM), F32)] * 2
        + [jax.ShapeDtypeStruct((depth, n, D_MODEL), BF16)] * 2,
        compiler_params=_params(2),
        name="mem_kv",
    )(mem, g_mem, wk, wv)


def _attn_heads(q, k_of, v_of):
    outs = []
    for h in range(XATTN_HEADS):
        sl = slice(h * XATTN_HEAD_DIM, (h + 1) * XATTN_HEAD_DIM)
        s = _dot_nt(q[:, sl].astype(BF16), k_of(h)) * (XATTN_HEAD_DIM ** -0.5)
        p = jnp.exp(s - jnp.max(s, axis=-1, keepdims=True))
        p = p * (1.0 / jnp.sum(p, axis=-1, keepdims=True))
        outs.append(_dot(p.astype(BF16), v_of(h)))
    return jnp.concatenate(outs, axis=-1)


def _mlp(x, g, wup_ref, wdn_ref):
    hn = _rms(x, g).astype(BF16)
    acc = x
    for c in range(D_FF // FF_CHUNK):
        a = jnp.maximum(_dot(hn, wup_ref[:, c * FF_CHUNK:(c + 1) * FF_CHUNK]), 0.0)
        acc = acc + _dot((a * a).astype(BF16), wdn_ref[c * FF_CHUNK:(c + 1) * FF_CHUNK, :])
    return acc


def _pool_windows(load, u, pos, n):
    outs = []
    for gi, w in enumerate(POOL_WINDOWS):
        cols = slice(gi * POOL_GROUP_DIM, (gi + 1) * POOL_GROUP_DIM)
        acc = load(0, cols)
        for k in range(1, w):
            acc = acc + load(k, cols)
        cnt = jnp.minimum(pos + 1, w).astype(F32)
        outs.append(acc / cnt - u[:, cols])
    return outs


def _pool_mix(pooled, wpool_ref, scale):
    outs = [_dot(p.astype(BF16), wpool_ref[gi]) for gi, p in enumerate(pooled)]
    return jnp.concatenate(outs, axis=-1) * scale


def _prompt_layer_body(*refs, tq, with_pool, with_final):
    it = iter(refs)
    x_ref = next(it)
    if with_pool:
        gmix_ref, wpool_ref, pscale_ref = next(it), next(it), next(it)
    gx_ref, wq_ref, kb_ref, vb_ref, wo_ref = next(it), next(it), next(it), next(it), next(it)
    gm_ref, wup_ref, wdn_ref = next(it), next(it), next(it)
    if with_final:
        gf_ref = next(it)
    y_ref = next(it)
    if with_pool:
        pool_ref, ext_ref = next(it), next(it)

    x = x_ref[0]
    if with_pool:
        j = pl.program_id(1)

        @pl.when(j == 0)
        def _():
            ext_ref[0:2 * SUBLANES, :] = jnp.zeros((2 * SUBLANES, D_MODEL), F32)

        u = _rms(x, gmix_ref[...])
        ext_ref[pl.ds(2 * SUBLANES, tq), :] = u
        pos = j * tq + lax.broadcasted_iota(jnp.int32, (tq, 1), 0)
        pooled = _pool_windows(lambda k, cols: ext_ref[pl.ds(2 * SUBLANES - k, tq), cols], u, pos, tq)
        x = x + _pool_mix(pooled, wpool_ref, pscale_ref[...])
        pool_ref[0] = ext_ref[pl.ds(tq + 1, POOL_BUF), :]
        ext_ref[0:2 * SUBLANES, :] = ext_ref[pl.ds(tq, 2 * SUBLANES), :]

    q = _dot(_rms(x, gx_ref[...]).astype(BF16), wq_ref[...])
    o = _attn_heads(
        q,
        lambda h: kb_ref[0, 0, :, h * XATTN_HEAD_DIM:(h + 1) * XATTN_HEAD_DIM],
        lambda h: vb_ref[0, 0, :, h * XATTN_HEAD_DIM:(h + 1) * XATTN_HEAD_DIM],
    )
    x = x + _dot(o.astype(BF16), wo_ref[...])
    x = _mlp(x, gm_ref[...], wup_ref, wdn_ref)
    if with_final:
        x = _rms(x, gf_ref[...])
    y_ref[0] = x


def _prompt_layer_call(x, layer, kb, vb, gx, wq, wo, gm, wup, wdn, pool=None, gfinal=None, tq=512):
    b, seq, _ = x.shape
    with_pool, with_final = pool is not None, gfinal is not None
    xspec = pl.BlockSpec((1, tq, D_MODEL), lambda i, j: (i, j, 0))
    vec = _const_spec((1, D_MODEL))
    kvspec = pl.BlockSpec((1, 1, N_MEM, D_MODEL), lambda i, j: (layer, i, 0, 0))
    args, specs = [x], [xspec]
    if with_pool:
        gmix, wpool, pscale = pool
        args += [gmix, wpool, pscale]
        specs += [vec, _const_spec(wpool.shape), vec]
    args += [gx, wq, kb, vb, wo, gm, wup, wdn]
    specs += [vec, _const_spec(wq.shape), kvspec, kvspec, _const_spec(wo.shape), vec,
              _const_spec(wup.shape), _const_spec(wdn.shape)]
    if with_final:
        args.append(gfinal)
        specs.append(vec)
    out_shape = [jax.ShapeDtypeStruct(x.shape, F32)]
    out_specs = [xspec]
    scratch = []
    if with_pool:
        out_shape.append(jax.ShapeDtypeStruct((b, POOL_BUF, D_MODEL), F32))
        out_specs.append(pl.BlockSpec((1, POOL_BUF, D_MODEL), lambda i, j: (i, 0, 0)))
        scratch.append(pltpu.VMEM((tq + 2 * SUBLANES, D_MODEL), F32))
    outs = pl.pallas_call(
        functools.partial(_prompt_layer_body, tq=tq, with_pool=with_pool, with_final=with_final),
        grid=(b, seq // tq),
        in_specs=specs,
        out_specs=out_specs,
        out_shape=out_shape,
        scratch_shapes=scratch,
        compiler_params=_params(2),
        name="prompt_layer_pool" if with_pool else "prompt_layer",
    )(*args)
    return outs if with_pool else outs[0]


ROW_ACS, ROW_DECAY, ROW_END = 0, 1, 2


def _dt_stage(xn_ref, wdt_ref, dtb_ref, alog_ref, dt_ref, acs_ref, rowsT_ref, mask, n_chunks, seg_last):
    z = _dot(xn_ref[...], wdt_ref[...]) + dtb_ref[...]
    dt_ref[...] = jnp.maximum(z, 0.0) + jnp.log1p(jnp.exp(-jnp.abs(z)))
    a = -jnp.exp(alog_ref[...])
    tri = mask.astype(F32)
    for c in range(n_chunks):
        rows = slice(c * Q, (c + 1) * Q)
        dt_c = dt_ref[rows, :]
        acs = jnp.dot(tri, dt_c * a, precision=lax.Precision.HIGHEST, preferred_element_type=F32)
        acs_ref[rows, :] = acs
        acsT, dtT = acs.T, dt_c.T
        rowsT_ref[c, ROW_ACS] = acsT
        rowsT_ref[c, ROW_DECAY] = acsT - jnp.log(dtT)
        rowsT_ref[c, ROW_END] = dtT * jnp.exp(seg_last(acs).T - acsT)


def _conv_rows(load_shifted, cw_ref, cb_ref, g):
    acc = cb_ref[g]
    for k in range(D_CONV):
        acc = acc + cw_ref[g, k:k + 1, :] * load_shifted(k)
    return _silu(acc)


def _expand_heads(cols):
    half = lax.broadcasted_iota(jnp.int32, (Q, LANES), 1) < SSM_HEADDIM
    return jnp.concatenate([jnp.where(half, cols[0], cols[1]), jnp.where(half, cols[2], cols[3])], axis=1)


def _ssd_group(g, xg, bg, cg, acs_c, rowsT_c, dskip_g, mask):
    heads = [HEADS_PER_GROUP * g + r for r in range(HEADS_PER_GROUP)]
    acs_cols = [jnp.broadcast_to(acs_c[:, h:h + 1], (Q, LANES)) for h in heads]
    cb = _dot_nt(cg, bg)
    xb = xg.astype(BF16)
    lane_head = _div_pow2(lax.broadcasted_iota(jnp.int32, (Q, GROUP_X), 1), SSM_HEADDIM)
    yd = None
    for r, h in enumerate(heads):
        dec_dt = jnp.exp(jnp.where(mask, acs_cols[r] - rowsT_c[ROW_DECAY, h:h + 1, :], -jnp.inf))
        yr = _dot((cb * dec_dt).astype(BF16), xb)
        yd = yr if r == 0 else jnp.where(lane_head == r, yr, yd)
    ydx = yd + dskip_g * xg
    xT = xg.T
    xddT = jnp.concatenate(
        [xT[r * SSM_HEADDIM:(r + 1) * SSM_HEADDIM, :] * rowsT_c[ROW_END, h:h + 1, :] for r, h in enumerate(heads)],
        axis=0)
    return ydx, jnp.exp(_expand_heads(acs_cols)), xddT


def _gate_norm(y, z, ng):
    yg = y * _silu(z)
    return yg * lax.rsqrt(jnp.mean(yg * yg, axis=-1, keepdims=True) + EPS) * ng


def _div_pow2(v, d):
    assert d & (d - 1) == 0
    return lax.shift_right_logical(v, d.bit_length() - 1)


def _causal_mask(block):
    li = lax.broadcasted_iota(jnp.int32, (Q, Q), 0)
    si = lax.broadcasted_iota(jnp.int32, (Q, Q), 1)
    m = li >= si
    if block is not None:
        m = jnp.logical_and(m, _div_pow2(li, block) == _div_pow2(si, block))
    return m


def _head_rows(vals):
    return jnp.concatenate([jnp.broadcast_to(v, (SSM_HEADDIM, LANES)) for v in vals], axis=0)


def _mamba_prompt_body(x_ref, gmix_ref, wg_ref, wdt_ref, cw_ref, cb_ref, dtb_ref, alog_ref, dsk_ref, ng_ref, wout_ref,
                       y_ref, ssm_ref, conv_ref,
                       xn_ref, dt_ref, acs_ref, rowsT_ref, zxg_ref, xpad_ref, act_ref, hist_ref, h_ref, yg_ref, *, tq):
    j = pl.program_id(1)
    n_chunks = tq // Q

    @pl.when(j == 0)
    def _():
        hist_ref[...] = jnp.zeros_like(hist_ref)
        h_ref[...] = jnp.zeros_like(h_ref)

    mask = _causal_mask(None)
    xn_ref[...] = _rms(x_ref[0], gmix_ref[...]).astype(BF16)
    _dt_stage(xn_ref, wdt_ref, dtb_ref, alog_ref, dt_ref, acs_ref, rowsT_ref, mask, n_chunks,
              lambda v: jnp.broadcast_to(v[Q - 1:Q, :], v.shape))

    for g in range(SSM_GROUPS):
        zxg, xpad, act = zxg_ref.at[g % 2], xpad_ref.at[g % 2], act_ref.at[g % 2]
        zxg[...] = _dot(xn_ref[...], wg_ref[g])
        xpad[0:SUBLANES, :] = hist_ref[g]
        xpad[pl.ds(SUBLANES, tq), :] = zxg[:, GROUP_X:GROUP_IN]

        for c in range(n_chunks):
            act[c * Q:(c + 1) * Q, :] = _conv_rows(
                lambda k: xpad[pl.ds(c * Q + SUBLANES - (D_CONV - 1) + k, Q), :], cw_ref, cb_ref, g)
        hist_ref[g] = xpad[pl.ds(tq, SUBLANES), :]

        for c in range(n_chunks):
            rows = slice(c * Q, (c + 1) * Q)
            xg = act[rows, 0:GROUP_X]
            bg = act[rows, GROUP_X:GROUP_X + D_STATE].astype(BF16)
            cg = act[rows, GROUP_X + D_STATE:GROUP_CONV].astype(BF16)
            rowsT_c = rowsT_ref.at[c]
            ydx, eacs, xddT = _ssd_group(g, xg, bg, cg, acs_ref[rows, :], rowsT_c,
                                         dsk_ref[:, g * GROUP_X:(g + 1) * GROUP_X], mask)
            hg = h_ref[g]
            y = ydx + eacs * _dot_nt(cg, hg.astype(BF16))
            yg_ref[rows, g * GROUP_X:(g + 1) * GROUP_X] = _gate_norm(
                y, zxg[rows, 0:GROUP_X], ng_ref[:, g * GROUP_X:(g + 1) * GROUP_X]).astype(BF16)
            scale = _head_rows([jnp.exp(rowsT_c[ROW_ACS, HEADS_PER_GROUP * g + r:HEADS_PER_GROUP * g + r + 1, Q - 1:Q])
                                for r in range(HEADS_PER_GROUP)])
            h_ref[g] = scale * hg + _dot(xddT.astype(BF16), bg)

        hist = hist_ref[g]
        conv_ref[0, :, g * GROUP_X:(g + 1) * GROUP_X] = hist[SUBLANES - (D_CONV - 1):, 0:GROUP_X]
        conv_ref[0, :, D_INNER + g * D_STATE:D_INNER + (g + 1) * D_STATE] = (
            hist[SUBLANES - (D_CONV - 1):, GROUP_X:GROUP_X + D_STATE])
        conv_ref[0, :, D_INNER + (SSM_GROUPS + g) * D_STATE:D_INNER + (SSM_GROUPS + g + 1) * D_STATE] = (
            hist[SUBLANES - (D_CONV - 1):, GROUP_X + D_STATE:GROUP_CONV])

    y_ref[0] = x_ref[0] + _dot(yg_ref[...], wout_ref[...])

    @pl.when(j == pl.num_programs(1) - 1)
    def _():
        ssm_ref[0] = h_ref[...]


def _mamba_weight_specs(w):
    return [_const_spec(a.shape) for a in w]


def _mamba_prompt_call(x, gmix, w, tq=512):
    b, seq, _ = x.shape
    xspec = pl.BlockSpec((1, tq, D_MODEL), lambda i, j: (i, j, 0))
    n_chunks = tq // Q
    return pl.pallas_call(
        functools.partial(_mamba_prompt_body, tq=tq),
        grid=(b, seq // tq),
        in_specs=[xspec, _const_spec((1, D_MODEL))] + _mamba_weight_specs(w),
        out_specs=[xspec,
                   pl.BlockSpec((1, SSM_GROUPS, GROUP_X, D_STATE), lambda i, j: (i, 0, 0, 0)),
                   pl.BlockSpec((1, D_CONV - 1, CONV_DIM), lambda i, j: (i, 0, 0))],
        out_shape=[jax.ShapeDtypeStruct(x.shape, F32),
                   jax.ShapeDtypeStruct((b, SSM_GROUPS, GROUP_X, D_STATE), F32),
                   jax.ShapeDtypeStruct((b, D_CONV - 1, CONV_DIM), F32)],
        scratch_shapes=[
            pltpu.VMEM((tq, D_MODEL), BF16),
            pltpu.VMEM((tq, LANES), F32),
            pltpu.VMEM((tq, LANES), F32),
            pltpu.VMEM((n_chunks, 3, LANES, Q), F32),
            pltpu.VMEM((2, tq, GROUP_IN), F32),
            pltpu.VMEM((2, tq + SUBLANES, GROUP_CONV), F32),
            pltpu.VMEM((2, tq, GROUP_CONV), F32),
            pltpu.VMEM((SSM_GROUPS, SUBLANES, GROUP_CONV), F32),
            pltpu.VMEM((SSM_GROUPS, GROUP_X, D_STATE), F32),
            pltpu.VMEM((tq, D_INNER), BF16),
        ],
        compiler_params=_params(2),
        name="mamba_prompt",
    )(x, gmix, *w)


def _mamba_sample_body(x_ref, gmix_ref, wg_ref, wdt_ref, cw_ref, cb_ref, dtb_ref, alog_ref, dsk_ref, ng_ref, wout_ref,
                       cst_ref, sst_ref, gx_ref, wq_ref,
                       y_ref, q_ref, ssm_ref, conv_ref,
                       xn_ref, dt_ref, acs_ref, rowsT_ref, zxg_ref, xpad_ref,
                       ydx_ref, eacs_ref, xddT_ref, bc_ref, z_ref, scale_ref, yg_ref, *, seq, sub_b):
    s = pl.program_id(1)
    nb = Q // seq
    mask = _causal_mask(seq)

    @pl.when(s == 0)
    def _():
        def seg_last(v):
            v3 = v.reshape(nb, seq, v.shape[-1])
            return jnp.broadcast_to(v3[:, seq - 1:seq, :], v3.shape).reshape(v.shape)

        xn_ref[...] = _rms(x_ref[...], gmix_ref[...]).astype(BF16)
        _dt_stage(xn_ref, wdt_ref, dtb_ref, alog_ref, dt_ref, acs_ref, rowsT_ref, mask, 1, seg_last)
        rowsT_c = rowsT_ref.at[0]
        acsT = rowsT_c[ROW_ACS]
        for bi in range(nb):
            scale_ref[bi] = jnp.broadcast_to(jnp.exp(acsT[:, (bi + 1) * seq - 1:(bi + 1) * seq]), (LANES, LANES))

        for g in range(SSM_GROUPS):
            zxg_ref[...] = _dot(xn_ref[...], wg_ref[g])
            col_slices = (
                (slice(g * GROUP_X, (g + 1) * GROUP_X), slice(0, GROUP_X)),
                (slice(D_INNER + g * D_STATE, D_INNER + (g + 1) * D_STATE), slice(GROUP_X, GROUP_X + D_STATE)),
                (slice(D_INNER + (SSM_GROUPS + g) * D_STATE, D_INNER + (SSM_GROUPS + g + 1) * D_STATE),
                 slice(GROUP_X + D_STATE, GROUP_CONV)),
            )
            for src, dst in col_slices:
                xpad_ref[:, SUBLANES - (D_CONV - 1):SUBLANES, dst] = cst_ref[:, :, src]
            xpad_ref[:, SUBLANES:SUBLANES + seq, :] = zxg_ref[:, GROUP_X:GROUP_IN].reshape(nb, seq, GROUP_CONV)
            act = _conv_rows(
                lambda k: xpad_ref[:, pl.ds(SUBLANES - (D_CONV - 1) + k, seq), :].reshape(Q, GROUP_CONV),
                cw_ref, cb_ref, g)
            for src, dst in col_slices:
                conv_ref[:, :, src] = xpad_ref[:, pl.ds(SUBLANES + seq - (D_CONV - 1), D_CONV - 1), dst]
            xg = act[:, 0:GROUP_X]
            bg = act[:, GROUP_X:GROUP_X + D_STATE].astype(BF16)
            cg = act[:, GROUP_X + D_STATE:GROUP_CONV].astype(BF16)
            ydx, eacs, xddT = _ssd_group(g, xg, bg, cg, acs_ref[...], rowsT_c,
                                         dsk_ref[:, g * GROUP_X:(g + 1) * GROUP_X], mask)
            ydx_ref[g] = ydx
            eacs_ref[g] = eacs
            xddT_ref[g] = xddT
            bc_ref[g, 0] = bg
            bc_ref[g, 1] = cg
            z_ref[g] = zxg_ref[:, 0:GROUP_X]

    rows_per = sub_b * seq
    r0 = pl.multiple_of(s * rows_per, rows_per)
    rows = pl.ds(r0, rows_per)
    row_b = _div_pow2(lax.broadcasted_iota(jnp.int32, (rows_per, GROUP_X), 0), seq)
    lane_b = _div_pow2(lax.broadcasted_iota(jnp.int32, (GROUP_X, Q), 1), seq)
    for g in range(SSM_GROUPS):
        cg = bc_ref[g, 1, rows, :]
        bg = bc_ref[g, 0]
        xddT = xddT_ref[g]
        yoff = jnp.zeros((rows_per, GROUP_X), F32)
        for bi in range(sub_b):
            hg = sst_ref[bi, g]
            yoff = jnp.where(row_b == bi, _dot_nt(cg, hg.astype(BF16)), yoff)
            sc = scale_ref[s * sub_b + bi]
            scale = _head_rows([sc[HEADS_PER_GROUP * g + r:HEADS_PER_GROUP * g + r + 1, :]
                                for r in range(HEADS_PER_GROUP)])
            xm = jnp.where(lane_b == s * sub_b + bi, xddT, 0.0).astype(BF16)
            ssm_ref[bi, g] = scale * hg + _dot(xm, bg)
        y = ydx_ref[g, rows, :] + eacs_ref[g, rows, :] * yoff
        yg_ref[rows, g * GROUP_X:(g + 1) * GROUP_X] = _gate_norm(
            y, z_ref[g, rows, :], ng_ref[:, g * GROUP_X:(g + 1) * GROUP_X]).astype(BF16)

    @pl.when(s == pl.num_programs(1) - 1)
    def _():
        x1 = x_ref[...] + _dot(yg_ref[...], wout_ref[...])
        y_ref[...] = x1
        q_ref[...] = _dot(_rms(x1, gx_ref[...]).astype(BF16), wq_ref[...])


def _mamba_sample_call(x2d, gmix, w, conv_state, ssm_state, gx, wq, seq, sub_b=4):
    n = x2d.shape[0]
    batch = n // seq
    nb = Q // seq
    n_sub = nb // sub_b
    xspec = pl.BlockSpec((Q, D_MODEL), lambda i, s: (i, 0))
    stspec = pl.BlockSpec((sub_b, SSM_GROUPS, GROUP_X, D_STATE), lambda i, s: (i * n_sub + s, 0, 0, 0))
    cspec = pl.BlockSpec((nb, D_CONV - 1, CONV_DIM), lambda i, s: (i, 0, 0))
    return pl.pallas_call(
        functools.partial(_mamba_sample_body, seq=seq, sub_b=sub_b),
        grid=(n // Q, n_sub),
        in_specs=[xspec, _const_spec((1, D_MODEL))] + _mamba_weight_specs(w)
        + [cspec, stspec, _const_spec((1, D_MODEL)), _const_spec(wq.shape)],
        out_specs=[xspec, xspec, stspec, cspec],
        out_shape=[jax.ShapeDtypeStruct(x2d.shape, F32), jax.ShapeDtypeStruct(x2d.shape, F32),
                   jax.ShapeDtypeStruct((batch, SSM_GROUPS, GROUP_X, D_STATE), F32),
                   jax.ShapeDtypeStruct((batch, D_CONV - 1, CONV_DIM), F32)],
        scratch_shapes=[
            pltpu.VMEM((Q, D_MODEL), BF16),
            pltpu.VMEM((Q, LANES), F32),
            pltpu.VMEM((Q, LANES), F32),
            pltpu.VMEM((1, 3, LANES, Q), F32),
            pltpu.VMEM((Q, GROUP_IN), F32),
            pltpu.VMEM((nb, SUBLANES + seq, GROUP_CONV), F32),
            pltpu.VMEM((SSM_GROUPS, Q, GROUP_X), F32),
            pltpu.VMEM((SSM_GROUPS, Q, GROUP_X), F32),
            pltpu.VMEM((SSM_GROUPS, GROUP_X, Q), F32),
            pltpu.VMEM((SSM_GROUPS, 2, Q, D_STATE), BF16),
            pltpu.VMEM((SSM_GROUPS, Q, GROUP_X), F32),
            pltpu.VMEM((nb, LANES, LANES), F32),
            pltpu.VMEM((Q, D_INNER), BF16),
        ],
        compiler_params=_params(2),
        name="mamba_sample",
    )(x2d, gmix, *w, conv_state, ssm_state, gx, wq)


def _pool_sample_body(x_ref, st_ref, gmix_ref, wpool_ref, pscale_ref, gx_ref, wq_ref,
                      y_ref, q_ref, pool_ref, ext_ref, *, seq, pos0):
    nb = x_ref.shape[0]
    n = nb * seq
    x = x_ref[...].reshape(n, D_MODEL)
    u = _rms(x, gmix_ref[...])
    hist0 = 2 * SUBLANES - POOL_BUF
    ext_ref[:, pl.ds(hist0, POOL_BUF), :] = st_ref[...]
    ext_ref[:, 2 * SUBLANES:2 * SUBLANES + seq, :] = u.reshape(nb, seq, D_MODEL)
    pos = pos0 + lax.broadcasted_iota(jnp.int32, (nb, seq, 1), 1).reshape(n, 1)
    pooled = _pool_windows(
        lambda k, cols: ext_ref[:, pl.ds(2 * SUBLANES - k, seq), cols].reshape(n, POOL_GROUP_DIM), u, pos, n)
    x1 = x + _pool_mix(pooled, wpool_ref, pscale_ref[...])
    pool_ref[...] = ext_ref[:, pl.ds(hist0 + seq, POOL_BUF), :]
    y_ref[...] = x1.reshape(nb, seq, D_MODEL)
    q_ref[...] = _dot(_rms(x1, gx_ref[...]).astype(BF16), wq_ref[...]).reshape(nb, seq, D_MODEL)


def _pool_sample_call(x, pool_state, gmix, wpool, pscale, gx, wq, pos0, nb=16):
    batch, seq, _ = x.shape
    xspec = pl.BlockSpec((nb, seq, D_MODEL), lambda i: (i, 0, 0))
    pspec = pl.BlockSpec((nb, POOL_BUF, D_MODEL), lambda i: (i, 0, 0))
    vec = _const_spec((1, D_MODEL))
    return pl.pallas_call(
        functools.partial(_pool_sample_body, seq=seq, pos0=pos0),
        grid=(batch // nb,),
        in_specs=[xspec, pspec, vec, _const_spec(wpool.shape), vec, vec, _const_spec(wq.shape)],
        out_specs=[xspec, xspec, pspec],
        out_shape=[jax.ShapeDtypeStruct(x.shape, F32), jax.ShapeDtypeStruct(x.shape, F32),
                   jax.ShapeDtypeStruct(pool_state.shape, F32)],
        scratch_shapes=[pltpu.VMEM((nb, 2 * SUBLANES + seq, D_MODEL), F32)],
        compiler_params=_params(1),
        name="pool_sample",
    )(x, pool_state, gmix, wpool, pscale, gx, wq)


HEAD_HALVES = XATTN_HEAD_DIM // LANES
KV_ROWS = N_MEM * XATTN_HEADS * HEAD_HALVES


def _attn_sample_body(q_ref, k_ref, v_ref, o_ref):
    nb, n_rows, _ = q_ref.shape
    half = n_rows // HEAD_HALVES
    grp = XATTN_HEADS * HEAD_HALVES
    lane = lax.broadcasted_iota(jnp.int32, (half, KV_ROWS), 1)
    rowi = lax.broadcasted_iota(jnp.int32, (half, KV_ROWS), 0)
    valid = (lane & (grp - 1)) == (rowi & (XATTN_HEADS - 1))
    for bi in range(nb):
        q2 = (q_ref[bi] * (XATTN_HEAD_DIM ** -0.5)).astype(BF16)
        prod = _dot_nt(q2, k_ref[0, bi].astype(BF16))
        s = prod[0:half] + pltpu.roll(prod[half:], KV_ROWS - XATTN_HEADS, axis=1)
        s = jnp.where(valid, s, -jnp.inf)
        p = jnp.exp(s - jnp.max(s, axis=-1, keepdims=True))
        p = p * (1.0 / jnp.sum(p, axis=-1, keepdims=True))
        pe = jnp.concatenate([p, pltpu.roll(p, XATTN_HEADS, axis=1)], axis=0).astype(BF16)
        o_ref[bi] = _dot(pe, v_ref[0, bi].astype(BF16))


def _cache_rows(cache):
    d, b = cache.shape[:2]
    c = cache.reshape(d, b, N_MEM, XATTN_HEADS, HEAD_HALVES, LANES)
    return c.transpose(0, 1, 2, 4, 3, 5).reshape(d, b, KV_ROWS, LANES)


def _attn_sample_call(q, k_rows, v_rows, layer, nb=4):
    assert HEAD_HALVES == 2
    batch, seq, _ = q.shape
    n_rows = seq * XATTN_HEADS * HEAD_HALVES
    q2 = q.reshape(batch, seq, XATTN_HEADS, HEAD_HALVES, LANES).transpose(0, 3, 1, 2, 4).reshape(batch, n_rows, LANES)
    qspec = pl.BlockSpec((nb, n_rows, LANES), lambda i: (i, 0, 0))
    kvspec = pl.BlockSpec((1, nb, KV_ROWS, LANES), lambda i: (layer, i, 0, 0))
    o2 = pl.pallas_call(
        _attn_sample_body,
        grid=(batch // nb,),
        in_specs=[qspec, kvspec, kvspec],
        out_specs=qspec,
        out_shape=jax.ShapeDtypeStruct(q2.shape, F32),
        compiler_params=_params(1),
        name="attn_sample",
    )(q2, k_rows, v_rows)
    o = o2.reshape(batch, HEAD_HALVES, seq, XATTN_HEADS, LANES).transpose(0, 2, 3, 1, 4)
    return o.reshape(batch, seq, D_MODEL)


def _out_mlp_body(*refs, with_final):
    it = iter(refs)
    x_ref, o_ref, wo_ref, gm_ref, wup_ref, wdn_ref = (next(it) for _ in range(6))
    if with_final:
        gf_ref = next(it)
    y_ref = next(it)
    x = x_ref[...] + _dot(o_ref[...].astype(BF16), wo_ref[...])
    x = _mlp(x, gm_ref[...], wup_ref, wdn_ref)
    if with_final:
        x = _rms(x, gf_ref[...])
    y_ref[...] = x


def _out_mlp_call(x2d, o2d, wo, gm, wup, wdn, gfinal=None, tm=512):
    n = x2d.shape[0]
    tm = min(tm, n)
    row = pl.BlockSpec((tm, D_MODEL), lambda i: (i, 0))
    vec = _const_spec((1, D_MODEL))
    args = [x2d, o2d, wo, gm, wup, wdn]
    specs = [row, row, _const_spec(wo.shape), vec, _const_spec(wup.shape), _const_spec(wdn.shape)]
    if gfinal is not None:
        args.append(gfinal)
        specs.append(vec)
    return pl.pallas_call(
        functools.partial(_out_mlp_body, with_final=gfinal is not None),
        grid=(n // tm,),
        in_specs=specs,
        out_specs=row,
        out_shape=jax.ShapeDtypeStruct(x2d.shape, F32),
        compiler_params=_params(1),
        name="out_mlp",
    )(*args)


def _mamba_weights(w_in, conv_w, conv_b, dt_bias, a_log, d_skip, norm_gated, w_out):
    z, x = w_in[:, :D_INNER], w_in[:, D_INNER:2 * D_INNER]
    bm = w_in[:, 2 * D_INNER:2 * D_INNER + SSM_GROUPS * D_STATE]
    cm = w_in[:, 2 * D_INNER + SSM_GROUPS * D_STATE:D_INNER + CONV_DIM]
    wdt = w_in[:, D_INNER + CONV_DIM:]

    def grp(a, width):
        return a.reshape(a.shape[0], SSM_GROUPS, width)

    wg = jnp.concatenate([grp(z, GROUP_X), grp(x, GROUP_X), grp(bm, D_STATE), grp(cm, D_STATE)], axis=-1)
    wg = wg.transpose(1, 0, 2).astype(BF16)
    pad = LANES - SSM_HEADS
    wdt = jnp.pad(wdt, ((0, 0), (0, pad))).astype(BF16)

    def conv_grp(a):
        return jnp.concatenate([grp(a[:, :D_INNER], GROUP_X), grp(a[:, D_INNER:D_INNER + SSM_GROUPS * D_STATE], D_STATE),
                                grp(a[:, D_INNER + SSM_GROUPS * D_STATE:], D_STATE)], axis=-1).transpose(1, 0, 2)

    cw = conv_grp(conv_w)
    cb = conv_grp(conv_b[None, :])
    dtb = jnp.pad(dt_bias, (0, pad))[None, :]
    alog = jnp.pad(a_log, (0, pad))[None, :]
    dsk = jnp.repeat(d_skip, SSM_HEADDIM)[None, :]
    return (wg, wdt, cw, cb, dtb, alog, dsk, norm_gated[None, :], w_out.astype(BF16))


def kernel(x_prompt, x_sample, cache_mem_k, cache_mem_v, state_ssm, state_conv, state_pool, mem_prompt, norm_mix, norm_xattn, norm_mem, norm_mlp, norm_final, w_in, conv_w, conv_b, dt_bias, a_log, d_skip, norm_gated, w_out, w_pool, pool_scale, w_xq, w_xk, w_xv, w_xo, w_up, w_down):
    bp, seq_p, _ = x_prompt.shape
    bs, seq_s, _ = x_sample.shape
    depth = w_xq.shape[0]
    row = lambda a: a[None, :]

    wq, wk, wv, wo = (a.astype(BF16) for a in (w_xq, w_xk, w_xv, w_xo))
    wup, wdn = w_up.astype(BF16), w_down.astype(BF16)
    wpool = w_pool.astype(BF16)
    mw = _mamba_weights(w_in[0], conv_w[0], conv_b[0], dt_bias[0], a_log[0], d_skip[0], norm_gated[0], w_out[0])

    k_p, v_p, kb, vb = _kv_call(mem_prompt.reshape(bp * N_MEM, D_MODEL), norm_mem[:, None, :], wk, wv)
    kv_shape = (depth, bp, N_MEM, XATTN_HEADS, XATTN_HEAD_DIM)
    kb = kb.reshape(depth, bp, N_MEM, D_MODEL)
    vb = vb.reshape(depth, bp, N_MEM, D_MODEL)

    xp, ssm_p, conv_p = _mamba_prompt_call(x_prompt, row(norm_mix[0]), mw)
    xp = _prompt_layer_call(xp, 0, kb, vb, row(norm_xattn[0]), wq[0], wo[0], row(norm_mlp[0]), wup[0], wdn[0])
    y_prompt, pool_p = _prompt_layer_call(
        xp, 1, kb, vb, row(norm_xattn[1]), wq[1], wo[1], row(norm_mlp[1]), wup[1], wdn[1],
        pool=(row(norm_mix[1]), wpool[0], row(pool_scale[0])), gfinal=row(norm_final))

    kc, vc = _cache_rows(cache_mem_k), _cache_rows(cache_mem_v)
    xs = x_sample.reshape(bs * seq_s, D_MODEL)
    ssm_in = state_ssm[0].reshape(bs, SSM_GROUPS, GROUP_X, D_STATE)
    xs, q, ssm_s, conv_s = _mamba_sample_call(xs, row(norm_mix[0]), mw, state_conv[0], ssm_in,
                                              row(norm_xattn[0]), wq[0], seq_s)
    o = _attn_sample_call(q.reshape(bs, seq_s, D_MODEL), kc, vc, 0)
    xs = _out_mlp_call(xs, o.reshape(bs * seq_s, D_MODEL), wo[0], row(norm_mlp[0]), wup[0], wdn[0])
    xs, q, pool_s = _pool_sample_call(xs.reshape(bs, seq_s, D_MODEL), state_pool[0], row(norm_mix[1]), wpool[0],
                                      row(pool_scale[0]), row(norm_xattn[1]), wq[1], PAST_LEN)
    o = _attn_sample_call(q, kc, vc, 1)
    y_sample = _out_mlp_call(xs.reshape(bs * seq_s, D_MODEL), o.reshape(bs * seq_s, D_MODEL), wo[1],
                             row(norm_mlp[1]), wup[1], wdn[1], gfinal=row(norm_final))

    ssm_shape = (1, -1, SSM_HEADS, SSM_HEADDIM, D_STATE)
    return (y_prompt, y_sample.reshape(bs, seq_s, D_MODEL),
            k_p.reshape(kv_shape), v_p.reshape(kv_shape),
            ssm_p.reshape(ssm_shape), conv_p[None], pool_p[None],
            ssm_s.reshape(ssm_shape), conv_s[None], pool_s[None])
```

```python
import functools

import jax
import jax.numpy as jnp
from jax import lax
from jax.experimental import pallas as pl
from jax.experimental.pallas import tpu as pltpu

F32 = jnp.float32
BF16 = jnp.bfloat16

D_MODEL = 1024
D_INNER = 2048
SSM_HEADS = 32
SSM_HEADDIM = 64
SSM_GROUPS = 8
HEADS_PER_GROUP = 4
GROUP_X = HEADS_PER_GROUP * SSM_HEADDIM
D_STATE = 128
D_CONV = 4
CONV_DIM = D_INNER + 2 * SSM_GROUPS * D_STATE
GROUP_CONV = GROUP_X + 2 * D_STATE
GROUP_IN = GROUP_X + GROUP_CONV
POOL_WINDOWS = (2, 4, 8, 16)
POOL_GROUP_DIM = 256
POOL_BUF = 15
N_MEM = 256
XATTN_HEADS = 4
XATTN_HEAD_DIM = 256
D_FF = 4096
EPS = 1e-5
PAST_LEN = 16384
LOG2_E = 1.4426950408889634

LANES = 128
SUBLANES = 8
Q = 128
FF_CHUNK = 1024
VMEM_LIMIT = 56 * 1024 * 1024

_NT = (((1,), (1,)), ((), ()))


def _params(n_grid):
    return pltpu.CompilerParams(dimension_semantics=("arbitrary",) * n_grid, vmem_limit_bytes=VMEM_LIMIT)


def _const_spec(shape):
    nd = len(shape)
    return pl.BlockSpec(shape, lambda *_: (0,) * nd, pipeline_mode=pl.Buffered(1))


def _dot(a, b):
    return jnp.dot(a, b, preferred_element_type=F32)


def _dot_nt(a, b):
    return lax.dot_general(a, b, _NT, preferred_element_type=F32)


def _rms(x, g):
    return x * lax.rsqrt(jnp.mean(x * x, axis=-1, keepdims=True) + EPS) * g


def _silu(x):
    return x * (1.0 / (1.0 + jnp.exp(-x)))


def _kv_body(mem_ref, g_ref, wk_ref, wv_ref, k_ref, v_ref, kb_ref, vb_ref):
    mn = _rms(mem_ref[...], g_ref[0]).astype(BF16)
    k = _dot(mn, wk_ref[0])
    v = _dot(mn, wv_ref[0])
    for h in range(XATTN_HEADS):
        sl = slice(h * XATTN_HEAD_DIM, (h + 1) * XATTN_HEAD_DIM)
        k_ref[0, :, h, :] = k[:, sl]
        v_ref[0, :, h, :] = v[:, sl]
    kb_ref[0] = k.astype(BF16)
    vb_ref[0] = v.astype(BF16)


def _kv_call(mem, g_mem, wk, wv, tm=512):
    n = mem.shape[0]
    depth = wk.shape[0]
    row = pl.BlockSpec((tm, D_MODEL), lambda l, t: (t, 0))
    wspec = pl.BlockSpec((1, D_MODEL, D_MODEL), lambda l, t: (l, 0, 0))
    ospec = pl.BlockSpec((1, tm, D_MODEL), lambda l, t: (l, t, 0))
    hspec = pl.BlockSpec((1, tm, XATTN_HEADS, XATTN_HEAD_DIM), lambda l, t: (l, t, 0, 0))
    return pl.pallas_call(
        _kv_body,
        grid=(depth, n // tm),
        in_specs=[row, pl.BlockSpec((1, 1, D_MODEL), lambda l, t: (l, 0, 0)), wspec, wspec],
        out_specs=[hspec, hspec, ospec, ospec],
        out_shape=[jax.ShapeDtypeStruct((depth, n, XATTN_HEADS, XATTN_HEAD_DIM), F32)] * 2
        + [jax.ShapeDtypeStruct((depth, n, D_MODEL), BF16)] * 2,
        compiler_params=_params(2),
        name="mem_kv",
    )(mem, g_mem, wk, wv)


def _attn_heads(q, k_of, v_of):
    outs = []
    for h in range(XATTN_HEADS):
        sl = slice(h * XATTN_HEAD_DIM, (h + 1) * XATTN_HEAD_DIM)
        s = _dot_nt(q[:, sl].astype(BF16), k_of(h)) * (XATTN_HEAD_DIM ** -0.5)
        p = jnp.exp(s - jnp.max(s, axis=-1, keepdims=True))
        p = p * (1.0 / jnp.sum(p, axis=-1, keepdims=True))
        outs.append(_dot(p.astype(BF16), v_of(h)))
    return jnp.concatenate(outs, axis=-1)


def _mlp(x, g, wup_ref, wdn_ref):
    hn = _rms(x, g).astype(BF16)
    acc = x
    for c in range(D_FF // FF_CHUNK):
        a = jnp.maximum(_dot(hn, wup_ref[:, c * FF_CHUNK:(c + 1) * FF_CHUNK]), 0.0)
        acc = acc + _dot((a * a).astype(BF16), wdn_ref[c * FF_CHUNK:(c + 1) * FF_CHUNK, :])
    return acc


def _pool_windows(load, u, pos, n):
    outs = []
    for gi, w in enumerate(POOL_WINDOWS):
        cols = slice(gi * POOL_GROUP_DIM, (gi + 1) * POOL_GROUP_DIM)
        acc = load(0, cols)
        for k in range(1, w):
            acc = acc + load(k, cols)
        cnt = jnp.minimum(pos + 1, w).astype(F32)
        outs.append(acc / cnt - u[:, cols])
    return outs


def _pool_windows_rolled(ext_ref, u, pos, n):
    outs = []
    for gi, w in enumerate(POOL_WINDOWS):
        assert w & (w - 1) == 0 and w <= 2 * SUBLANES
        cols = slice(gi * POOL_GROUP_DIM, (gi + 1) * POOL_GROUP_DIM)
        acc = ext_ref[0:n + 2 * SUBLANES, cols]
        span = 1
        while span < w:
            acc = acc + pltpu.roll(acc, span, axis=0)
            span *= 2
        cnt = jnp.minimum(pos + 1, w).astype(F32)
        outs.append(acc[2 * SUBLANES:, :] / cnt - u[:, cols])
    return outs


def _pool_mix(pooled, wpool_ref, scale):
    outs = [_dot(p.astype(BF16), wpool_ref[gi]) for gi, p in enumerate(pooled)]
    return jnp.concatenate(outs, axis=-1) * scale


def _prompt_layer_body(*refs, tq, with_pool, with_final):
    it = iter(refs)
    x_ref = next(it)
    if with_pool:
        gmix_ref, wpool_ref, pscale_ref = next(it), next(it), next(it)
    gx_ref, wq_ref, kb_ref, vb_ref, wo_ref = next(it), next(it), next(it), next(it), next(it)
    gm_ref, wup_ref, wdn_ref = next(it), next(it), next(it)
    if with_final:
        gf_ref = next(it)
    y_ref = next(it)
    if with_pool:
        pool_ref, ext_ref = next(it), next(it)

    x = x_ref[0]
    if with_pool:
        j = pl.program_id(1)

        @pl.when(j == 0)
        def _():
            ext_ref[0:2 * SUBLANES, :] = jnp.zeros((2 * SUBLANES, D_MODEL), F32)

        u = _rms(x, gmix_ref[...])
        ext_ref[pl.ds(2 * SUBLANES, tq), :] = u
        pos = j * tq + lax.broadcasted_iota(jnp.int32, (tq, 1), 0)
        pooled = _pool_windows_rolled(ext_ref, u, pos, tq)
        x = x + _pool_mix(pooled, wpool_ref, pscale_ref[...])
        pool_ref[0] = ext_ref[pl.ds(tq + 1, POOL_BUF), :]
        ext_ref[0:2 * SUBLANES, :] = ext_ref[pl.ds(tq, 2 * SUBLANES), :]

    q = _dot(_rms(x, gx_ref[...]).astype(BF16), wq_ref[...])
    o = _attn_heads(
        q,
        lambda h: kb_ref[0, 0, :, h * XATTN_HEAD_DIM:(h + 1) * XATTN_HEAD_DIM],
        lambda h: vb_ref[0, 0, :, h * XATTN_HEAD_DIM:(h + 1) * XATTN_HEAD_DIM],
    )
    x = x + _dot(o.astype(BF16), wo_ref[...])
    x = _mlp(x, gm_ref[...], wup_ref, wdn_ref)
    if with_final:
        x = _rms(x, gf_ref[...])
    y_ref[0] = x


def _prompt_layer_call(x, layer, kb, vb, gx, wq, wo, gm, wup, wdn, pool=None, gfinal=None, tq=512):
    b, seq, _ = x.shape
    with_pool, with_final = pool is not None, gfinal is not None
    xspec = pl.BlockSpec((1, tq, D_MODEL), lambda i, j: (i, j, 0))
    vec = _const_spec((1, D_MODEL))
    kvspec = pl.BlockSpec((1, 1, N_MEM, D_MODEL), lambda i, j: (layer, i, 0, 0))
    args, specs = [x], [xspec]
    if with_pool:
        gmix, wpool, pscale = pool
        args += [gmix, wpool, pscale]
        specs += [vec, _const_spec(wpool.shape), vec]
    args += [gx, wq, kb, vb, wo, gm, wup, wdn]
    specs += [vec, _const_spec(wq.shape), kvspec, kvspec, _const_spec(wo.shape), vec,
              _const_spec(wup.shape), _const_spec(wdn.shape)]
    if with_final:
        args.append(gfinal)
        specs.append(vec)
    out_shape = [jax.ShapeDtypeStruct(x.shape, F32)]
    out_specs = [xspec]
    scratch = []
    if with_pool:
        out_shape.append(jax.ShapeDtypeStruct((b, POOL_BUF, D_MODEL), F32))
        out_specs.append(pl.BlockSpec((1, POOL_BUF, D_MODEL), lambda i, j: (i, 0, 0)))
        scratch.append(pltpu.VMEM((tq + 2 * SUBLANES, D_MODEL), F32))
    outs = pl.pallas_call(
        functools.partial(_prompt_layer_body, tq=tq, with_pool=with_pool, with_final=with_final),
        grid=(b, seq // tq),
        in_specs=specs,
        out_specs=out_specs,
        out_shape=out_shape,
        scratch_shapes=scratch,
        compiler_params=_params(2),
        name="prompt_layer_pool" if with_pool else "prompt_layer",
    )(*args)
    return outs if with_pool else outs[0]


ROW_ACS, ROW_DECAY, ROW_END = 0, 1, 2


def _dt_stage(xn_ref, wdt_ref, dtb_ref, alog_ref, dt_ref, acs_ref, rowsT_ref, mask, n_chunks, seg_last):
    z = _dot(xn_ref[...], wdt_ref[...]) + dtb_ref[...]
    dt_ref[...] = jnp.maximum(z, 0.0) + jnp.log1p(jnp.exp(-jnp.abs(z)))
    a = -jnp.exp(alog_ref[...]) * LOG2_E
    tri = mask.astype(F32)
    for c in range(n_chunks):
        rows = slice(c * Q, (c + 1) * Q)
        dt_c = dt_ref[rows, :]
        acs = jnp.dot(tri, dt_c * a, precision=lax.Precision.HIGHEST, preferred_element_type=F32)
        acs_ref[rows, :] = acs
        acsT, dtT = acs.T, dt_c.T
        rowsT_ref[c, ROW_ACS] = acsT
        rowsT_ref[c, ROW_DECAY] = acsT - jnp.log2(dtT)
        rowsT_ref[c, ROW_END] = dtT * jnp.exp2(seg_last(acs).T - acsT)


def _conv_rows(load_shifted, cw_ref, cb_ref, g):
    acc = cb_ref[g]
    for k in range(D_CONV):
        acc = acc + cw_ref[g, k:k + 1, :] * load_shifted(k)
    return _silu(acc)


def _conv_rolled(ext, cw_ref, cb_ref, g):
    acc = cb_ref[g] + cw_ref[g, D_CONV - 1:D_CONV, :] * ext
    for s in range(1, D_CONV):
        acc = acc + cw_ref[g, D_CONV - 1 - s:D_CONV - s, :] * pltpu.roll(ext, s, axis=0)
    return _silu(acc[SUBLANES:, :])


def _expand_heads(cols):
    half = lax.broadcasted_iota(jnp.int32, (Q, LANES), 1) < SSM_HEADDIM
    return jnp.concatenate([jnp.where(half, cols[0], cols[1]), jnp.where(half, cols[2], cols[3])], axis=1)


def _ssd_group(g, xg, bg, cg, acs_c, rowsT_c, dskip_g, mask):
    heads = [HEADS_PER_GROUP * g + r for r in range(HEADS_PER_GROUP)]
    acs_cols = [jnp.broadcast_to(acs_c[:, h:h + 1], (Q, LANES)) for h in heads]
    cb = _dot_nt(cg, bg)
    xb = xg.astype(BF16)
    lane_head = _div_pow2(lax.broadcasted_iota(jnp.int32, (Q, GROUP_X), 1), SSM_HEADDIM)
    yd = None
    for r, h in enumerate(heads):
        dec_dt = jnp.exp2(jnp.where(mask, acs_cols[r] - rowsT_c[ROW_DECAY, h:h + 1, :], -jnp.inf))
        yr = _dot((cb * dec_dt).astype(BF16), xb)
        yd = yr if r == 0 else jnp.where(lane_head == r, yr, yd)
    ydx = yd + dskip_g * xg
    xT = xg.T
    xddT = jnp.concatenate(
        [xT[r * SSM_HEADDIM:(r + 1) * SSM_HEADDIM, :] * rowsT_c[ROW_END, h:h + 1, :] for r, h in enumerate(heads)],
        axis=0)
    return ydx, jnp.exp2(_expand_heads(acs_cols)), xddT


def _gate_norm(y, z, ng):
    yg = y * _silu(z)
    return yg * lax.rsqrt(jnp.mean(yg * yg, axis=-1, keepdims=True) + EPS) * ng


def _div_pow2(v, d):
    assert d & (d - 1) == 0
    return lax.shift_right_logical(v, d.bit_length() - 1)


def _causal_mask(block):
    li = lax.broadcasted_iota(jnp.int32, (Q, Q), 0)
    si = lax.broadcasted_iota(jnp.int32, (Q, Q), 1)
    m = li >= si
    if block is not None:
        m = jnp.logical_and(m, _div_pow2(li, block) == _div_pow2(si, block))
    return m


def _head_rows(vals):
    return jnp.concatenate([jnp.broadcast_to(v, (SSM_HEADDIM, LANES)) for v in vals], axis=0)


def _mamba_prompt_body(x_ref, gmix_ref, wg_ref, wdt_ref, cw_ref, cb_ref, dtb_ref, alog_ref, dsk_ref, ng_ref, wout_ref,
                       y_ref, ssm_ref, conv_ref,
                       xn_ref, dt_ref, acs_ref, rowsT_ref, zxg_ref, xpad_ref, act_ref, hist_ref, h_ref, yg_ref, *, tq):
    j = pl.program_id(1)
    n_chunks = tq // Q

    @pl.when(j == 0)
    def _():
        hist_ref[...] = jnp.zeros_like(hist_ref)
        h_ref[...] = jnp.zeros_like(h_ref)

    mask = _causal_mask(None)
    xn_ref[...] = _rms(x_ref[0], gmix_ref[...]).astype(BF16)
    _dt_stage(xn_ref, wdt_ref, dtb_ref, alog_ref, dt_ref, acs_ref, rowsT_ref, mask, n_chunks,
              lambda v: jnp.broadcast_to(v[Q - 1:Q, :], v.shape))

    def in_proj(g):
        zxg_ref[g % 2] = _dot(xn_ref[...], wg_ref[g])

    in_proj(0)
    for g in range(SSM_GROUPS):
        if g + 1 < SSM_GROUPS:
            in_proj(g + 1)
        zxg, xpad, act = zxg_ref.at[g % 2], xpad_ref.at[g % 2], act_ref.at[g % 2]
        xpad[0:SUBLANES, :] = hist_ref[g]
        xpad[pl.ds(SUBLANES, tq), :] = zxg[:, GROUP_X:GROUP_IN]

        for c in range(n_chunks):
            act[c * Q:(c + 1) * Q, :] = _conv_rolled(xpad[c * Q:(c + 1) * Q + SUBLANES, :], cw_ref, cb_ref, g)
        hist_ref[g] = xpad[pl.ds(tq, SUBLANES), :]

        for c in range(n_chunks):
            rows = slice(c * Q, (c + 1) * Q)
            xg = act[rows, 0:GROUP_X]
            bg = act[rows, GROUP_X:GROUP_X + D_STATE].astype(BF16)
            cg = act[rows, GROUP_X + D_STATE:GROUP_CONV].astype(BF16)
            rowsT_c = rowsT_ref.at[c]
            ydx, eacs, xddT = _ssd_group(g, xg, bg, cg, acs_ref[rows, :], rowsT_c,
                                         dsk_ref[:, g * GROUP_X:(g + 1) * GROUP_X], mask)
            hg = h_ref[g]
            y = ydx + eacs * _dot_nt(cg, hg.astype(BF16))
            yg_ref[rows, g * GROUP_X:(g + 1) * GROUP_X] = _gate_norm(
                y, zxg[rows, 0:GROUP_X], ng_ref[:, g * GROUP_X:(g + 1) * GROUP_X]).astype(BF16)
            scale = _head_rows([jnp.exp2(rowsT_c[ROW_ACS, HEADS_PER_GROUP * g + r:HEADS_PER_GROUP * g + r + 1, Q - 1:Q])
                                for r in range(HEADS_PER_GROUP)])
            h_ref[g] = scale * hg + _dot(xddT.astype(BF16), bg)

        hist = hist_ref[g]
        conv_ref[0, :, g * GROUP_X:(g + 1) * GROUP_X] = hist[SUBLANES - (D_CONV - 1):, 0:GROUP_X]
        conv_ref[0, :, D_INNER + g * D_STATE:D_INNER + (g + 1) * D_STATE] = (
            hist[SUBLANES - (D_CONV - 1):, GROUP_X:GROUP_X + D_STATE])
        conv_ref[0, :, D_INNER + (SSM_GROUPS + g) * D_STATE:D_INNER + (SSM_GROUPS + g + 1) * D_STATE] = (
            hist[SUBLANES - (D_CONV - 1):, GROUP_X + D_STATE:GROUP_CONV])

        if g % 2 == 1:
            cols = slice((g - 1) * GROUP_X, (g + 1) * GROUP_X)
            y_ref[0] = (x_ref[0] if g == 1 else y_ref[0]) + _dot(yg_ref[:, cols], wout_ref[cols, :])

    @pl.when(j == pl.num_programs(1) - 1)
    def _():
        ssm_ref[0] = h_ref[...]


def _mamba_weight_specs(w):
    return [_const_spec(a.shape) for a in w]


def _mamba_prompt_call(x, gmix, w, tq=512):
    b, seq, _ = x.shape
    xspec = pl.BlockSpec((1, tq, D_MODEL), lambda i, j: (i, j, 0))
    n_chunks = tq // Q
    return pl.pallas_call(
        functools.partial(_mamba_prompt_body, tq=tq),
        grid=(b, seq // tq),
        in_specs=[xspec, _const_spec((1, D_MODEL))] + _mamba_weight_specs(w),
        out_specs=[xspec,
                   pl.BlockSpec((1, SSM_GROUPS, GROUP_X, D_STATE), lambda i, j: (i, 0, 0, 0)),
                   pl.BlockSpec((1, D_CONV - 1, CONV_DIM), lambda i, j: (i, 0, 0))],
        out_shape=[jax.ShapeDtypeStruct(x.shape, F32),
                   jax.ShapeDtypeStruct((b, SSM_GROUPS, GROUP_X, D_STATE), F32),
                   jax.ShapeDtypeStruct((b, D_CONV - 1, CONV_DIM), F32)],
        scratch_shapes=[
            pltpu.VMEM((tq, D_MODEL), BF16),
            pltpu.VMEM((tq, LANES), F32),
            pltpu.VMEM((tq, LANES), F32),
            pltpu.VMEM((n_chunks, 3, LANES, Q), F32),
            pltpu.VMEM((2, tq, GROUP_IN), F32),
            pltpu.VMEM((2, tq + SUBLANES, GROUP_CONV), F32),
            pltpu.VMEM((2, tq, GROUP_CONV), F32),
            pltpu.VMEM((SSM_GROUPS, SUBLANES, GROUP_CONV), F32),
            pltpu.VMEM((SSM_GROUPS, GROUP_X, D_STATE), F32),
            pltpu.VMEM((tq, D_INNER), BF16),
        ],
        compiler_params=_params(2),
        name="mamba_prompt",
    )(x, gmix, *w)


def _mamba_sample_body(x_ref, gmix_ref, wg_ref, wdt_ref, cw_ref, cb_ref, dtb_ref, alog_ref, dsk_ref, ng_ref, wout_ref,
                       cst_ref, sst_ref, gx_ref, wq_ref,
                       y_ref, q_ref, ssm_ref, conv_ref,
                       xn_ref, dt_ref, acs_ref, rowsT_ref, zxg_ref, xpad_ref,
                       ydx_ref, eacs_ref, xddT_ref, bc_ref, z_ref, scale_ref, yg_ref, *, seq, sub_b):
    s = pl.program_id(1)
    nb = Q // seq
    mask = _causal_mask(seq)

    @pl.when(s == 0)
    def _():
        def seg_last(v):
            v3 = v.reshape(nb, seq, v.shape[-1])
            return jnp.broadcast_to(v3[:, seq - 1:seq, :], v3.shape).reshape(v.shape)

        xn_ref[...] = _rms(x_ref[...], gmix_ref[...]).astype(BF16)
        _dt_stage(xn_ref, wdt_ref, dtb_ref, alog_ref, dt_ref, acs_ref, rowsT_ref, mask, 1, seg_last)
        rowsT_c = rowsT_ref.at[0]
        acsT = rowsT_c[ROW_ACS]
        for bi in range(nb):
            scale_ref[bi] = jnp.broadcast_to(jnp.exp2(acsT[:, (bi + 1) * seq - 1:(bi + 1) * seq]), (LANES, LANES))

        for g in range(SSM_GROUPS):
            zxg_ref[...] = _dot(xn_ref[...], wg_ref[g])
            col_slices = (
                (slice(g * GROUP_X, (g + 1) * GROUP_X), slice(0, GROUP_X)),
                (slice(D_INNER + g * D_STATE, D_INNER + (g + 1) * D_STATE), slice(GROUP_X, GROUP_X + D_STATE)),
                (slice(D_INNER + (SSM_GROUPS + g) * D_STATE, D_INNER + (SSM_GROUPS + g + 1) * D_STATE),
                 slice(GROUP_X + D_STATE, GROUP_CONV)),
            )
            for src, dst in col_slices:
                xpad_ref[:, SUBLANES - (D_CONV - 1):SUBLANES, dst] = cst_ref[:, :, src]
            xpad_ref[:, SUBLANES:SUBLANES + seq, :] = zxg_ref[:, GROUP_X:GROUP_IN].reshape(nb, seq, GROUP_CONV)
            act = _conv_rows(
                lambda k: xpad_ref[:, pl.ds(SUBLANES - (D_CONV - 1) + k, seq), :].reshape(Q, GROUP_CONV),
                cw_ref, cb_ref, g)
            for src, dst in col_slices:
                conv_ref[:, :, src] = xpad_ref[:, pl.ds(SUBLANES + seq - (D_CONV - 1), D_CONV - 1), dst]
            xg = act[:, 0:GROUP_X]
            bg = act[:, GROUP_X:GROUP_X + D_STATE].astype(BF16)
            cg = act[:, GROUP_X + D_STATE:GROUP_CONV].astype(BF16)
            ydx, eacs, xddT = _ssd_group(g, xg, bg, cg, acs_ref[...], rowsT_c,
                                         dsk_ref[:, g * GROUP_X:(g + 1) * GROUP_X], mask)
            ydx_ref[g] = ydx
            eacs_ref[g] = eacs
            xddT_ref[g] = xddT
            bc_ref[g, 0] = bg
            bc_ref[g, 1] = cg
            z_ref[g] = zxg_ref[:, 0:GROUP_X]

    rows_per = sub_b * seq
    r0 = pl.multiple_of(s * rows_per, rows_per)
    rows = pl.ds(r0, rows_per)
    row_b = _div_pow2(lax.broadcasted_iota(jnp.int32, (rows_per, GROUP_X), 0), seq)
    lane_b = _div_pow2(lax.broadcasted_iota(jnp.int32, (GROUP_X, Q), 1), seq)
    for g in range(SSM_GROUPS):
        cg = bc_ref[g, 1, rows, :]
        bg = bc_ref[g, 0]
        xddT = xddT_ref[g]
        yoff = jnp.zeros((rows_per, GROUP_X), F32)
        for bi in range(sub_b):
            hg = sst_ref[bi, g]
            yoff = jnp.where(row_b == bi, _dot_nt(cg, hg.astype(BF16)), yoff)
            sc = scale_ref[s * sub_b + bi]
            scale = _head_rows([sc[HEADS_PER_GROUP * g + r:HEADS_PER_GROUP * g + r + 1, :]
                                for r in range(HEADS_PER_GROUP)])
            xm = jnp.where(lane_b == s * sub_b + bi, xddT, 0.0).astype(BF16)
            ssm_ref[bi, g] = scale * hg + _dot(xm, bg)
        y = ydx_ref[g, rows, :] + eacs_ref[g, rows, :] * yoff
        yg_ref[rows, g * GROUP_X:(g + 1) * GROUP_X] = _gate_norm(
            y, z_ref[g, rows, :], ng_ref[:, g * GROUP_X:(g + 1) * GROUP_X]).astype(BF16)

    @pl.when(s == pl.num_programs(1) - 1)
    def _():
        x1 = x_ref[...] + _dot(yg_ref[...], wout_ref[...])
        y_ref[...] = x1
        q_ref[...] = _dot(_rms(x1, gx_ref[...]).astype(BF16), wq_ref[...])


def _mamba_sample_call(x2d, gmix, w, conv_state, ssm_state, gx, wq, seq, sub_b=4):
    n = x2d.shape[0]
    batch = n // seq
    nb = Q // seq
    n_sub = nb // sub_b
    xspec = pl.BlockSpec((Q, D_MODEL), lambda i, s: (i, 0))
    stspec = pl.BlockSpec((sub_b, SSM_GROUPS, GROUP_X, D_STATE), lambda i, s: (i * n_sub + s, 0, 0, 0))
    cspec = pl.BlockSpec((nb, D_CONV - 1, CONV_DIM), lambda i, s: (i, 0, 0))
    return pl.pallas_call(
        functools.partial(_mamba_sample_body, seq=seq, sub_b=sub_b),
        grid=(n // Q, n_sub),
        in_specs=[xspec, _const_spec((1, D_MODEL))] + _mamba_weight_specs(w)
        + [cspec, stspec, _const_spec((1, D_MODEL)), _const_spec(wq.shape)],
        out_specs=[xspec, xspec, stspec, cspec],
        out_shape=[jax.ShapeDtypeStruct(x2d.shape, F32), jax.ShapeDtypeStruct(x2d.shape, F32),
                   jax.ShapeDtypeStruct((batch, SSM_GROUPS, GROUP_X, D_STATE), F32),
                   jax.ShapeDtypeStruct((batch, D_CONV - 1, CONV_DIM), F32)],
        scratch_shapes=[
            pltpu.VMEM((Q, D_MODEL), BF16),
            pltpu.VMEM((Q, LANES), F32),
            pltpu.VMEM((Q, LANES), F32),
            pltpu.VMEM((1, 3, LANES, Q), F32),
            pltpu.VMEM((Q, GROUP_IN), F32),
            pltpu.VMEM((nb, SUBLANES + seq, GROUP_CONV), F32),
            pltpu.VMEM((SSM_GROUPS, Q, GROUP_X), F32),
            pltpu.VMEM((SSM_GROUPS, Q, GROUP_X), F32),
            pltpu.VMEM((SSM_GROUPS, GROUP_X, Q), F32),
            pltpu.VMEM((SSM_GROUPS, 2, Q, D_STATE), BF16),
            pltpu.VMEM((SSM_GROUPS, Q, GROUP_X), F32),
            pltpu.VMEM((nb, LANES, LANES), F32),
            pltpu.VMEM((Q, D_INNER), BF16),
        ],
        compiler_params=_params(2),
        name="mamba_sample",
    )(x2d, gmix, *w, conv_state, ssm_state, gx, wq)


def _pool_sample_body(x_ref, st_ref, gmix_ref, wpool_ref, pscale_ref, gx_ref, wq_ref,
                      y_ref, q_ref, pool_ref, ext_ref, *, seq, pos0):
    nb = x_ref.shape[0]
    n = nb * seq
    x = x_ref[...].reshape(n, D_MODEL)
    u = _rms(x, gmix_ref[...])
    hist0 = 2 * SUBLANES - POOL_BUF
    ext_ref[:, pl.ds(hist0, POOL_BUF), :] = st_ref[...]
    ext_ref[:, 2 * SUBLANES:2 * SUBLANES + seq, :] = u.reshape(nb, seq, D_MODEL)
    pos = pos0 + lax.broadcasted_iota(jnp.int32, (nb, seq, 1), 1).reshape(n, 1)
    pooled = _pool_windows(
        lambda k, cols: ext_ref[:, pl.ds(2 * SUBLANES - k, seq), cols].reshape(n, POOL_GROUP_DIM), u, pos, n)
    x1 = x + _pool_mix(pooled, wpool_ref, pscale_ref[...])
    pool_ref[...] = ext_ref[:, pl.ds(hist0 + seq, POOL_BUF), :]
    y_ref[...] = x1.reshape(nb, seq, D_MODEL)
    q_ref[...] = _dot(_rms(x1, gx_ref[...]).astype(BF16), wq_ref[...]).reshape(nb, seq, D_MODEL)


def _pool_sample_call(x, pool_state, gmix, wpool, pscale, gx, wq, pos0, nb=16):
    batch, seq, _ = x.shape
    xspec = pl.BlockSpec((nb, seq, D_MODEL), lambda i: (i, 0, 0))
    pspec = pl.BlockSpec((nb, POOL_BUF, D_MODEL), lambda i: (i, 0, 0))
    vec = _const_spec((1, D_MODEL))
    return pl.pallas_call(
        functools.partial(_pool_sample_body, seq=seq, pos0=pos0),
        grid=(batch // nb,),
        in_specs=[xspec, pspec, vec, _const_spec(wpool.shape), vec, vec, _const_spec(wq.shape)],
        out_specs=[xspec, xspec, pspec],
        out_shape=[jax.ShapeDtypeStruct(x.shape, F32), jax.ShapeDtypeStruct(x.shape, F32),
                   jax.ShapeDtypeStruct(pool_state.shape, F32)],
        scratch_shapes=[pltpu.VMEM((nb, 2 * SUBLANES + seq, D_MODEL), F32)],
        compiler_params=_params(1),
        name="pool_sample",
    )(x, pool_state, gmix, wpool, pscale, gx, wq)


HEAD_HALVES = XATTN_HEAD_DIM // LANES
KV_ROWS = N_MEM * XATTN_HEADS * HEAD_HALVES


def _attn_sample_body(q_ref, k_ref, v_ref, o_ref):
    nb, n_rows, _ = q_ref.shape
    half = n_rows // HEAD_HALVES
    grp = XATTN_HEADS * HEAD_HALVES
    lane = lax.broadcasted_iota(jnp.int32, (half, KV_ROWS), 1)
    rowi = lax.broadcasted_iota(jnp.int32, (half, KV_ROWS), 0)
    valid = (lane & (grp - 1)) == (rowi & (XATTN_HEADS - 1))
    for bi in range(nb):
        q2 = (q_ref[bi] * (XATTN_HEAD_DIM ** -0.5)).astype(BF16)
        prod = _dot_nt(q2, k_ref[0, bi].astype(BF16))
        s = prod[0:half] + pltpu.roll(prod[half:], KV_ROWS - XATTN_HEADS, axis=1)
        s = jnp.where(valid, s, -jnp.inf)
        p = jnp.exp(s - jnp.max(s, axis=-1, keepdims=True))
        p = p * (1.0 / jnp.sum(p, axis=-1, keepdims=True))
        pe = jnp.concatenate([p, pltpu.roll(p, XATTN_HEADS, axis=1)], axis=0).astype(BF16)
        o_ref[bi] = _dot(pe, v_ref[0, bi].astype(BF16))


def _cache_rows(cache):
    d, b = cache.shape[:2]
    c = cache.reshape(d, b, N_MEM, XATTN_HEADS, HEAD_HALVES, LANES)
    return c.transpose(0, 1, 2, 4, 3, 5).reshape(d, b, KV_ROWS, LANES)


def _attn_sample_call(q, k_rows, v_rows, layer, nb=4):
    assert HEAD_HALVES == 2
    batch, seq, _ = q.shape
    n_rows = seq * XATTN_HEADS * HEAD_HALVES
    q2 = q.reshape(batch, seq, XATTN_HEADS, HEAD_HALVES, LANES).transpose(0, 3, 1, 2, 4).reshape(batch, n_rows, LANES)
    qspec = pl.BlockSpec((nb, n_rows, LANES), lambda i: (i, 0, 0))
    kvspec = pl.BlockSpec((1, nb, KV_ROWS, LANES), lambda i: (layer, i, 0, 0))
    o2 = pl.pallas_call(
        _attn_sample_body,
        grid=(batch // nb,),
        in_specs=[qspec, kvspec, kvspec],
        out_specs=qspec,
        out_shape=jax.ShapeDtypeStruct(q2.shape, F32),
        compiler_params=_params(1),
        name="attn_sample",
    )(q2, k_rows, v_rows)
    o = o2.reshape(batch, HEAD_HALVES, seq, XATTN_HEADS, LANES).transpose(0, 2, 3, 1, 4)
    return o.reshape(batch, seq, D_MODEL)


def _out_mlp_body(*refs, with_final):
    it = iter(refs)
    x_ref, o_ref, wo_ref, gm_ref, wup_ref, wdn_ref = (next(it) for _ in range(6))
    if with_final:
        gf_ref = next(it)
    y_ref = next(it)
    x = x_ref[...] + _dot(o_ref[...].astype(BF16), wo_ref[...])
    x = _mlp(x, gm_ref[...], wup_ref, wdn_ref)
    if with_final:
        x = _rms(x, gf_ref[...])
    y_ref[...] = x


def _out_mlp_call(x2d, o2d, wo, gm, wup, wdn, gfinal=None, tm=512):
    n = x2d.shape[0]
    tm = min(tm, n)
    row = pl.BlockSpec((tm, D_MODEL), lambda i: (i, 0))
    vec = _const_spec((1, D_MODEL))
    args = [x2d, o2d, wo, gm, wup, wdn]
    specs = [row, row, _const_spec(wo.shape), vec, _const_spec(wup.shape), _const_spec(wdn.shape)]
    if gfinal is not None:
        args.append(gfinal)
        specs.append(vec)
    return pl.pallas_call(
        functools.partial(_out_mlp_body, with_final=gfinal is not None),
        grid=(n // tm,),
        in_specs=specs,
        out_specs=row,
        out_shape=jax.ShapeDtypeStruct(x2d.shape, F32),
        compiler_params=_params(1),
        name="out_mlp",
    )(*args)


def _mamba_weights(w_in, conv_w, conv_b, dt_bias, a_log, d_skip, norm_gated, w_out):
    z, x = w_in[:, :D_INNER], w_in[:, D_INNER:2 * D_INNER]
    bm = w_in[:, 2 * D_INNER:2 * D_INNER + SSM_GROUPS * D_STATE]
    cm = w_in[:, 2 * D_INNER + SSM_GROUPS * D_STATE:D_INNER + CONV_DIM]
    wdt = w_in[:, D_INNER + CONV_DIM:]

    def grp(a, width):
        return a.reshape(a.shape[0], SSM_GROUPS, width)

    wg = jnp.concatenate([grp(z, GROUP_X), grp(x, GROUP_X), grp(bm, D_STATE), grp(cm, D_STATE)], axis=-1)
    wg = wg.transpose(1, 0, 2).astype(BF16)
    pad = LANES - SSM_HEADS
    wdt = jnp.pad(wdt, ((0, 0), (0, pad))).astype(BF16)

    def conv_grp(a):
        return jnp.concatenate([grp(a[:, :D_INNER], GROUP_X), grp(a[:, D_INNER:D_INNER + SSM_GROUPS * D_STATE], D_STATE),
                                grp(a[:, D_INNER + SSM_GROUPS * D_STATE:], D_STATE)], axis=-1).transpose(1, 0, 2)

    cw = conv_grp(conv_w)
    cb = conv_grp(conv_b[None, :])
    dtb = jnp.pad(dt_bias, (0, pad))[None, :]
    alog = jnp.pad(a_log, (0, pad))[None, :]
    dsk = jnp.repeat(d_skip, SSM_HEADDIM)[None, :]
    return (wg, wdt, cw, cb, dtb, alog, dsk, norm_gated[None, :], w_out.astype(BF16))


def kernel(x_prompt, x_sample, cache_mem_k, cache_mem_v, state_ssm, state_conv, state_pool, mem_prompt, norm_mix, norm_xattn, norm_mem, norm_mlp, norm_final, w_in, conv_w, conv_b, dt_bias, a_log, d_skip, norm_gated, w_out, w_pool, pool_scale, w_xq, w_xk, w_xv, w_xo, w_up, w_down):
    bp, seq_p, _ = x_prompt.shape
    bs, seq_s, _ = x_sample.shape
    depth = w_xq.shape[0]
    row = lambda a: a[None, :]

    wq, wk, wv, wo = (a.astype(BF16) for a in (w_xq, w_xk, w_xv, w_xo))
    wup, wdn = w_up.astype(BF16), w_down.astype(BF16)
    wpool = w_pool.astype(BF16)
    mw = _mamba_weights(w_in[0], conv_w[0], conv_b[0], dt_bias[0], a_log[0], d_skip[0], norm_gated[0], w_out[0])

    k_p, v_p, kb, vb = _kv_call(mem_prompt.reshape(bp * N_MEM, D_MODEL), norm_mem[:, None, :], wk, wv)
    kv_shape = (depth, bp, N_MEM, XATTN_HEADS, XATTN_HEAD_DIM)
    kb = kb.reshape(depth, bp, N_MEM, D_MODEL)
    vb = vb.reshape(depth, bp, N_MEM, D_MODEL)

    xp, ssm_p, conv_p = _mamba_prompt_call(x_prompt, row(norm_mix[0]), mw)
    xp = _prompt_layer_call(xp, 0, kb, vb, row(norm_xattn[0]), wq[0], wo[0], row(norm_mlp[0]), wup[0], wdn[0])
    y_prompt, pool_p = _prompt_layer_call(
        xp, 1, kb, vb, row(norm_xattn[1]), wq[1], wo[1], row(norm_mlp[1]), wup[1], wdn[1],
        pool=(row(norm_mix[1]), wpool[0], row(pool_scale[0])), gfinal=row(norm_final))

    kc, vc = _cache_rows(cache_mem_k), _cache_rows(cache_mem_v)
    xs = x_sample.reshape(bs * seq_s, D_MODEL)
    ssm_in = state_ssm[0].reshape(bs, SSM_GROUPS, GROUP_X, D_STATE)
    xs, q, ssm_s, conv_s = _mamba_sample_call(xs, row(norm_mix[0]), mw, state_conv[0], ssm_in,
                                              row(norm_xattn[0]), wq[0], seq_s)
    o = _attn_sample_call(q.reshape(bs, seq_s, D_MODEL), kc, vc, 0)
    xs = _out_mlp_call(xs, o.reshape(bs * seq_s, D_MODEL), wo[0], row(norm_mlp[0]), wup[0], wdn[0])
    xs, q, pool_s = _pool_sample_call(xs.reshape(bs, seq_s, D_MODEL), state_pool[0], row(norm_mix[1]), wpool[0],
                                      row(pool_scale[0]), row(norm_xattn[1]), wq[1], PAST_LEN)
    o = _attn_sample_call(q, kc, vc, 1)
    y_sample = _out_mlp_call(xs.reshape(bs * seq_s, D_MODEL), o.reshape(bs * seq_s, D_MODEL), wo[1],
                             row(norm_mlp[1]), wup[1], wdn[1], gfinal=row(norm_final))

    ssm_shape = (1, -1, SSM_HEADS, SSM_HEADDIM, D_STATE)
    return (y_prompt, y_sample.reshape(bs, seq_s, D_MODEL),
            k_p.reshape(kv_shape), v_p.reshape(kv_shape),
            ssm_p.reshape(ssm_shape), conv_p[None], pool_p[None],
            ssm_s.reshape(ssm_shape), conv_s[None], pool_s[None])
```

```python
import functools

import jax
import jax.numpy as jnp
from jax import lax
from jax.experimental import pallas as pl
from jax.experimental.pallas import tpu as pltpu

F32 = jnp.float32
BF16 = jnp.bfloat16

D_MODEL = 1024
D_INNER = 2048
SSM_HEADS = 32
SSM_HEADDIM = 64
SSM_GROUPS = 8
HEADS_PER_GROUP = 4
GROUP_X = HEADS_PER_GROUP * SSM_HEADDIM
D_STATE = 128
D_CONV = 4
CONV_DIM = D_INNER + 2 * SSM_GROUPS * D_STATE
GROUP_CONV = GROUP_X + 2 * D_STATE
GROUP_IN = GROUP_X + GROUP_CONV
POOL_WINDOWS = (2, 4, 8, 16)
POOL_GROUP_DIM = 256
POOL_BUF = 15
N_MEM = 256
XATTN_HEADS = 4
XATTN_HEAD_DIM = 256
D_FF = 4096
EPS = 1e-5
PAST_LEN = 16384
LOG2_E = 1.4426950408889634

LANES = 128
SUBLANES = 8
Q = 128
FF_CHUNK = 1024
VMEM_LIMIT = 56 * 1024 * 1024

_NT = (((1,), (1,)), ((), ()))


def _params(n_grid):
    return pltpu.CompilerParams(dimension_semantics=("arbitrary",) * n_grid, vmem_limit_bytes=VMEM_LIMIT)


def _const_spec(shape):
    nd = len(shape)
    return pl.BlockSpec(shape, lambda *_: (0,) * nd, pipeline_mode=pl.Buffered(1))


def _dot(a, b):
    return jnp.dot(a, b, preferred_element_type=F32)


def _dot_nt(a, b):
    return lax.dot_general(a, b, _NT, preferred_element_type=F32)


def _rms(x, g):
    return x * lax.rsqrt(jnp.mean(x * x, axis=-1, keepdims=True) + EPS) * g


def _silu(x):
    h = 0.5 * x
    return h + h * jnp.tanh(h)


def _kv_body(mem_ref, g_ref, wk_ref, wv_ref, k_ref, v_ref, kb_ref, vb_ref):
    mn = _rms(mem_ref[...], g_ref[0]).astype(BF16)
    k = _dot(mn, wk_ref[0])
    v = _dot(mn, wv_ref[0])
    for h in range(XATTN_HEADS):
        sl = slice(h * XATTN_HEAD_DIM, (h + 1) * XATTN_HEAD_DIM)
        k_ref[0, :, h, :] = k[:, sl]
        v_ref[0, :, h, :] = v[:, sl]
    kb_ref[0] = k.astype(BF16)
    vb_ref[0] = v.astype(BF16)


def _kv_call(mem, g_mem, wk, wv, tm=512):
    n = mem.shape[0]
    depth = wk.shape[0]
    row = pl.BlockSpec((tm, D_MODEL), lambda l, t: (t, 0))
    wspec = pl.BlockSpec((1, D_MODEL, D_MODEL), lambda l, t: (l, 0, 0))
    ospec = pl.BlockSpec((1, tm, D_MODEL), lambda l, t: (l, t, 0))
    hspec = pl.BlockSpec((1, tm, XATTN_HEADS, XATTN_HEAD_DIM), lambda l, t: (l, t, 0, 0))
    return pl.pallas_call(
        _kv_body,
        grid=(depth, n // tm),
        in_specs=[row, pl.BlockSpec((1, 1, D_MODEL), lambda l, t: (l, 0, 0)), wspec, wspec],
        out_specs=[hspec, hspec, ospec, ospec],
        out_shape=[jax.ShapeDtypeStruct((depth, n, XATTN_HEADS, XATTN_HEAD_DIM), F32)] * 2
        + [jax.ShapeDtypeStruct((depth, n, D_MODEL), BF16)] * 2,
        compiler_params=_params(2),
        name="mem_kv",
    )(mem, g_mem, wk, wv)


def _attn_heads(q, k_of, v_of):
    outs = []
    for h in range(XATTN_HEADS):
        sl = slice(h * XATTN_HEAD_DIM, (h + 1) * XATTN_HEAD_DIM)
        s = _dot_nt(q[:, sl].astype(BF16), k_of(h)) * (XATTN_HEAD_DIM ** -0.5)
        p = jnp.exp(s - jnp.max(s, axis=-1, keepdims=True))
        p = p * (1.0 / jnp.sum(p, axis=-1, keepdims=True))
        outs.append(_dot(p.astype(BF16), v_of(h)))
    return jnp.concatenate(outs, axis=-1)


def _mlp(x, g, wup_ref, wdn_ref):
    hn = _rms(x, g).astype(BF16)
    acc = x
    for c in range(D_FF // FF_CHUNK):
        a = jnp.maximum(_dot(hn, wup_ref[:, c * FF_CHUNK:(c + 1) * FF_CHUNK]), 0.0)
        acc = acc + _dot((a * a).astype(BF16), wdn_ref[c * FF_CHUNK:(c + 1) * FF_CHUNK, :])
    return acc


def _pool_windows(load, u, pos, n):
    outs = []
    for gi, w in enumerate(POOL_WINDOWS):
        cols = slice(gi * POOL_GROUP_DIM, (gi + 1) * POOL_GROUP_DIM)
        acc = load(0, cols)
        for k in range(1, w):
            acc = acc + load(k, cols)
        cnt = jnp.minimum(pos + 1, w).astype(F32)
        outs.append(acc / cnt - u[:, cols])
    return outs


def _pool_windows_rolled(ext_ref, u, pos, n):
    outs = []
    for gi, w in enumerate(POOL_WINDOWS):
        assert w & (w - 1) == 0 and w <= 2 * SUBLANES
        cols = slice(gi * POOL_GROUP_DIM, (gi + 1) * POOL_GROUP_DIM)
        acc = ext_ref[0:n + 2 * SUBLANES, cols]
        span = 1
        while span < w:
            acc = acc + pltpu.roll(acc, span, axis=0)
            span *= 2
        cnt = jnp.minimum(pos + 1, w).astype(F32)
        outs.append(acc[2 * SUBLANES:, :] / cnt - u[:, cols])
    return outs


def _pool_mix(pooled, wpool_ref, scale):
    outs = [_dot(p.astype(BF16), wpool_ref[gi]) for gi, p in enumerate(pooled)]
    return jnp.concatenate(outs, axis=-1) * scale


def _prompt_layer_body(*refs, tq, with_pool, with_final):
    it = iter(refs)
    x_ref = next(it)
    if with_pool:
        gmix_ref, wpool_ref, pscale_ref = next(it), next(it), next(it)
    gx_ref, wq_ref, kb_ref, vb_ref, wo_ref = next(it), next(it), next(it), next(it), next(it)
    gm_ref, wup_ref, wdn_ref = next(it), next(it), next(it)
    if with_final:
        gf_ref = next(it)
    y_ref = next(it)
    if with_pool:
        pool_ref, ext_ref = next(it), next(it)

    x = x_ref[0]
    if with_pool:
        j = pl.program_id(1)

        @pl.when(j == 0)
        def _():
            ext_ref[0:2 * SUBLANES, :] = jnp.zeros((2 * SUBLANES, D_MODEL), F32)

        u = _rms(x, gmix_ref[...])
        ext_ref[pl.ds(2 * SUBLANES, tq), :] = u
        pos = j * tq + lax.broadcasted_iota(jnp.int32, (tq, 1), 0)
        pooled = _pool_windows_rolled(ext_ref, u, pos, tq)
        x = x + _pool_mix(pooled, wpool_ref, pscale_ref[...])
        pool_ref[0] = ext_ref[pl.ds(tq + 1, POOL_BUF), :]
        ext_ref[0:2 * SUBLANES, :] = ext_ref[pl.ds(tq, 2 * SUBLANES), :]

    q = _dot(_rms(x, gx_ref[...]).astype(BF16), wq_ref[...])
    o = _attn_heads(
        q,
        lambda h: kb_ref[0, 0, :, h * XATTN_HEAD_DIM:(h + 1) * XATTN_HEAD_DIM],
        lambda h: vb_ref[0, 0, :, h * XATTN_HEAD_DIM:(h + 1) * XATTN_HEAD_DIM],
    )
    x = x + _dot(o.astype(BF16), wo_ref[...])
    x = _mlp(x, gm_ref[...], wup_ref, wdn_ref)
    if with_final:
        x = _rms(x, gf_ref[...])
    y_ref[0] = x


def _prompt_layer_call(x, layer, kb, vb, gx, wq, wo, gm, wup, wdn, pool=None, gfinal=None, tq=512):
    b, seq, _ = x.shape
    with_pool, with_final = pool is not None, gfinal is not None
    xspec = pl.BlockSpec((1, tq, D_MODEL), lambda i, j: (i, j, 0))
    vec = _const_spec((1, D_MODEL))
    kvspec = pl.BlockSpec((1, 1, N_MEM, D_MODEL), lambda i, j: (layer, i, 0, 0))
    args, specs = [x], [xspec]
    if with_pool:
        gmix, wpool, pscale = pool
        args += [gmix, wpool, pscale]
        specs += [vec, _const_spec(wpool.shape), vec]
    args += [gx, wq, kb, vb, wo, gm, wup, wdn]
    specs += [vec, _const_spec(wq.shape), kvspec, kvspec, _const_spec(wo.shape), vec,
              _const_spec(wup.shape), _const_spec(wdn.shape)]
    if with_final:
        args.append(gfinal)
        specs.append(vec)
    out_shape = [jax.ShapeDtypeStruct(x.shape, F32)]
    out_specs = [xspec]
    scratch = []
    if with_pool:
        out_shape.append(jax.ShapeDtypeStruct((b, POOL_BUF, D_MODEL), F32))
        out_specs.append(pl.BlockSpec((1, POOL_BUF, D_MODEL), lambda i, j: (i, 0, 0)))
        scratch.append(pltpu.VMEM((tq + 2 * SUBLANES, D_MODEL), F32))
    outs = pl.pallas_call(
        functools.partial(_prompt_layer_body, tq=tq, with_pool=with_pool, with_final=with_final),
        grid=(b, seq // tq),
        in_specs=specs,
        out_specs=out_specs,
        out_shape=out_shape,
        scratch_shapes=scratch,
        compiler_params=_params(2),
        name="prompt_layer_pool" if with_pool else "prompt_layer",
    )(*args)
    return outs if with_pool else outs[0]


ROW_ACS, ROW_DECAY, ROW_END = 0, 1, 2


def _dt_stage(xn_ref, wdt_ref, dtb_ref, alog_ref, dt_ref, acs_ref, rowsT_ref, mask, n_chunks, seg_last):
    z = _dot(xn_ref[...], wdt_ref[...]) + dtb_ref[...]
    dt_ref[...] = jnp.maximum(z, 0.0) + jnp.log1p(jnp.exp(-jnp.abs(z)))
    a = -jnp.exp(alog_ref[...]) * LOG2_E
    tri = mask.astype(F32)
    for c in range(n_chunks):
        rows = slice(c * Q, (c + 1) * Q)
        dt_c = dt_ref[rows, :]
        acs = jnp.dot(tri, dt_c * a, precision=lax.Precision.HIGHEST, preferred_element_type=F32)
        acs_ref[rows, :] = acs
        acsT, dtT = acs.T, dt_c.T
        rowsT_ref[c, ROW_ACS] = acsT
        rowsT_ref[c, ROW_DECAY] = acsT - jnp.log2(dtT)
        rowsT_ref[c, ROW_END] = dtT * jnp.exp2(seg_last(acs).T - acsT)


def _conv_rows(load_shifted, cw_ref, cb_ref, g):
    acc = cb_ref[g]
    for k in range(D_CONV):
        acc = acc + cw_ref[g, k:k + 1, :] * load_shifted(k)
    return _silu(acc)


def _conv_rolled(ext, cw_ref, cb_ref, g):
    acc = cb_ref[g] + cw_ref[g, D_CONV - 1:D_CONV, :] * ext
    for s in range(1, D_CONV):
        acc = acc + cw_ref[g, D_CONV - 1 - s:D_CONV - s, :] * pltpu.roll(ext, s, axis=0)
    return _silu(acc[SUBLANES:, :])


def _expand_heads(cols):
    half = lax.broadcasted_iota(jnp.int32, (Q, LANES), 1) < SSM_HEADDIM
    return jnp.concatenate([jnp.where(half, cols[0], cols[1]), jnp.where(half, cols[2], cols[3])], axis=1)


def _head_lane_masks():
    lane_head = _div_pow2(lax.broadcasted_iota(jnp.int32, (Q, GROUP_X), 1), SSM_HEADDIM)
    return [jnp.where(lane_head == r, 1.0, 0.0).astype(BF16) for r in range(HEADS_PER_GROUP)]


def _ssd_group(g, xgs, bgs, cgs, acs_cs, rowsT_cs, dskip_g, mask, head_masks):
    heads = [HEADS_PER_GROUP * g + r for r in range(HEADS_PER_GROUP)]
    n = len(xgs)
    acs_cols = [[jnp.broadcast_to(a[:, h:h + 1], (Q, LANES)) for h in heads] for a in acs_cs]
    cbs = [_dot_nt(cg, bg) for cg, bg in zip(cgs, bgs)]
    xbs = [xg.astype(BF16) for xg in xgs]
    lcats = [jnp.concatenate(
        [(cbs[i] * jnp.exp2(jnp.where(mask, acs_cols[i][r] - rowsT_cs[i][ROW_DECAY, h:h + 1, :], -jnp.inf))).astype(BF16)
         for r, h in enumerate(heads)], axis=1) for i in range(n)]
    xblks = [jnp.concatenate([xb * hm for hm in head_masks], axis=0) for xb in xbs]
    ydxs = [_dot(l, xb) + dskip_g * xg for l, xb, xg in zip(lcats, xblks, xgs)]
    xTs = [xg.T for xg in xgs]
    xddTs = [jnp.concatenate(
        [xTs[i][r * SSM_HEADDIM:(r + 1) * SSM_HEADDIM, :] * rowsT_cs[i][ROW_END, h:h + 1, :]
         for r, h in enumerate(heads)], axis=0) for i in range(n)]
    eacss = [jnp.exp2(_expand_heads(cols)) for cols in acs_cols]
    return ydxs, eacss, xddTs


def _gate_norm(y, z, ng):
    yg = y * _silu(z)
    return yg * lax.rsqrt(jnp.mean(yg * yg, axis=-1, keepdims=True) + EPS) * ng


def _div_pow2(v, d):
    assert d & (d - 1) == 0
    return lax.shift_right_logical(v, d.bit_length() - 1)


def _causal_mask(block):
    li = lax.broadcasted_iota(jnp.int32, (Q, Q), 0)
    si = lax.broadcasted_iota(jnp.int32, (Q, Q), 1)
    m = li >= si
    if block is not None:
        m = jnp.logical_and(m, _div_pow2(li, block) == _div_pow2(si, block))
    return m


def _head_rows(vals):
    return jnp.concatenate([jnp.broadcast_to(v, (SSM_HEADDIM, LANES)) for v in vals], axis=0)


def _mamba_prompt_body(x_ref, gmix_ref, wg_ref, wdt_ref, cw_ref, cb_ref, dtb_ref, alog_ref, dsk_ref, ng_ref, wout_ref,
                       y_ref, ssm_ref, conv_ref,
                       xn_ref, dt_ref, acs_ref, rowsT_ref, zxg_ref, xpad_ref, act_ref, hist_ref, h_ref, yg_ref, *, tq):
    j = pl.program_id(1)
    n_chunks = tq // Q

    @pl.when(j == 0)
    def _():
        hist_ref[...] = jnp.zeros_like(hist_ref)
        h_ref[...] = jnp.zeros_like(h_ref)

    mask = _causal_mask(None)
    head_masks = _head_lane_masks()
    xn_ref[...] = _rms(x_ref[0], gmix_ref[...]).astype(BF16)
    _dt_stage(xn_ref, wdt_ref, dtb_ref, alog_ref, dt_ref, acs_ref, rowsT_ref, mask, n_chunks,
              lambda v: jnp.broadcast_to(v[Q - 1:Q, :], v.shape))

    def in_proj(g):
        zxg_ref[g % 2] = _dot(xn_ref[...], wg_ref[g])

    in_proj(0)
    for g in range(SSM_GROUPS):
        if g + 1 < SSM_GROUPS:
            in_proj(g + 1)
        zxg, xpad, act = zxg_ref.at[g % 2], xpad_ref.at[g % 2], act_ref.at[g % 2]
        xpad[0:SUBLANES, :] = hist_ref[g]
        xpad[pl.ds(SUBLANES, tq), :] = zxg[:, GROUP_X:GROUP_IN]

        for c in range(n_chunks):
            act[c * Q:(c + 1) * Q, :] = _conv_rolled(xpad[c * Q:(c + 1) * Q + SUBLANES, :], cw_ref, cb_ref, g)
        hist_ref[g] = xpad[pl.ds(tq, SUBLANES), :]

        chunks = range(n_chunks)
        rows = [slice(c * Q, (c + 1) * Q) for c in chunks]
        bgs = [act[r, GROUP_X:GROUP_X + D_STATE].astype(BF16) for r in rows]
        cgs = [act[r, GROUP_X + D_STATE:GROUP_CONV].astype(BF16) for r in rows]
        rowsT_cs = [rowsT_ref.at[c] for c in chunks]
        ydxs, eacss, xddTs = _ssd_group(
            g, [act[r, 0:GROUP_X] for r in rows], bgs, cgs, [acs_ref[r, :] for r in rows], rowsT_cs,
            dsk_ref[:, g * GROUP_X:(g + 1) * GROUP_X], mask, head_masks)
        updates = [_dot(xddTs[c].astype(BF16), bgs[c]) for c in chunks]
        scales = [_head_rows([jnp.exp2(rowsT_cs[c][ROW_ACS, h:h + 1, Q - 1:Q])
                              for h in range(HEADS_PER_GROUP * g, HEADS_PER_GROUP * (g + 1))]) for c in chunks]
        hs = [h_ref[g]]
        for c in chunks:
            hs.append(scales[c] * hs[c] + updates[c])
        h_ref[g] = hs[n_chunks]
        yoffs = [_dot_nt(cgs[c], hs[c].astype(BF16)) for c in chunks]
        for c in chunks:
            yg_ref[rows[c], g * GROUP_X:(g + 1) * GROUP_X] = _gate_norm(
                ydxs[c] + eacss[c] * yoffs[c], zxg[rows[c], 0:GROUP_X],
                ng_ref[:, g * GROUP_X:(g + 1) * GROUP_X]).astype(BF16)

        hist = hist_ref[g]
        conv_ref[0, :, g * GROUP_X:(g + 1) * GROUP_X] = hist[SUBLANES - (D_CONV - 1):, 0:GROUP_X]
        conv_ref[0, :, D_INNER + g * D_STATE:D_INNER + (g + 1) * D_STATE] = (
            hist[SUBLANES - (D_CONV - 1):, GROUP_X:GROUP_X + D_STATE])
        conv_ref[0, :, D_INNER + (SSM_GROUPS + g) * D_STATE:D_INNER + (SSM_GROUPS + g + 1) * D_STATE] = (
            hist[SUBLANES - (D_CONV - 1):, GROUP_X + D_STATE:GROUP_CONV])

        if g % 2 == 1:
            cols = slice((g - 1) * GROUP_X, (g + 1) * GROUP_X)
            y_ref[0] = (x_ref[0] if g == 1 else y_ref[0]) + _dot(yg_ref[:, cols], wout_ref[cols, :])

    @pl.when(j == pl.num_programs(1) - 1)
    def _():
        ssm_ref[0] = h_ref[...]


def _mamba_weight_specs(w):
    return [_const_spec(a.shape) for a in w]


def _mamba_prompt_call(x, gmix, w, tq=512):
    b, seq, _ = x.shape
    xspec = pl.BlockSpec((1, tq, D_MODEL), lambda i, j: (i, j, 0))
    n_chunks = tq // Q
    return pl.pallas_call(
        functools.partial(_mamba_prompt_body, tq=tq),
        grid=(b, seq // tq),
        in_specs=[xspec, _const_spec((1, D_MODEL))] + _mamba_weight_specs(w),
        out_specs=[xspec,
                   pl.BlockSpec((1, SSM_GROUPS, GROUP_X, D_STATE), lambda i, j: (i, 0, 0, 0)),
                   pl.BlockSpec((1, D_CONV - 1, CONV_DIM), lambda i, j: (i, 0, 0))],
        out_shape=[jax.ShapeDtypeStruct(x.shape, F32),
                   jax.ShapeDtypeStruct((b, SSM_GROUPS, GROUP_X, D_STATE), F32),
                   jax.ShapeDtypeStruct((b, D_CONV - 1, CONV_DIM), F32)],
        scratch_shapes=[
            pltpu.VMEM((tq, D_MODEL), BF16),
            pltpu.VMEM((tq, LANES), F32),
            pltpu.VMEM((tq, LANES), F32),
            pltpu.VMEM((n_chunks, 3, LANES, Q), F32),
            pltpu.VMEM((2, tq, GROUP_IN), F32),
            pltpu.VMEM((2, tq + SUBLANES, GROUP_CONV), F32),
            pltpu.VMEM((2, tq, GROUP_CONV), F32),
            pltpu.VMEM((SSM_GROUPS, SUBLANES, GROUP_CONV), F32),
            pltpu.VMEM((SSM_GROUPS, GROUP_X, D_STATE), F32),
            pltpu.VMEM((tq, D_INNER), BF16),
        ],
        compiler_params=_params(2),
        name="mamba_prompt",
    )(x, gmix, *w)


def _mamba_sample_body(x_ref, gmix_ref, wg_ref, wdt_ref, cw_ref, cb_ref, dtb_ref, alog_ref, dsk_ref, ng_ref, wout_ref,
                       cst_ref, sst_ref, gx_ref, wq_ref,
                       y_ref, q_ref, ssm_ref, conv_ref,
                       xn_ref, dt_ref, acs_ref, rowsT_ref, zxg_ref, xpad_ref,
                       ydx_ref, eacs_ref, xddT_ref, bc_ref, z_ref, scale_ref, yg_ref, *, seq, sub_b):
    s = pl.program_id(1)
    nb = Q // seq
    mask = _causal_mask(seq)
    head_masks = _head_lane_masks()

    @pl.when(s == 0)
    def _():
        def seg_last(v):
            v3 = v.reshape(nb, seq, v.shape[-1])
            return jnp.broadcast_to(v3[:, seq - 1:seq, :], v3.shape).reshape(v.shape)

        xn_ref[...] = _rms(x_ref[...], gmix_ref[...]).astype(BF16)
        _dt_stage(xn_ref, wdt_ref, dtb_ref, alog_ref, dt_ref, acs_ref, rowsT_ref, mask, 1, seg_last)
        rowsT_c = rowsT_ref.at[0]
        acsT = rowsT_c[ROW_ACS]
        for bi in range(nb):
            scale_ref[bi] = jnp.broadcast_to(jnp.exp2(acsT[:, (bi + 1) * seq - 1:(bi + 1) * seq]), (LANES, LANES))

        for g in range(SSM_GROUPS):
            zxg_ref[...] = _dot(xn_ref[...], wg_ref[g])
            col_slices = (
                (slice(g * GROUP_X, (g + 1) * GROUP_X), slice(0, GROUP_X)),
                (slice(D_INNER + g * D_STATE, D_INNER + (g + 1) * D_STATE), slice(GROUP_X, GROUP_X + D_STATE)),
                (slice(D_INNER + (SSM_GROUPS + g) * D_STATE, D_INNER + (SSM_GROUPS + g + 1) * D_STATE),
                 slice(GROUP_X + D_STATE, GROUP_CONV)),
            )
            for src, dst in col_slices:
                xpad_ref[:, SUBLANES - (D_CONV - 1):SUBLANES, dst] = cst_ref[:, :, src]
            xpad_ref[:, SUBLANES:SUBLANES + seq, :] = zxg_ref[:, GROUP_X:GROUP_IN].reshape(nb, seq, GROUP_CONV)
            act = _conv_rows(
                lambda k: xpad_ref[:, pl.ds(SUBLANES - (D_CONV - 1) + k, seq), :].reshape(Q, GROUP_CONV),
                cw_ref, cb_ref, g)
            for src, dst in col_slices:
                conv_ref[:, :, src] = xpad_ref[:, pl.ds(SUBLANES + seq - (D_CONV - 1), D_CONV - 1), dst]
            xg = act[:, 0:GROUP_X]
            bg = act[:, GROUP_X:GROUP_X + D_STATE].astype(BF16)
            cg = act[:, GROUP_X + D_STATE:GROUP_CONV].astype(BF16)
            ydxs, eacss, xddTs = _ssd_group(g, [xg], [bg], [cg], [acs_ref[...]], [rowsT_c],
                                            dsk_ref[:, g * GROUP_X:(g + 1) * GROUP_X], mask, head_masks)
            ydx_ref[g] = ydxs[0]
            eacs_ref[g] = eacss[0]
            xddT_ref[g] = xddTs[0]
            bc_ref[g, 0] = bg
            bc_ref[g, 1] = cg
            z_ref[g] = zxg_ref[:, 0:GROUP_X]

    rows_per = sub_b * seq
    r0 = pl.multiple_of(s * rows_per, rows_per)
    rows = pl.ds(r0, rows_per)
    row_b = _div_pow2(lax.broadcasted_iota(jnp.int32, (rows_per, GROUP_X), 0), seq)
    lane_b = _div_pow2(lax.broadcasted_iota(jnp.int32, (GROUP_X, Q), 1), seq)
    for g in range(SSM_GROUPS):
        cg = bc_ref[g, 1, rows, :]
        bg = bc_ref[g, 0]
        xddT = xddT_ref[g]
        yoff = jnp.zeros((rows_per, GROUP_X), F32)
        for bi in range(sub_b):
            hg = sst_ref[bi, g]
            yoff = jnp.where(row_b == bi, _dot_nt(cg, hg.astype(BF16)), yoff)
            sc = scale_ref[s * sub_b + bi]
            scale = _head_rows([sc[HEADS_PER_GROUP * g + r:HEADS_PER_GROUP * g + r + 1, :]
                                for r in range(HEADS_PER_GROUP)])
            xm = jnp.where(lane_b == s * sub_b + bi, xddT, 0.0).astype(BF16)
            ssm_ref[bi, g] = scale * hg + _dot(xm, bg)
        y = ydx_ref[g, rows, :] + eacs_ref[g, rows, :] * yoff
        yg_ref[rows, g * GROUP_X:(g + 1) * GROUP_X] = _gate_norm(
            y, z_ref[g, rows, :], ng_ref[:, g * GROUP_X:(g + 1) * GROUP_X]).astype(BF16)

    @pl.when(s == pl.num_programs(1) - 1)
    def _():
        x1 = x_ref[...] + _dot(yg_ref[...], wout_ref[...])
        y_ref[...] = x1
        q_ref[...] = _dot(_rms(x1, gx_ref[...]).astype(BF16), wq_ref[...])


def _mamba_sample_call(x2d, gmix, w, conv_state, ssm_state, gx, wq, seq, sub_b=4):
    n = x2d.shape[0]
    batch = n // seq
    nb = Q // seq
    n_sub = nb // sub_b
    xspec = pl.BlockSpec((Q, D_MODEL), lambda i, s: (i, 0))
    stspec = pl.BlockSpec((sub_b, SSM_GROUPS, GROUP_X, D_STATE), lambda i, s: (i * n_sub + s, 0, 0, 0))
    cspec = pl.BlockSpec((nb, D_CONV - 1, CONV_DIM), lambda i, s: (i, 0, 0))
    return pl.pallas_call(
        functools.partial(_mamba_sample_body, seq=seq, sub_b=sub_b),
        grid=(n // Q, n_sub),
        in_specs=[xspec, _const_spec((1, D_MODEL))] + _mamba_weight_specs(w)
        + [cspec, stspec, _const_spec((1, D_MODEL)), _const_spec(wq.shape)],
        out_specs=[xspec, xspec, stspec, cspec],
        out_shape=[jax.ShapeDtypeStruct(x2d.shape, F32), jax.ShapeDtypeStruct(x2d.shape, F32),
                   jax.ShapeDtypeStruct((batch, SSM_GROUPS, GROUP_X, D_STATE), F32),
                   jax.ShapeDtypeStruct((batch, D_CONV - 1, CONV_DIM), F32)],
        scratch_shapes=[
            pltpu.VMEM((Q, D_MODEL), BF16),
            pltpu.VMEM((Q, LANES), F32),
            pltpu.VMEM((Q, LANES), F32),
            pltpu.VMEM((1, 3, LANES, Q), F32),
            pltpu.VMEM((Q, GROUP_IN), F32),
            pltpu.VMEM((nb, SUBLANES + seq, GROUP_CONV), F32),
            pltpu.VMEM((SSM_GROUPS, Q, GROUP_X), F32),
            pltpu.VMEM((SSM_GROUPS, Q, GROUP_X), F32),
            pltpu.VMEM((SSM_GROUPS, GROUP_X, Q), F32),
            pltpu.VMEM((SSM_GROUPS, 2, Q, D_STATE), BF16),
            pltpu.VMEM((SSM_GROUPS, Q, GROUP_X), F32),
            pltpu.VMEM((nb, LANES, LANES), F32),
            pltpu.VMEM((Q, D_INNER), BF16),
        ],
        compiler_params=_params(2),
        name="mamba_sample",
    )(x2d, gmix, *w, conv_state, ssm_state, gx, wq)


def _pool_sample_body(x_ref, st_ref, gmix_ref, wpool_ref, pscale_ref, gx_ref, wq_ref,
                      y_ref, q_ref, pool_ref, ext_ref, *, seq, pos0):
    nb = x_ref.shape[0]
    n = nb * seq
    x = x_ref[...].reshape(n, D_MODEL)
    u = _rms(x, gmix_ref[...])
    hist0 = 2 * SUBLANES - POOL_BUF
    ext_ref[:, pl.ds(hist0, POOL_BUF), :] = st_ref[...]
    ext_ref[:, 2 * SUBLANES:2 * SUBLANES + seq, :] = u.reshape(nb, seq, D_MODEL)
    pos = pos0 + lax.broadcasted_iota(jnp.int32, (nb, seq, 1), 1).reshape(n, 1)
    pooled = _pool_windows(
        lambda k, cols: ext_ref[:, pl.ds(2 * SUBLANES - k, seq), cols].reshape(n, POOL_GROUP_DIM), u, pos, n)
    x1 = x + _pool_mix(pooled, wpool_ref, pscale_ref[...])
    pool_ref[...] = ext_ref[:, pl.ds(hist0 + seq, POOL_BUF), :]
    y_ref[...] = x1.reshape(nb, seq, D_MODEL)
    q_ref[...] = _dot(_rms(x1, gx_ref[...]).astype(BF16), wq_ref[...]).reshape(nb, seq, D_MODEL)


def _pool_sample_call(x, pool_state, gmix, wpool, pscale, gx, wq, pos0, nb=16):
    batch, seq, _ = x.shape
    xspec = pl.BlockSpec((nb, seq, D_MODEL), lambda i: (i, 0, 0))
    pspec = pl.BlockSpec((nb, POOL_BUF, D_MODEL), lambda i: (i, 0, 0))
    vec = _const_spec((1, D_MODEL))
    return pl.pallas_call(
        functools.partial(_pool_sample_body, seq=seq, pos0=pos0),
        grid=(batch // nb,),
        in_specs=[xspec, pspec, vec, _const_spec(wpool.shape), vec, vec, _const_spec(wq.shape)],
        out_specs=[xspec, xspec, pspec],
        out_shape=[jax.ShapeDtypeStruct(x.shape, F32), jax.ShapeDtypeStruct(x.shape, F32),
                   jax.ShapeDtypeStruct(pool_state.shape, F32)],
        scratch_shapes=[pltpu.VMEM((nb, 2 * SUBLANES + seq, D_MODEL), F32)],
        compiler_params=_params(1),
        name="pool_sample",
    )(x, pool_state, gmix, wpool, pscale, gx, wq)


HEAD_HALVES = XATTN_HEAD_DIM // LANES
KV_ROWS = N_MEM * XATTN_HEADS * HEAD_HALVES


def _attn_sample_body(q_ref, k_ref, v_ref, o_ref):
    nb, n_rows, _ = q_ref.shape
    half = n_rows // HEAD_HALVES
    grp = XATTN_HEADS * HEAD_HALVES
    lane = lax.broadcasted_iota(jnp.int32, (half, KV_ROWS), 1)
    rowi = lax.broadcasted_iota(jnp.int32, (half, KV_ROWS), 0)
    valid = (lane & (grp - 1)) == (rowi & (XATTN_HEADS - 1))
    prods = [_dot_nt((q_ref[bi] * (XATTN_HEAD_DIM ** -0.5)).astype(BF16), k_ref[0, bi].astype(BF16))
             for bi in range(nb)]
    ss = [jnp.where(valid, pr[0:half] + pltpu.roll(pr[half:], KV_ROWS - XATTN_HEADS, axis=1), -jnp.inf)
          for pr in prods]
    es = [jnp.exp(s - jnp.max(s, axis=-1, keepdims=True)) for s in ss]
    ps = [e * (1.0 / jnp.sum(e, axis=-1, keepdims=True)) for e in es]
    pes = [jnp.concatenate([p, pltpu.roll(p, XATTN_HEADS, axis=1)], axis=0).astype(BF16) for p in ps]
    for bi in range(nb):
        o_ref[bi] = _dot(pes[bi], v_ref[0, bi].astype(BF16))


def _cache_rows(cache):
    d, b = cache.shape[:2]
    c = cache.reshape(d, b, N_MEM, XATTN_HEADS, HEAD_HALVES, LANES)
    return c.transpose(0, 1, 2, 4, 3, 5).reshape(d, b, KV_ROWS, LANES)


def _attn_sample_call(q, k_rows, v_rows, layer, nb=4):
    assert HEAD_HALVES == 2
    batch, seq, _ = q.shape
    n_rows = seq * XATTN_HEADS * HEAD_HALVES
    q2 = q.reshape(batch, seq, XATTN_HEADS, HEAD_HALVES, LANES).transpose(0, 3, 1, 2, 4).reshape(batch, n_rows, LANES)
    qspec = pl.BlockSpec((nb, n_rows, LANES), lambda i: (i, 0, 0))
    kvspec = pl.BlockSpec((1, nb, KV_ROWS, LANES), lambda i: (layer, i, 0, 0))
    o2 = pl.pallas_call(
        _attn_sample_body,
        grid=(batch // nb,),
        in_specs=[qspec, kvspec, kvspec],
        out_specs=qspec,
        out_shape=jax.ShapeDtypeStruct(q2.shape, F32),
        compiler_params=_params(1),
        name="attn_sample",
    )(q2, k_rows, v_rows)
    o = o2.reshape(batch, HEAD_HALVES, seq, XATTN_HEADS, LANES).transpose(0, 2, 3, 1, 4)
    return o.reshape(batch, seq, D_MODEL)


def _out_mlp_body(*refs, with_final):
    it = iter(refs)
    x_ref, o_ref, wo_ref, gm_ref, wup_ref, wdn_ref = (next(it) for _ in range(6))
    if with_final:
        gf_ref = next(it)
    y_ref = next(it)
    x = x_ref[...] + _dot(o_ref[...].astype(BF16), wo_ref[...])
    x = _mlp(x, gm_ref[...], wup_ref, wdn_ref)
    if with_final:
        x = _rms(x, gf_ref[...])
    y_ref[...] = x


def _out_mlp_call(x2d, o2d, wo, gm, wup, wdn, gfinal=None, tm=512):
    n = x2d.shape[0]
    tm = min(tm, n)
    row = pl.BlockSpec((tm, D_MODEL), lambda i: (i, 0))
    vec = _const_spec((1, D_MODEL))
    args = [x2d, o2d, wo, gm, wup, wdn]
    specs = [row, row, _const_spec(wo.shape), vec, _const_spec(wup.shape), _const_spec(wdn.shape)]
    if gfinal is not None:
        args.append(gfinal)
        specs.append(vec)
    return pl.pallas_call(
        functools.partial(_out_mlp_body, with_final=gfinal is not None),
        grid=(n // tm,),
        in_specs=specs,
        out_specs=row,
        out_shape=jax.ShapeDtypeStruct(x2d.shape, F32),
        compiler_params=_params(1),
        name="out_mlp",
    )(*args)


def _mamba_weights(w_in, conv_w, conv_b, dt_bias, a_log, d_skip, norm_gated, w_out):
    z, x = w_in[:, :D_INNER], w_in[:, D_INNER:2 * D_INNER]
    bm = w_in[:, 2 * D_INNER:2 * D_INNER + SSM_GROUPS * D_STATE]
    cm = w_in[:, 2 * D_INNER + SSM_GROUPS * D_STATE:D_INNER + CONV_DIM]
    wdt = w_in[:, D_INNER + CONV_DIM:]

    def grp(a, width):
        return a.reshape(a.shape[0], SSM_GROUPS, width)

    wg = jnp.concatenate([grp(z, GROUP_X), grp(x, GROUP_X), grp(bm, D_STATE), grp(cm, D_STATE)], axis=-1)
    wg = wg.transpose(1, 0, 2).astype(BF16)
    pad = LANES - SSM_HEADS
    wdt = jnp.pad(wdt, ((0, 0), (0, pad))).astype(BF16)

    def conv_grp(a):
        return jnp.concatenate([grp(a[:, :D_INNER], GROUP_X), grp(a[:, D_INNER:D_INNER + SSM_GROUPS * D_STATE], D_STATE),
                                grp(a[:, D_INNER + SSM_GROUPS * D_STATE:], D_STATE)], axis=-1).transpose(1, 0, 2)

    cw = conv_grp(conv_w)
    cb = conv_grp(conv_b[None, :])
    dtb = jnp.pad(dt_bias, (0, pad))[None, :]
    alog = jnp.pad(a_log, (0, pad))[None, :]
    dsk = jnp.repeat(d_skip, SSM_HEADDIM)[None, :]
    return (wg, wdt, cw, cb, dtb, alog, dsk, norm_gated[None, :], w_out.astype(BF16))


def kernel(x_prompt, x_sample, cache_mem_k, cache_mem_v, state_ssm, state_conv, state_pool, mem_prompt, norm_mix, norm_xattn, norm_mem, norm_mlp, norm_final, w_in, conv_w, conv_b, dt_bias, a_log, d_skip, norm_gated, w_out, w_pool, pool_scale, w_xq, w_xk, w_xv, w_xo, w_up, w_down):
    bp, seq_p, _ = x_prompt.shape
    bs, seq_s, _ = x_sample.shape
    depth = w_xq.shape[0]
    row = lambda a: a[None, :]

    wq, wk, wv, wo = (a.astype(BF16) for a in (w_xq, w_xk, w_xv, w_xo))
    wup, wdn = w_up.astype(BF16), w_down.astype(BF16)
    wpool = w_pool.astype(BF16)
    mw = _mamba_weights(w_in[0], conv_w[0], conv_b[0], dt_bias[0], a_log[0], d_skip[0], norm_gated[0], w_out[0])

    k_p, v_p, kb, vb = _kv_call(mem_prompt.reshape(bp * N_MEM, D_MODEL), norm_mem[:, None, :], wk, wv)
    kv_shape = (depth, bp, N_MEM, XATTN_HEADS, XATTN_HEAD_DIM)
    kb = kb.reshape(depth, bp, N_MEM, D_MODEL)
    vb = vb.reshape(depth, bp, N_MEM, D_MODEL)

    xp, ssm_p, conv_p = _mamba_prompt_call(x_prompt, row(norm_mix[0]), mw)
    xp = _prompt_layer_call(xp, 0, kb, vb, row(norm_xattn[0]), wq[0], wo[0], row(norm_mlp[0]), wup[0], wdn[0])
    y_prompt, pool_p = _prompt_layer_call(
        xp, 1, kb, vb, row(norm_xattn[1]), wq[1], wo[1], row(norm_mlp[1]), wup[1], wdn[1],
        pool=(row(norm_mix[1]), wpool[0], row(pool_scale[0])), gfinal=row(norm_final))

    kc, vc = _cache_rows(cache_mem_k), _cache_rows(cache_mem_v)
    xs = x_sample.reshape(bs * seq_s, D_MODEL)
    ssm_in = state_ssm[0].reshape(bs, SSM_GROUPS, GROUP_X, D_STATE)
    xs, q, ssm_s, conv_s = _mamba_sample_call(xs, row(norm_mix[0]), mw, state_conv[0], ssm_in,
                                              row(norm_xattn[0]), wq[0], seq_s)
    o = _attn_sample_call(q.reshape(bs, seq_s, D_MODEL), kc, vc, 0)
    xs = _out_mlp_call(xs, o.reshape(bs * seq_s, D_MODEL), wo[0], row(norm_mlp[0]), wup[0], wdn[0])
    xs, q, pool_s = _pool_sample_call(xs.reshape(bs, seq_s, D_MODEL), state_pool[0], row(norm_mix[1]), wpool[0],
                                      row(pool_scale[0]), row(norm_xattn[1]), wq[1], PAST_LEN)
    o = _attn_sample_call(q, kc, vc, 1)
    y_sample = _out_mlp_call(xs.reshape(bs * seq_s, D_MODEL), o.reshape(bs * seq_s, D_MODEL), wo[1],
                             row(norm_mlp[1]), wup[1], wdn[1], gfinal=row(norm_final))

    ssm_shape = (1, -1, SSM_HEADS, SSM_HEADDIM, D_STATE)
    return (y_prompt, y_sample.reshape(bs, seq_s, D_MODEL),
            k_p.reshape(kv_shape), v_p.reshape(kv_shape),
            ssm_p.reshape(ssm_shape), conv_p[None], pool_p[None],
            ssm_s.reshape(ssm_shape), conv_s[None], pool_s[None])
```

```python
import functools

import jax
import jax.numpy as jnp
from jax import lax
from jax.experimental import pallas as pl
from jax.experimental.pallas import tpu as pltpu

F32 = jnp.float32
BF16 = jnp.bfloat16

D_MODEL = 1024
D_INNER = 2048
SSM_HEADS = 32
SSM_HEADDIM = 64
SSM_GROUPS = 8
HEADS_PER_GROUP = 4
GROUP_X = HEADS_PER_GROUP * SSM_HEADDIM
D_STATE = 128
D_CONV = 4
CONV_DIM = D_INNER + 2 * SSM_GROUPS * D_STATE
GROUP_CONV = GROUP_X + 2 * D_STATE
GROUP_IN = GROUP_X + GROUP_CONV
POOL_WINDOWS = (2, 4, 8, 16)
POOL_GROUP_DIM = 256
POOL_BUF = 15
N_MEM = 256
XATTN_HEADS = 4
XATTN_HEAD_DIM = 256
D_FF = 4096
EPS = 1e-5
PAST_LEN = 16384
LOG2_E = 1.4426950408889634

LANES = 128
SUBLANES = 8
Q = 128
SSD_STAGE = 2
FF_CHUNK = 1024
VMEM_LIMIT = 56 * 1024 * 1024

_NT = (((1,), (1,)), ((), ()))


def _params(n_grid):
    return pltpu.CompilerParams(dimension_semantics=("arbitrary",) * n_grid, vmem_limit_bytes=VMEM_LIMIT)


def _const_spec(shape):
    nd = len(shape)
    return pl.BlockSpec(shape, lambda *_: (0,) * nd, pipeline_mode=pl.Buffered(1))


def _dot(a, b):
    return jnp.dot(a, b, preferred_element_type=F32)


def _dot_nt(a, b):
    return lax.dot_general(a, b, _NT, preferred_element_type=F32)


def _rms(x, g):
    return x * lax.rsqrt(jnp.mean(x * x, axis=-1, keepdims=True) + EPS) * g


def _silu(x):
    h = 0.5 * x
    return h + h * jnp.tanh(h)


def _kv_body(mem_ref, g_ref, wk_ref, wv_ref, k_ref, v_ref, kb_ref, vb_ref):
    mn = _rms(mem_ref[...], g_ref[0]).astype(BF16)
    k = _dot(mn, wk_ref[0])
    v = _dot(mn, wv_ref[0])
    for h in range(XATTN_HEADS):
        sl = slice(h * XATTN_HEAD_DIM, (h + 1) * XATTN_HEAD_DIM)
        k_ref[0, :, h, :] = k[:, sl]
        v_ref[0, :, h, :] = v[:, sl]
    kb_ref[0] = k.astype(BF16)
    vb_ref[0] = v.astype(BF16)


def _kv_call(mem, g_mem, wk, wv, tm=512):
    n = mem.shape[0]
    depth = wk.shape[0]
    row = pl.BlockSpec((tm, D_MODEL), lambda l, t: (t, 0))
    wspec = pl.BlockSpec((1, D_MODEL, D_MODEL), lambda l, t: (l, 0, 0))
    ospec = pl.BlockSpec((1, tm, D_MODEL), lambda l, t: (l, t, 0))
    hspec = pl.BlockSpec((1, tm, XATTN_HEADS, XATTN_HEAD_DIM), lambda l, t: (l, t, 0, 0))
    return pl.pallas_call(
        _kv_body,
        grid=(depth, n // tm),
        in_specs=[row, pl.BlockSpec((1, 1, D_MODEL), lambda l, t: (l, 0, 0)), wspec, wspec],
        out_specs=[hspec, hspec, ospec, ospec],
        out_shape=[jax.ShapeDtypeStruct((depth, n, XATTN_HEADS, XATTN_HEAD_DIM), F32)] * 2
        + [jax.ShapeDtypeStruct((depth, n, D_MODEL), BF16)] * 2,
        compiler_params=_params(2),
        name="mem_kv",
    )(mem, g_mem, wk, wv)


def _attn_heads(q, k_of, v_of):
    outs = []
    for h in range(XATTN_HEADS):
        sl = slice(h * XATTN_HEAD_DIM, (h + 1) * XATTN_HEAD_DIM)
        s = _dot_nt(q[:, sl].astype(BF16), k_of(h)) * (XATTN_HEAD_DIM ** -0.5)
        p = jnp.exp(s - jnp.max(s, axis=-1, keepdims=True))
        p = p * (1.0 / jnp.sum(p, axis=-1, keepdims=True))
        outs.append(_dot(p.astype(BF16), v_of(h)))
    return jnp.concatenate(outs, axis=-1)


def _mlp(x, g, wup_ref, wdn_ref):
    hn = _rms(x, g).astype(BF16)
    acc = x
    for c in range(D_FF // FF_CHUNK):
        a = jnp.maximum(_dot(hn, wup_ref[:, c * FF_CHUNK:(c + 1) * FF_CHUNK]), 0.0)
        acc = acc + _dot((a * a).astype(BF16), wdn_ref[c * FF_CHUNK:(c + 1) * FF_CHUNK, :])
    return acc


def _pool_windows(load, u, pos, n):
    outs = []
    for gi, w in enumerate(POOL_WINDOWS):
        cols = slice(gi * POOL_GROUP_DIM, (gi + 1) * POOL_GROUP_DIM)
        acc = load(0, cols)
        for k in range(1, w):
            acc = acc + load(k, cols)
        cnt = jnp.minimum(pos + 1, w).astype(F32)
        outs.append(acc / cnt - u[:, cols])
    return outs


def _pool_windows_rolled(ext_ref, u, pos, n):
    outs = []
    for gi, w in enumerate(POOL_WINDOWS):
        assert w & (w - 1) == 0 and w <= 2 * SUBLANES
        cols = slice(gi * POOL_GROUP_DIM, (gi + 1) * POOL_GROUP_DIM)
        acc = ext_ref[0:n + 2 * SUBLANES, cols]
        span = 1
        while span < w:
            acc = acc + pltpu.roll(acc, span, axis=0)
            span *= 2
        cnt = jnp.minimum(pos + 1, w).astype(F32)
        outs.append(acc[2 * SUBLANES:, :] / cnt - u[:, cols])
    return outs


def _pool_mix(pooled, wpool_ref, scale):
    outs = [_dot(p.astype(BF16), wpool_ref[gi]) for gi, p in enumerate(pooled)]
    return jnp.concatenate(outs, axis=-1) * scale


def _prompt_layer_body(*refs, tq, with_pool, with_final, with_sample):
    it = iter(refs)
    x_ref = next(it)
    if with_pool:
        gmix_ref, wpool_ref, pscale_ref = next(it), next(it), next(it)
    gx_ref, wq_ref, kb_ref, vb_ref, wo_ref = next(it), next(it), next(it), next(it), next(it)
    gm_ref, wup_ref, wdn_ref = next(it), next(it), next(it)
    if with_final:
        gf_ref = next(it)
    if with_sample:
        qs_ref, ks_ref, vs_ref = next(it), next(it), next(it)
    y_ref = next(it)
    if with_pool:
        pool_ref = next(it)
    if with_sample:
        os_ref = next(it)
    if with_pool:
        ext_ref = next(it)

    if with_sample:
        _attn_sample_stage(qs_ref, ks_ref, vs_ref, os_ref)

    x = x_ref[0]
    if with_pool:
        j = pl.program_id(1)

        @pl.when(j == 0)
        def _():
            ext_ref[0:2 * SUBLANES, :] = jnp.zeros((2 * SUBLANES, D_MODEL), F32)

        u = _rms(x, gmix_ref[...])
        ext_ref[pl.ds(2 * SUBLANES, tq), :] = u
        pos = j * tq + lax.broadcasted_iota(jnp.int32, (tq, 1), 0)
        pooled = _pool_windows_rolled(ext_ref, u, pos, tq)
        x = x + _pool_mix(pooled, wpool_ref, pscale_ref[...])
        pool_ref[0] = ext_ref[pl.ds(tq + 1, POOL_BUF), :]
        ext_ref[0:2 * SUBLANES, :] = ext_ref[pl.ds(tq, 2 * SUBLANES), :]

    q = _dot(_rms(x, gx_ref[...]).astype(BF16), wq_ref[...])
    o = _attn_heads(
        q,
        lambda h: kb_ref[0, 0, :, h * XATTN_HEAD_DIM:(h + 1) * XATTN_HEAD_DIM],
        lambda h: vb_ref[0, 0, :, h * XATTN_HEAD_DIM:(h + 1) * XATTN_HEAD_DIM],
    )
    x = x + _dot(o.astype(BF16), wo_ref[...])
    x = _mlp(x, gm_ref[...], wup_ref, wdn_ref)
    if with_final:
        x = _rms(x, gf_ref[...])
    y_ref[0] = x


def _prompt_layer_call(x, layer, kb, vb, gx, wq, wo, gm, wup, wdn, pool=None, gfinal=None, sample=None, tq=512):
    b, seq, _ = x.shape
    with_pool, with_final, with_sample = pool is not None, gfinal is not None, sample is not None
    nj = seq // tq
    xspec = pl.BlockSpec((1, tq, D_MODEL), lambda i, j: (i, j, 0))
    vec = _const_spec((1, D_MODEL))
    kvspec = pl.BlockSpec((1, 1, N_MEM, D_MODEL), lambda i, j: (layer, i, 0, 0))
    args, specs = [x], [xspec]
    if with_pool:
        gmix, wpool, pscale = pool
        args += [gmix, wpool, pscale]
        specs += [vec, _const_spec(wpool.shape), vec]
    args += [gx, wq, kb, vb, wo, gm, wup, wdn]
    specs += [vec, _const_spec(wq.shape), kvspec, kvspec, _const_spec(wo.shape), vec,
              _const_spec(wup.shape), _const_spec(wdn.shape)]
    if with_final:
        args.append(gfinal)
        specs.append(vec)
    out_shape = [jax.ShapeDtypeStruct(x.shape, F32)]
    out_specs = [xspec]
    scratch = []
    if with_pool:
        out_shape.append(jax.ShapeDtypeStruct((b, POOL_BUF, D_MODEL), F32))
        out_specs.append(pl.BlockSpec((1, POOL_BUF, D_MODEL), lambda i, j: (i, 0, 0)))
        scratch.append(pltpu.VMEM((tq + 2 * SUBLANES, D_MODEL), F32))
    if with_sample:
        q, k_rows, v_rows = sample
        q2 = _attn_sample_rows(q)
        nb = q2.shape[0] // (b * nj)
        assert nb * b * nj == q2.shape[0]
        qspec = pl.BlockSpec((nb,) + q2.shape[1:], lambda i, j: (i * nj + j, 0, 0))
        cspec = pl.BlockSpec((1, nb, KV_ROWS, LANES), lambda i, j: (layer, i * nj + j, 0, 0))
        args += [q2, k_rows, v_rows]
        specs += [qspec, cspec, cspec]
        out_shape.append(jax.ShapeDtypeStruct(q2.shape, F32))
        out_specs.append(qspec)
    outs = pl.pallas_call(
        functools.partial(_prompt_layer_body, tq=tq, with_pool=with_pool, with_final=with_final,
                          with_sample=with_sample),
        grid=(b, nj),
        in_specs=specs,
        out_specs=out_specs,
        out_shape=out_shape,
        scratch_shapes=scratch,
        compiler_params=_params(2),
        name="prompt_layer_pool" if with_pool else "prompt_layer",
    )(*args)
    outs = list(outs)
    if with_sample:
        outs[-1] = _attn_sample_unrows(outs[-1], q.shape)
    return outs if len(outs) > 1 else outs[0]


ROW_ACS, ROW_DECAY, ROW_END = 0, 1, 2


def _dt_stage(xn_ref, wdt_ref, dtb_ref, alog_ref, dt_ref, acs_ref, rowsT_ref, mask, n_chunks, seg_last):
    z = _dot(xn_ref[...], wdt_ref[...]) + dtb_ref[...]
    dt_ref[...] = jnp.maximum(z, 0.0) + jnp.log1p(jnp.exp(-jnp.abs(z)))
    a = -jnp.exp(alog_ref[...]) * LOG2_E
    tri = mask.astype(F32)
    for c in range(n_chunks):
        rows = slice(c * Q, (c + 1) * Q)
        dt_c = dt_ref[rows, :]
        acs = jnp.dot(tri, dt_c * a, precision=lax.Precision.HIGHEST, preferred_element_type=F32)
        acs_ref[rows, :] = acs
        acsT, dtT = acs.T, dt_c.T
        rowsT_ref[c, ROW_ACS] = acsT
        rowsT_ref[c, ROW_DECAY] = acsT - jnp.log2(dtT)
        rowsT_ref[c, ROW_END] = dtT * jnp.exp2(seg_last(acs).T - acsT)


def _conv_rows(load_shifted, cw_ref, cb_ref, g):
    acc = cb_ref[g]
    for k in range(D_CONV):
        acc = acc + cw_ref[g, k:k + 1, :] * load_shifted(k)
    return _silu(acc)


def _conv_rolled(ext, cw_ref, cb_ref, g):
    acc = cb_ref[g] + cw_ref[g, D_CONV - 1:D_CONV, :] * ext
    for s in range(1, D_CONV):
        acc = acc + cw_ref[g, D_CONV - 1 - s:D_CONV - s, :] * pltpu.roll(ext, s, axis=0)
    return _silu(acc[SUBLANES:, :])


def _expand_heads(cols):
    half = lax.broadcasted_iota(jnp.int32, (Q, LANES), 1) < SSM_HEADDIM
    return jnp.concatenate([jnp.where(half, cols[0], cols[1]), jnp.where(half, cols[2], cols[3])], axis=1)


def _head_lane_masks():
    lane_head = _div_pow2(lax.broadcasted_iota(jnp.int32, (Q, GROUP_X), 1), SSM_HEADDIM)
    return [jnp.where(lane_head == r, 1.0, 0.0).astype(BF16) for r in range(HEADS_PER_GROUP)]


def _ssd_group(g, xgs, bgs, cgs, acs_cs, rowsT_cs, dskip_g, mask, head_masks):
    heads = [HEADS_PER_GROUP * g + r for r in range(HEADS_PER_GROUP)]
    n = len(xgs)
    acs_cols = [[jnp.broadcast_to(a[:, h:h + 1], (Q, LANES)) for h in heads] for a in acs_cs]
    cbs = [_dot_nt(cg, bg) for cg, bg in zip(cgs, bgs)]
    xbs = [xg.astype(BF16) for xg in xgs]
    lcats = [jnp.concatenate(
        [(cbs[i] * jnp.exp2(jnp.where(mask, acs_cols[i][r] - rowsT_cs[i][ROW_DECAY, h:h + 1, :], -jnp.inf))).astype(BF16)
         for r, h in enumerate(heads)], axis=1) for i in range(n)]
    xblks = [jnp.concatenate([xb * hm for hm in head_masks], axis=0) for xb in xbs]
    ydxs = [_dot(l, xb) + dskip_g * xg for l, xb, xg in zip(lcats, xblks, xgs)]
    xTs = [xg.T for xg in xgs]
    xddTs = [jnp.concatenate(
        [xTs[i][r * SSM_HEADDIM:(r + 1) * SSM_HEADDIM, :] * rowsT_cs[i][ROW_END, h:h + 1, :]
         for r, h in enumerate(heads)], axis=0) for i in range(n)]
    eacss = [jnp.exp2(_expand_heads(cols)) for cols in acs_cols]
    return ydxs, eacss, xddTs


def _gate_norm(y, z, ng):
    yg = y * _silu(z)
    return yg * lax.rsqrt(jnp.mean(yg * yg, axis=-1, keepdims=True) + EPS) * ng


def _div_pow2(v, d):
    assert d & (d - 1) == 0
    return lax.shift_right_logical(v, d.bit_length() - 1)


def _causal_mask(block):
    li = lax.broadcasted_iota(jnp.int32, (Q, Q), 0)
    si = lax.broadcasted_iota(jnp.int32, (Q, Q), 1)
    m = li >= si
    if block is not None:
        m = jnp.logical_and(m, _div_pow2(li, block) == _div_pow2(si, block))
    return m


def _head_rows(vals):
    return jnp.concatenate([jnp.broadcast_to(v, (SSM_HEADDIM, LANES)) for v in vals], axis=0)


def _mamba_prompt_body(x_ref, gmix_ref, wg_ref, wdt_ref, cw_ref, cb_ref, dtb_ref, alog_ref, dsk_ref, ng_ref, wout_ref,
                       y_ref, ssm_ref, conv_ref,
                       xn_ref, dt_ref, acs_ref, rowsT_ref, zxg_ref, xpad_ref, act_ref, hist_ref, h_ref, yg_ref, *, tq):
    j = pl.program_id(1)
    n_chunks = tq // Q

    @pl.when(j == 0)
    def _():
        hist_ref[...] = jnp.zeros_like(hist_ref)
        h_ref[...] = jnp.zeros_like(h_ref)

    mask = _causal_mask(None)
    head_masks = _head_lane_masks()
    xn_ref[...] = _rms(x_ref[0], gmix_ref[...]).astype(BF16)
    _dt_stage(xn_ref, wdt_ref, dtb_ref, alog_ref, dt_ref, acs_ref, rowsT_ref, mask, n_chunks,
              lambda v: jnp.broadcast_to(v[Q - 1:Q, :], v.shape))

    def in_proj(g):
        zxg_ref[g % 2] = _dot(xn_ref[...], wg_ref[g])

    in_proj(0)
    for g in range(SSM_GROUPS):
        if g + 1 < SSM_GROUPS:
            in_proj(g + 1)
        zxg, xpad, act = zxg_ref.at[g % 2], xpad_ref.at[g % 2], act_ref.at[g % 2]
        xpad[0:SUBLANES, :] = hist_ref[g]
        xpad[pl.ds(SUBLANES, tq), :] = zxg[:, GROUP_X:GROUP_IN]

        for c in range(n_chunks):
            act[c * Q:(c + 1) * Q, :] = _conv_rolled(xpad[c * Q:(c + 1) * Q + SUBLANES, :], cw_ref, cb_ref, g)
        hist_ref[g] = xpad[pl.ds(tq, SUBLANES), :]

        h_in = h_ref[g]
        for c0 in range(0, n_chunks, SSD_STAGE):
            chunks = list(range(c0, min(c0 + SSD_STAGE, n_chunks)))
            idx = range(len(chunks))
            rows = [slice(c * Q, (c + 1) * Q) for c in chunks]
            bgs = [act[r, GROUP_X:GROUP_X + D_STATE].astype(BF16) for r in rows]
            cgs = [act[r, GROUP_X + D_STATE:GROUP_CONV].astype(BF16) for r in rows]
            rowsT_cs = [rowsT_ref.at[c] for c in chunks]
            ydxs, eacss, xddTs = _ssd_group(
                g, [act[r, 0:GROUP_X] for r in rows], bgs, cgs, [acs_ref[r, :] for r in rows], rowsT_cs,
                dsk_ref[:, g * GROUP_X:(g + 1) * GROUP_X], mask, head_masks)
            updates = [_dot(xddTs[i].astype(BF16), bgs[i]) for i in idx]
            scales = [_head_rows([jnp.exp2(rowsT_cs[i][ROW_ACS, h:h + 1, Q - 1:Q])
                                  for h in range(HEADS_PER_GROUP * g, HEADS_PER_GROUP * (g + 1))]) for i in idx]
            hs = [h_in]
            for i in idx:
                hs.append(scales[i] * hs[i] + updates[i])
            h_in = hs[-1]
            yoffs = [_dot_nt(cgs[i], hs[i].astype(BF16)) for i in idx]
            for i in idx:
                yg_ref[rows[i], g * GROUP_X:(g + 1) * GROUP_X] = _gate_norm(
                    ydxs[i] + eacss[i] * yoffs[i], zxg[rows[i], 0:GROUP_X],
                    ng_ref[:, g * GROUP_X:(g + 1) * GROUP_X]).astype(BF16)
        h_ref[g] = h_in

        hist = hist_ref[g]
        conv_ref[0, :, g * GROUP_X:(g + 1) * GROUP_X] = hist[SUBLANES - (D_CONV - 1):, 0:GROUP_X]
        conv_ref[0, :, D_INNER + g * D_STATE:D_INNER + (g + 1) * D_STATE] = (
            hist[SUBLANES - (D_CONV - 1):, GROUP_X:GROUP_X + D_STATE])
        conv_ref[0, :, D_INNER + (SSM_GROUPS + g) * D_STATE:D_INNER + (SSM_GROUPS + g + 1) * D_STATE] = (
            hist[SUBLANES - (D_CONV - 1):, GROUP_X + D_STATE:GROUP_CONV])

        if g % 2 == 1:
            cols = slice((g - 1) * GROUP_X, (g + 1) * GROUP_X)
            y_ref[0] = (x_ref[0] if g == 1 else y_ref[0]) + _dot(yg_ref[:, cols], wout_ref[cols, :])

    @pl.when(j == pl.num_programs(1) - 1)
    def _():
        ssm_ref[0] = h_ref[...]


def _mamba_weight_specs(w):
    return [_const_spec(a.shape) for a in w]


def _mamba_prompt_call(x, gmix, w, tq=512):
    b, seq, _ = x.shape
    xspec = pl.BlockSpec((1, tq, D_MODEL), lambda i, j: (i, j, 0))
    n_chunks = tq // Q
    return pl.pallas_call(
        functools.partial(_mamba_prompt_body, tq=tq),
        grid=(b, seq // tq),
        in_specs=[xspec, _const_spec((1, D_MODEL))] + _mamba_weight_specs(w),
        out_specs=[xspec,
                   pl.BlockSpec((1, SSM_GROUPS, GROUP_X, D_STATE), lambda i, j: (i, 0, 0, 0)),
                   pl.BlockSpec((1, D_CONV - 1, CONV_DIM), lambda i, j: (i, 0, 0))],
        out_shape=[jax.ShapeDtypeStruct(x.shape, F32),
                   jax.ShapeDtypeStruct((b, SSM_GROUPS, GROUP_X, D_STATE), F32),
                   jax.ShapeDtypeStruct((b, D_CONV - 1, CONV_DIM), F32)],
        scratch_shapes=[
            pltpu.VMEM((tq, D_MODEL), BF16),
            pltpu.VMEM((tq, LANES), F32),
            pltpu.VMEM((tq, LANES), F32),
            pltpu.VMEM((n_chunks, 3, LANES, Q), F32),
            pltpu.VMEM((2, tq, GROUP_IN), F32),
            pltpu.VMEM((2, tq + SUBLANES, GROUP_CONV), F32),
            pltpu.VMEM((2, tq, GROUP_CONV), F32),
            pltpu.VMEM((SSM_GROUPS, SUBLANES, GROUP_CONV), F32),
            pltpu.VMEM((SSM_GROUPS, GROUP_X, D_STATE), F32),
            pltpu.VMEM((tq, D_INNER), BF16),
        ],
        compiler_params=_params(2),
        name="mamba_prompt",
    )(x, gmix, *w)


def _mamba_sample_body(x_ref, gmix_ref, wg_ref, wdt_ref, cw_ref, cb_ref, dtb_ref, alog_ref, dsk_ref, ng_ref, wout_ref,
                       cst_ref, sst_ref, gx_ref, wq_ref,
                       y_ref, q_ref, ssm_ref, conv_ref,
                       xn_ref, dt_ref, acs_ref, rowsT_ref, zxg_ref, xpad_ref,
                       ydx_ref, eacs_ref, xddT_ref, bc_ref, z_ref, scale_ref, yg_ref, *, seq, sub_b):
    s = pl.program_id(1)
    nb = Q // seq
    mask = _causal_mask(seq)
    head_masks = _head_lane_masks()

    @pl.when(s == 0)
    def _():
        def seg_last(v):
            v3 = v.reshape(nb, seq, v.shape[-1])
            return jnp.broadcast_to(v3[:, seq - 1:seq, :], v3.shape).reshape(v.shape)

        xn_ref[...] = _rms(x_ref[...], gmix_ref[...]).astype(BF16)
        _dt_stage(xn_ref, wdt_ref, dtb_ref, alog_ref, dt_ref, acs_ref, rowsT_ref, mask, 1, seg_last)
        rowsT_c = rowsT_ref.at[0]
        acsT = rowsT_c[ROW_ACS]
        for bi in range(nb):
            scale_ref[bi] = jnp.broadcast_to(jnp.exp2(acsT[:, (bi + 1) * seq - 1:(bi + 1) * seq]), (LANES, LANES))

        for g in range(SSM_GROUPS):
            zxg_ref[...] = _dot(xn_ref[...], wg_ref[g])
            col_slices = (
                (slice(g * GROUP_X, (g + 1) * GROUP_X), slice(0, GROUP_X)),
                (slice(D_INNER + g * D_STATE, D_INNER + (g + 1) * D_STATE), slice(GROUP_X, GROUP_X + D_STATE)),
                (slice(D_INNER + (SSM_GROUPS + g) * D_STATE, D_INNER + (SSM_GROUPS + g + 1) * D_STATE),
                 slice(GROUP_X + D_STATE, GROUP_CONV)),
            )
            for src, dst in col_slices:
                xpad_ref[:, SUBLANES - (D_CONV - 1):SUBLANES, dst] = cst_ref[:, :, src]
            xpad_ref[:, SUBLANES:SUBLANES + seq, :] = zxg_ref[:, GROUP_X:GROUP_IN].reshape(nb, seq, GROUP_CONV)
            act = _conv_rows(
                lambda k: xpad_ref[:, pl.ds(SUBLANES - (D_CONV - 1) + k, seq), :].reshape(Q, GROUP_CONV),
                cw_ref, cb_ref, g)
            for src, dst in col_slices:
                conv_ref[:, :, src] = xpad_ref[:, pl.ds(SUBLANES + seq - (D_CONV - 1), D_CONV - 1), dst]
            xg = act[:, 0:GROUP_X]
            bg = act[:, GROUP_X:GROUP_X + D_STATE].astype(BF16)
            cg = act[:, GROUP_X + D_STATE:GROUP_CONV].astype(BF16)
            ydxs, eacss, xddTs = _ssd_group(g, [xg], [bg], [cg], [acs_ref[...]], [rowsT_c],
                                            dsk_ref[:, g * GROUP_X:(g + 1) * GROUP_X], mask, head_masks)
            ydx_ref[g] = ydxs[0]
            eacs_ref[g] = eacss[0]
            xddT_ref[g] = xddTs[0]
            bc_ref[g, 0] = bg
            bc_ref[g, 1] = cg
            z_ref[g] = zxg_ref[:, 0:GROUP_X]

    rows_per = sub_b * seq
    r0 = pl.multiple_of(s * rows_per, rows_per)
    rows = pl.ds(r0, rows_per)
    row_b = _div_pow2(lax.broadcasted_iota(jnp.int32, (rows_per, GROUP_X), 0), seq)
    lane_b = _div_pow2(lax.broadcasted_iota(jnp.int32, (GROUP_X, Q), 1), seq)
    for g in range(SSM_GROUPS):
        cg = bc_ref[g, 1, rows, :]
        bg = bc_ref[g, 0]
        xddT = xddT_ref[g]
        yoff = jnp.zeros((rows_per, GROUP_X), F32)
        for bi in range(sub_b):
            hg = sst_ref[bi, g]
            yoff = jnp.where(row_b == bi, _dot_nt(cg, hg.astype(BF16)), yoff)
            sc = scale_ref[s * sub_b + bi]
            scale = _head_rows([sc[HEADS_PER_GROUP * g + r:HEADS_PER_GROUP * g + r + 1, :]
                                for r in range(HEADS_PER_GROUP)])
            xm = jnp.where(lane_b == s * sub_b + bi, xddT, 0.0).astype(BF16)
            ssm_ref[bi, g] = scale * hg + _dot(xm, bg)
        y = ydx_ref[g, rows, :] + eacs_ref[g, rows, :] * yoff
        yg_ref[rows, g * GROUP_X:(g + 1) * GROUP_X] = _gate_norm(
            y, z_ref[g, rows, :], ng_ref[:, g * GROUP_X:(g + 1) * GROUP_X]).astype(BF16)

    @pl.when(s == pl.num_programs(1) - 1)
    def _():
        x1 = x_ref[...] + _dot(yg_ref[...], wout_ref[...])
        y_ref[...] = x1
        q_ref[...] = _dot(_rms(x1, gx_ref[...]).astype(BF16), wq_ref[...])


def _mamba_sample_call(x2d, gmix, w, conv_state, ssm_state, gx, wq, seq, sub_b=4):
    n = x2d.shape[0]
    batch = n // seq
    nb = Q // seq
    n_sub = nb // sub_b
    xspec = pl.BlockSpec((Q, D_MODEL), lambda i, s: (i, 0))
    stspec = pl.BlockSpec((sub_b, SSM_GROUPS, GROUP_X, D_STATE), lambda i, s: (i * n_sub + s, 0, 0, 0))
    cspec = pl.BlockSpec((nb, D_CONV - 1, CONV_DIM), lambda i, s: (i, 0, 0))
    return pl.pallas_call(
        functools.partial(_mamba_sample_body, seq=seq, sub_b=sub_b),
        grid=(n // Q, n_sub),
        in_specs=[xspec, _const_spec((1, D_MODEL))] + _mamba_weight_specs(w)
        + [cspec, stspec, _const_spec((1, D_MODEL)), _const_spec(wq.shape)],
        out_specs=[xspec, xspec, stspec, cspec],
        out_shape=[jax.ShapeDtypeStruct(x2d.shape, F32), jax.ShapeDtypeStruct(x2d.shape, F32),
                   jax.ShapeDtypeStruct((batch, SSM_GROUPS, GROUP_X, D_STATE), F32),
                   jax.ShapeDtypeStruct((batch, D_CONV - 1, CONV_DIM), F32)],
        scratch_shapes=[
            pltpu.VMEM((Q, D_MODEL), BF16),
            pltpu.VMEM((Q, LANES), F32),
            pltpu.VMEM((Q, LANES), F32),
            pltpu.VMEM((1, 3, LANES, Q), F32),
            pltpu.VMEM((Q, GROUP_IN), F32),
            pltpu.VMEM((nb, SUBLANES + seq, GROUP_CONV), F32),
            pltpu.VMEM((SSM_GROUPS, Q, GROUP_X), F32),
            pltpu.VMEM((SSM_GROUPS, Q, GROUP_X), F32),
            pltpu.VMEM((SSM_GROUPS, GROUP_X, Q), F32),
            pltpu.VMEM((SSM_GROUPS, 2, Q, D_STATE), BF16),
            pltpu.VMEM((SSM_GROUPS, Q, GROUP_X), F32),
            pltpu.VMEM((nb, LANES, LANES), F32),
            pltpu.VMEM((Q, D_INNER), BF16),
        ],
        compiler_params=_params(2),
        name="mamba_sample",
    )(x2d, gmix, *w, conv_state, ssm_state, gx, wq)


def _pool_sample_body(x_ref, st_ref, gmix_ref, wpool_ref, pscale_ref, gx_ref, wq_ref,
                      y_ref, q_ref, pool_ref, ext_ref, *, seq, pos0):
    nb = x_ref.shape[0]
    n = nb * seq
    x = x_ref[...].reshape(n, D_MODEL)
    u = _rms(x, gmix_ref[...])
    hist0 = 2 * SUBLANES - POOL_BUF
    ext_ref[:, pl.ds(hist0, POOL_BUF), :] = st_ref[...]
    ext_ref[:, 2 * SUBLANES:2 * SUBLANES + seq, :] = u.reshape(nb, seq, D_MODEL)
    pos = pos0 + lax.broadcasted_iota(jnp.int32, (nb, seq, 1), 1).reshape(n, 1)
    pooled = _pool_windows(
        lambda k, cols: ext_ref[:, pl.ds(2 * SUBLANES - k, seq), cols].reshape(n, POOL_GROUP_DIM), u, pos, n)
    x1 = x + _pool_mix(pooled, wpool_ref, pscale_ref[...])
    pool_ref[...] = ext_ref[:, pl.ds(hist0 + seq, POOL_BUF), :]
    y_ref[...] = x1.reshape(nb, seq, D_MODEL)
    q_ref[...] = _dot(_rms(x1, gx_ref[...]).astype(BF16), wq_ref[...]).reshape(nb, seq, D_MODEL)


def _pool_sample_call(x, pool_state, gmix, wpool, pscale, gx, wq, pos0, nb=16):
    batch, seq, _ = x.shape
    xspec = pl.BlockSpec((nb, seq, D_MODEL), lambda i: (i, 0, 0))
    pspec = pl.BlockSpec((nb, POOL_BUF, D_MODEL), lambda i: (i, 0, 0))
    vec = _const_spec((1, D_MODEL))
    return pl.pallas_call(
        functools.partial(_pool_sample_body, seq=seq, pos0=pos0),
        grid=(batch // nb,),
        in_specs=[xspec, pspec, vec, _const_spec(wpool.shape), vec, vec, _const_spec(wq.shape)],
        out_specs=[xspec, xspec, pspec],
        out_shape=[jax.ShapeDtypeStruct(x.shape, F32), jax.ShapeDtypeStruct(x.shape, F32),
                   jax.ShapeDtypeStruct(pool_state.shape, F32)],
        scratch_shapes=[pltpu.VMEM((nb, 2 * SUBLANES + seq, D_MODEL), F32)],
        compiler_params=_params(1),
        name="pool_sample",
    )(x, pool_state, gmix, wpool, pscale, gx, wq)


HEAD_HALVES = XATTN_HEAD_DIM // LANES
KV_ROWS = N_MEM * XATTN_HEADS * HEAD_HALVES


def _attn_sample_stage(q_ref, k_ref, v_ref, o_ref):
    nb, n_rows, _ = q_ref.shape
    half = n_rows // HEAD_HALVES
    grp = XATTN_HEADS * HEAD_HALVES
    lane = lax.broadcasted_iota(jnp.int32, (half, KV_ROWS), 1)
    rowi = lax.broadcasted_iota(jnp.int32, (half, KV_ROWS), 0)
    valid = (lane & (grp - 1)) == (rowi & (XATTN_HEADS - 1))
    prods = [_dot_nt((q_ref[bi] * (XATTN_HEAD_DIM ** -0.5)).astype(BF16), k_ref[0, bi].astype(BF16))
             for bi in range(nb)]
    ss = [jnp.where(valid, pr[0:half] + pltpu.roll(pr[half:], KV_ROWS - XATTN_HEADS, axis=1), -jnp.inf)
          for pr in prods]
    es = [jnp.exp(s - jnp.max(s, axis=-1, keepdims=True)) for s in ss]
    ps = [e * (1.0 / jnp.sum(e, axis=-1, keepdims=True)) for e in es]
    pes = [jnp.concatenate([p, pltpu.roll(p, XATTN_HEADS, axis=1)], axis=0).astype(BF16) for p in ps]
    for bi in range(nb):
        o_ref[bi] = _dot(pes[bi], v_ref[0, bi].astype(BF16))


def _cache_rows(cache):
    d, b = cache.shape[:2]
    c = cache.reshape(d, b, N_MEM, XATTN_HEADS, HEAD_HALVES, LANES)
    return c.transpose(0, 1, 2, 4, 3, 5).reshape(d, b, KV_ROWS, LANES)


def _attn_sample_rows(q):
    assert HEAD_HALVES == 2
    batch, seq, _ = q.shape
    q5 = q.reshape(batch, seq, XATTN_HEADS, HEAD_HALVES, LANES).transpose(0, 3, 1, 2, 4)
    return q5.reshape(batch, seq * XATTN_HEADS * HEAD_HALVES, LANES)


def _attn_sample_unrows(o2, shape):
    batch, seq, _ = shape
    return o2.reshape(batch, HEAD_HALVES, seq, XATTN_HEADS, LANES).transpose(0, 2, 3, 1, 4).reshape(shape)


def _out_mlp_body(*refs, with_final):
    it = iter(refs)
    x_ref, o_ref, wo_ref, gm_ref, wup_ref, wdn_ref = (next(it) for _ in range(6))
    if with_final:
        gf_ref = next(it)
    y_ref = next(it)
    x = x_ref[...] + _dot(o_ref[...].astype(BF16), wo_ref[...])
    x = _mlp(x, gm_ref[...], wup_ref, wdn_ref)
    if with_final:
        x = _rms(x, gf_ref[...])
    y_ref[...] = x


def _out_mlp_call(x2d, o2d, wo, gm, wup, wdn, gfinal=None, tm=512):
    n = x2d.shape[0]
    tm = min(tm, n)
    row = pl.BlockSpec((tm, D_MODEL), lambda i: (i, 0))
    vec = _const_spec((1, D_MODEL))
    args = [x2d, o2d, wo, gm, wup, wdn]
    specs = [row, row, _const_spec(wo.shape), vec, _const_spec(wup.shape), _const_spec(wdn.shape)]
    if gfinal is not None:
        args.append(gfinal)
        specs.append(vec)
    return pl.pallas_call(
        functools.partial(_out_mlp_body, with_final=gfinal is not None),
        grid=(n // tm,),
        in_specs=specs,
        out_specs=row,
        out_shape=jax.ShapeDtypeStruct(x2d.shape, F32),
        compiler_params=_params(1),
        name="out_mlp",
    )(*args)


def _mamba_weights(w_in, conv_w, conv_b, dt_bias, a_log, d_skip, norm_gated, w_out):
    z, x = w_in[:, :D_INNER], w_in[:, D_INNER:2 * D_INNER]
    bm = w_in[:, 2 * D_INNER:2 * D_INNER + SSM_GROUPS * D_STATE]
    cm = w_in[:, 2 * D_INNER + SSM_GROUPS * D_STATE:D_INNER + CONV_DIM]
    wdt = w_in[:, D_INNER + CONV_DIM:]

    def grp(a, width):
        return a.reshape(a.shape[0], SSM_GROUPS, width)

    wg = jnp.concatenate([grp(z, GROUP_X), grp(x, GROUP_X), grp(bm, D_STATE), grp(cm, D_STATE)], axis=-1)
    wg = wg.transpose(1, 0, 2).astype(BF16)
    pad = LANES - SSM_HEADS
    wdt = jnp.pad(wdt, ((0, 0), (0, pad))).astype(BF16)

    def conv_grp(a):
        return jnp.concatenate([grp(a[:, :D_INNER], GROUP_X), grp(a[:, D_INNER:D_INNER + SSM_GROUPS * D_STATE], D_STATE),
                                grp(a[:, D_INNER + SSM_GROUPS * D_STATE:], D_STATE)], axis=-1).transpose(1, 0, 2)

    cw = conv_grp(conv_w)
    cb = conv_grp(conv_b[None, :])
    dtb = jnp.pad(dt_bias, (0, pad))[None, :]
    alog = jnp.pad(a_log, (0, pad))[None, :]
    dsk = jnp.repeat(d_skip, SSM_HEADDIM)[None, :]
    return (wg, wdt, cw, cb, dtb, alog, dsk, norm_gated[None, :], w_out.astype(BF16))


def kernel(x_prompt, x_sample, cache_mem_k, cache_mem_v, state_ssm, state_conv, state_pool, mem_prompt, norm_mix, norm_xattn, norm_mem, norm_mlp, norm_final, w_in, conv_w, conv_b, dt_bias, a_log, d_skip, norm_gated, w_out, w_pool, pool_scale, w_xq, w_xk, w_xv, w_xo, w_up, w_down):
    bp, seq_p, _ = x_prompt.shape
    bs, seq_s, _ = x_sample.shape
    depth = w_xq.shape[0]
    row = lambda a: a[None, :]

    wq, wk, wv, wo = (a.astype(BF16) for a in (w_xq, w_xk, w_xv, w_xo))
    wup, wdn = w_up.astype(BF16), w_down.astype(BF16)
    wpool = w_pool.astype(BF16)
    mw = _mamba_weights(w_in[0], conv_w[0], conv_b[0], dt_bias[0], a_log[0], d_skip[0], norm_gated[0], w_out[0])

    k_p, v_p, kb, vb = _kv_call(mem_prompt.reshape(bp * N_MEM, D_MODEL), norm_mem[:, None, :], wk, wv)
    kv_shape = (depth, bp, N_MEM, XATTN_HEADS, XATTN_HEAD_DIM)
    kb = kb.reshape(depth, bp, N_MEM, D_MODEL)
    vb = vb.reshape(depth, bp, N_MEM, D_MODEL)

    kc, vc = _cache_rows(cache_mem_k), _cache_rows(cache_mem_v)
    xp, ssm_p, conv_p = _mamba_prompt_call(x_prompt, row(norm_mix[0]), mw)
    xs = x_sample.reshape(bs * seq_s, D_MODEL)
    ssm_in = state_ssm[0].reshape(bs, SSM_GROUPS, GROUP_X, D_STATE)
    xs, q, ssm_s, conv_s = _mamba_sample_call(xs, row(norm_mix[0]), mw, state_conv[0], ssm_in,
                                              row(norm_xattn[0]), wq[0], seq_s)
    xp, o = _prompt_layer_call(xp, 0, kb, vb, row(norm_xattn[0]), wq[0], wo[0], row(norm_mlp[0]), wup[0], wdn[0],
                               sample=(q.reshape(bs, seq_s, D_MODEL), kc, vc))
    xs = _out_mlp_call(xs, o.reshape(bs * seq_s, D_MODEL), wo[0], row(norm_mlp[0]), wup[0], wdn[0])
    xs, q, pool_s = _pool_sample_call(xs.reshape(bs, seq_s, D_MODEL), state_pool[0], row(norm_mix[1]), wpool[0],
                                      row(pool_scale[0]), row(norm_xattn[1]), wq[1], PAST_LEN)
    y_prompt, pool_p, o = _prompt_layer_call(
        xp, 1, kb, vb, row(norm_xattn[1]), wq[1], wo[1], row(norm_mlp[1]), wup[1], wdn[1],
        pool=(row(norm_mix[1]), wpool[0], row(pool_scale[0])), gfinal=row(norm_final), sample=(q, kc, vc))
    y_sample = _out_mlp_call(xs.reshape(bs * seq_s, D_MODEL), o.reshape(bs * seq_s, D_MODEL), wo[1],
                             row(norm_mlp[1]), wup[1], wdn[1], gfinal=row(norm_final))

    ssm_shape = (1, -1, SSM_HEADS, SSM_HEADDIM, D_STATE)
    return (y_prompt, y_sample.reshape(bs, seq_s, D_MODEL),
            k_p.reshape(kv_shape), v_p.reshape(kv_shape),
            ssm_p.reshape(ssm_shape), conv_p[None], pool_p[None],
            ssm_s.reshape(ssm_shape), conv_s[None], pool_s[None])
```

```python
import functools

import jax
import jax.numpy as jnp
from jax import lax
from jax.experimental import pallas as pl
from jax.experimental.pallas import tpu as pltpu

F32 = jnp.float32
BF16 = jnp.bfloat16

D_MODEL = 1024
D_INNER = 2048
SSM_HEADS = 32
SSM_HEADDIM = 64
SSM_GROUPS = 8
HEADS_PER_GROUP = 4
GROUP_X = HEADS_PER_GROUP * SSM_HEADDIM
D_STATE = 128
D_CONV = 4
CONV_DIM = D_INNER + 2 * SSM_GROUPS * D_STATE
GROUP_CONV = GROUP_X + 2 * D_STATE
GROUP_IN = GROUP_X + GROUP_CONV
POOL_WINDOWS = (2, 4, 8, 16)
POOL_GROUP_DIM = 256
POOL_BUF = 15
N_MEM = 256
XATTN_HEADS = 4
XATTN_HEAD_DIM = 256
D_FF = 4096
EPS = 1e-5
PAST_LEN = 16384
LOG2_E = 1.4426950408889634

LANES = 128
SUBLANES = 8
Q = 128
SSD_STAGE = 2
FF_CHUNK = 1024
VMEM_LIMIT = 56 * 1024 * 1024

_NT = (((1,), (1,)), ((), ()))


def _params(n_grid):
    return pltpu.CompilerParams(dimension_semantics=("arbitrary",) * n_grid, vmem_limit_bytes=VMEM_LIMIT)


def _const_spec(shape):
    nd = len(shape)
    return pl.BlockSpec(shape, lambda *_: (0,) * nd, pipeline_mode=pl.Buffered(1))


def _dot(a, b):
    return jnp.dot(a, b, preferred_element_type=F32)


def _dot_nt(a, b):
    return lax.dot_general(a, b, _NT, preferred_element_type=F32)


def _rms(x, g):
    return x * lax.rsqrt(jnp.mean(x * x, axis=-1, keepdims=True) + EPS) * g


def _silu(x):
    h = 0.5 * x
    return h + h * jnp.tanh(h)


def _kv_body(mem_ref, g_ref, wk_ref, wv_ref, k_ref, v_ref, kb_ref, vb_ref):
    mn = _rms(mem_ref[...], g_ref[0]).astype(BF16)
    k = _dot(mn, wk_ref[0].astype(BF16))
    v = _dot(mn, wv_ref[0].astype(BF16))
    for h in range(XATTN_HEADS):
        sl = slice(h * XATTN_HEAD_DIM, (h + 1) * XATTN_HEAD_DIM)
        k_ref[0, :, h, :] = k[:, sl]
        v_ref[0, :, h, :] = v[:, sl]
    kb_ref[0] = k.astype(BF16)
    vb_ref[0] = v.astype(BF16)


def _kv_call(mem, g_mem, wk, wv, tm=512):
    n = mem.shape[0]
    depth = wk.shape[0]
    row = pl.BlockSpec((tm, D_MODEL), lambda l, t: (t, 0))
    wspec = pl.BlockSpec((1, D_MODEL, D_MODEL), lambda l, t: (l, 0, 0))
    ospec = pl.BlockSpec((1, tm, D_MODEL), lambda l, t: (l, t, 0))
    hspec = pl.BlockSpec((1, tm, XATTN_HEADS, XATTN_HEAD_DIM), lambda l, t: (l, t, 0, 0))
    return pl.pallas_call(
        _kv_body,
        grid=(depth, n // tm),
        in_specs=[row, pl.BlockSpec((1, 1, D_MODEL), lambda l, t: (l, 0, 0)), wspec, wspec],
        out_specs=[hspec, hspec, ospec, ospec],
        out_shape=[jax.ShapeDtypeStruct((depth, n, XATTN_HEADS, XATTN_HEAD_DIM), F32)] * 2
        + [jax.ShapeDtypeStruct((depth, n, D_MODEL), BF16)] * 2,
        compiler_params=_params(2),
        name="mem_kv",
    )(mem, g_mem, wk, wv)


def _attn_heads(q, k_of, v_of):
    outs = []
    for h in range(XATTN_HEADS):
        sl = slice(h * XATTN_HEAD_DIM, (h + 1) * XATTN_HEAD_DIM)
        s = _dot_nt(q[:, sl].astype(BF16), k_of(h)) * (XATTN_HEAD_DIM ** -0.5)
        p = jnp.exp(s - jnp.max(s, axis=-1, keepdims=True))
        p = p * (1.0 / jnp.sum(p, axis=-1, keepdims=True))
        outs.append(_dot(p.astype(BF16), v_of(h)))
    return jnp.concatenate(outs, axis=-1)


def _mlp(x, g, wup_ref, wdn_ref):
    hn = _rms(x, g).astype(BF16)
    acc = x
    for c in range(D_FF // FF_CHUNK):
        a = jnp.maximum(_dot(hn, wup_ref[:, c * FF_CHUNK:(c + 1) * FF_CHUNK]), 0.0)
        acc = acc + _dot((a * a).astype(BF16), wdn_ref[c * FF_CHUNK:(c + 1) * FF_CHUNK, :])
    return acc


def _pool_windows(load, u, pos, n):
    outs = []
    for gi, w in enumerate(POOL_WINDOWS):
        cols = slice(gi * POOL_GROUP_DIM, (gi + 1) * POOL_GROUP_DIM)
        acc = load(0, cols)
        for k in range(1, w):
            acc = acc + load(k, cols)
        cnt = jnp.minimum(pos + 1, w).astype(F32)
        outs.append(acc / cnt - u[:, cols])
    return outs


def _pool_windows_rolled(ext_ref, u, pos, n):
    outs = []
    for gi, w in enumerate(POOL_WINDOWS):
        assert w & (w - 1) == 0 and w <= 2 * SUBLANES
        cols = slice(gi * POOL_GROUP_DIM, (gi + 1) * POOL_GROUP_DIM)
        acc = ext_ref[0:n + 2 * SUBLANES, cols]
        span = 1
        while span < w:
            acc = acc + pltpu.roll(acc, span, axis=0)
            span *= 2
        cnt = jnp.minimum(pos + 1, w).astype(F32)
        outs.append(acc[2 * SUBLANES:, :] / cnt - u[:, cols])
    return outs


def _pool_mix(pooled, wpool_ref, scale):
    outs = [_dot(p.astype(BF16), wpool_ref[gi]) for gi, p in enumerate(pooled)]
    return jnp.concatenate(outs, axis=-1) * scale


def _prompt_layer_body(*refs, tq, with_pool, with_final, with_sample):
    it = iter(refs)
    x_ref = next(it)
    if with_pool:
        gmix_ref, wpool_ref, pscale_ref = next(it), next(it), next(it)
    gx_ref, wq_ref, kb_ref, vb_ref, wo_ref = next(it), next(it), next(it), next(it), next(it)
    gm_ref, wup_ref, wdn_ref = next(it), next(it), next(it)
    if with_final:
        gf_ref = next(it)
    if with_sample:
        qs_ref, ks_ref, vs_ref = next(it), next(it), next(it)
    y_ref = next(it)
    if with_pool:
        pool_ref = next(it)
    if with_sample:
        os_ref = next(it)
    if with_pool:
        ext_ref = next(it)

    if with_sample:
        _attn_sample_stage(qs_ref, ks_ref, vs_ref, os_ref)

    x = x_ref[0]
    if with_pool:
        j = pl.program_id(1)

        @pl.when(j == 0)
        def _():
            ext_ref[0:2 * SUBLANES, :] = jnp.zeros((2 * SUBLANES, D_MODEL), F32)

        u = _rms(x, gmix_ref[...])
        ext_ref[pl.ds(2 * SUBLANES, tq), :] = u
        pos = j * tq + lax.broadcasted_iota(jnp.int32, (tq, 1), 0)
        pooled = _pool_windows_rolled(ext_ref, u, pos, tq)
        x = x + _pool_mix(pooled, wpool_ref, pscale_ref[...])
        pool_ref[0] = ext_ref[pl.ds(tq + 1, POOL_BUF), :]
        ext_ref[0:2 * SUBLANES, :] = ext_ref[pl.ds(tq, 2 * SUBLANES), :]

    q = _dot(_rms(x, gx_ref[...]).astype(BF16), wq_ref[...])
    o = _attn_heads(
        q,
        lambda h: kb_ref[0, 0, :, h * XATTN_HEAD_DIM:(h + 1) * XATTN_HEAD_DIM],
        lambda h: vb_ref[0, 0, :, h * XATTN_HEAD_DIM:(h + 1) * XATTN_HEAD_DIM],
    )
    x = x + _dot(o.astype(BF16), wo_ref[...])
    x = _mlp(x, gm_ref[...], wup_ref, wdn_ref)
    if with_final:
        x = _rms(x, gf_ref[...])
    y_ref[0] = x


def _prompt_layer_call(x, layer, kb, vb, gx, wq, wo, gm, wup, wdn, pool=None, gfinal=None, sample=None, tq=512):
    b, seq, _ = x.shape
    with_pool, with_final, with_sample = pool is not None, gfinal is not None, sample is not None
    nj = seq // tq
    xspec = pl.BlockSpec((1, tq, D_MODEL), lambda i, j: (i, j, 0))
    vec = _const_spec((1, D_MODEL))
    kvspec = pl.BlockSpec((1, 1, N_MEM, D_MODEL), lambda i, j: (layer, i, 0, 0))
    args, specs = [x], [xspec]
    if with_pool:
        gmix, wpool, pscale = pool
        args += [gmix, wpool, pscale]
        specs += [vec, _const_spec(wpool.shape), vec]
    args += [gx, wq, kb, vb, wo, gm, wup, wdn]
    specs += [vec, _const_spec(wq.shape), kvspec, kvspec, _const_spec(wo.shape), vec,
              _const_spec(wup.shape), _const_spec(wdn.shape)]
    if with_final:
        args.append(gfinal)
        specs.append(vec)
    out_shape = [jax.ShapeDtypeStruct(x.shape, F32)]
    out_specs = [xspec]
    scratch = []
    if with_pool:
        out_shape.append(jax.ShapeDtypeStruct((b, POOL_BUF, D_MODEL), F32))
        out_specs.append(pl.BlockSpec((1, POOL_BUF, D_MODEL), lambda i, j: (i, 0, 0)))
        scratch.append(pltpu.VMEM((tq + 2 * SUBLANES, D_MODEL), F32))
    if with_sample:
        q, k_rows, v_rows = sample
        q2 = _attn_sample_rows(q)
        nb = q2.shape[0] // (b * nj)
        assert nb * b * nj == q2.shape[0]
        qspec = pl.BlockSpec((nb,) + q2.shape[1:], lambda i, j: (i * nj + j, 0, 0))
        cspec = pl.BlockSpec((1, nb, KV_ROWS, LANES), lambda i, j: (layer, i * nj + j, 0, 0))
        args += [q2, k_rows, v_rows]
        specs += [qspec, cspec, cspec]
        out_shape.append(jax.ShapeDtypeStruct(q2.shape, F32))
        out_specs.append(qspec)
    outs = pl.pallas_call(
        functools.partial(_prompt_layer_body, tq=tq, with_pool=with_pool, with_final=with_final,
                          with_sample=with_sample),
        grid=(b, nj),
        in_specs=specs,
        out_specs=out_specs,
        out_shape=out_shape,
        scratch_shapes=scratch,
        compiler_params=_params(2),
        name="prompt_layer_pool" if with_pool else "prompt_layer",
    )(*args)
    outs = list(outs)
    if with_sample:
        outs[-1] = _attn_sample_unrows(outs[-1], q.shape)
    return outs if len(outs) > 1 else outs[0]


ROW_ACS, ROW_DECAY, ROW_END = 0, 1, 2


def _dt_stage(xn_ref, wdt_ref, dtb_ref, alog_ref, dt_ref, acs_ref, rowsT_ref, mask, n_chunks, seg_last):
    z = _dot(xn_ref[...], wdt_ref[...]) + dtb_ref[...]
    dt_ref[...] = jnp.maximum(z, 0.0) + jnp.log1p(jnp.exp(-jnp.abs(z)))
    a = -jnp.exp(alog_ref[...]) * LOG2_E
    tri = mask.astype(F32)
    for c in range(n_chunks):
        rows = slice(c * Q, (c + 1) * Q)
        dt_c = dt_ref[rows, :]
        acs = jnp.dot(tri, dt_c * a, precision=lax.Precision.HIGHEST, preferred_element_type=F32)
        acs_ref[rows, :] = acs
        acsT, dtT = acs.T, dt_c.T
        rowsT_ref[c, ROW_ACS] = acsT
        rowsT_ref[c, ROW_DECAY] = acsT - jnp.log2(dtT)
        rowsT_ref[c, ROW_END] = dtT * jnp.exp2(seg_last(acs).T - acsT)


def _conv_rows(load_shifted, cw_ref, cb_ref, g):
    acc = cb_ref[g]
    for k in range(D_CONV):
        acc = acc + cw_ref[g, k:k + 1, :] * load_shifted(k)
    return _silu(acc)


def _conv_rolled(ext, cw_ref, cb_ref, g):
    acc = cb_ref[g] + cw_ref[g, D_CONV - 1:D_CONV, :] * ext
    for s in range(1, D_CONV):
        acc = acc + cw_ref[g, D_CONV - 1 - s:D_CONV - s, :] * pltpu.roll(ext, s, axis=0)
    return _silu(acc[SUBLANES:, :])


def _expand_heads(cols):
    half = lax.broadcasted_iota(jnp.int32, (Q, LANES), 1) < SSM_HEADDIM
    return jnp.concatenate([jnp.where(half, cols[0], cols[1]), jnp.where(half, cols[2], cols[3])], axis=1)


def _head_lane_masks():
    lane_head = _div_pow2(lax.broadcasted_iota(jnp.int32, (Q, GROUP_X), 1), SSM_HEADDIM)
    return [jnp.where(lane_head == r, 1.0, 0.0).astype(BF16) for r in range(HEADS_PER_GROUP)]


def _ssd_group(g, xgs, bgs, cgs, acs_cs, rowsT_cs, dskip_g, mask, head_masks):
    heads = [HEADS_PER_GROUP * g + r for r in range(HEADS_PER_GROUP)]
    n = len(xgs)
    acs_cols = [[jnp.broadcast_to(a[:, h:h + 1], (Q, LANES)) for h in heads] for a in acs_cs]
    cbs = [_dot_nt(cg, bg) for cg, bg in zip(cgs, bgs)]
    xbs = [xg.astype(BF16) for xg in xgs]
    lcats = [jnp.concatenate(
        [(cbs[i] * jnp.exp2(jnp.where(mask, acs_cols[i][r] - rowsT_cs[i][ROW_DECAY, h:h + 1, :], -jnp.inf))).astype(BF16)
         for r, h in enumerate(heads)], axis=1) for i in range(n)]
    xblks = [jnp.concatenate([xb * hm for hm in head_masks], axis=0) for xb in xbs]
    ydxs = [_dot(l, xb) + dskip_g * xg for l, xb, xg in zip(lcats, xblks, xgs)]
    xTs = [xg.T for xg in xgs]
    xddTs = [jnp.concatenate(
        [xTs[i][r * SSM_HEADDIM:(r + 1) * SSM_HEADDIM, :] * rowsT_cs[i][ROW_END, h:h + 1, :]
         for r, h in enumerate(heads)], axis=0) for i in range(n)]
    eacss = [jnp.exp2(_expand_heads(cols)) for cols in acs_cols]
    return ydxs, eacss, xddTs


def _gate_norm(y, z, ng):
    yg = y * _silu(z)
    return yg * lax.rsqrt(jnp.mean(yg * yg, axis=-1, keepdims=True) + EPS) * ng


def _div_pow2(v, d):
    assert d & (d - 1) == 0
    return lax.shift_right_logical(v, d.bit_length() - 1)


def _causal_mask(block):
    li = lax.broadcasted_iota(jnp.int32, (Q, Q), 0)
    si = lax.broadcasted_iota(jnp.int32, (Q, Q), 1)
    m = li >= si
    if block is not None:
        m = jnp.logical_and(m, _div_pow2(li, block) == _div_pow2(si, block))
    return m


def _head_rows(vals):
    return jnp.concatenate([jnp.broadcast_to(v, (SSM_HEADDIM, LANES)) for v in vals], axis=0)


def _mamba_prompt_body(x_ref, gmix_ref, wg_ref, wdt_ref, cw_ref, cb_ref, dtb_ref, alog_ref, dsk_ref, ng_ref, wout_ref,
                       *rest, tq, n_cast):
    cast_in, rest = rest[:n_cast], rest[n_cast:]
    y_ref, ssm_ref, conv_ref = rest[:3]
    cast_out, rest = rest[3:3 + n_cast], rest[3 + n_cast:]
    xn_ref, dt_ref, acs_ref, rowsT_ref, zxg_ref, xpad_ref, act_ref, hist_ref, h_ref, yg_ref = rest
    j = pl.program_id(1)
    n_chunks = tq // Q

    for src, dst in zip(cast_in, cast_out):
        dst[...] = src[...].astype(BF16)

    @pl.when(j == 0)
    def _():
        hist_ref[...] = jnp.zeros_like(hist_ref)
        h_ref[...] = jnp.zeros_like(h_ref)

    mask = _causal_mask(None)
    head_masks = _head_lane_masks()
    xn_ref[...] = _rms(x_ref[0], gmix_ref[...]).astype(BF16)
    _dt_stage(xn_ref, wdt_ref, dtb_ref, alog_ref, dt_ref, acs_ref, rowsT_ref, mask, n_chunks,
              lambda v: jnp.broadcast_to(v[Q - 1:Q, :], v.shape))

    def in_proj(g):
        zxg_ref[g % 2] = _dot(xn_ref[...], wg_ref[g])

    in_proj(0)
    for g in range(SSM_GROUPS):
        if g + 1 < SSM_GROUPS:
            in_proj(g + 1)
        zxg, xpad, act = zxg_ref.at[g % 2], xpad_ref.at[g % 2], act_ref.at[g % 2]
        xpad[0:SUBLANES, :] = hist_ref[g]
        xpad[pl.ds(SUBLANES, tq), :] = zxg[:, GROUP_X:GROUP_IN]

        for c in range(n_chunks):
            act[c * Q:(c + 1) * Q, :] = _conv_rolled(xpad[c * Q:(c + 1) * Q + SUBLANES, :], cw_ref, cb_ref, g)
        hist_ref[g] = xpad[pl.ds(tq, SUBLANES), :]

        h_in = h_ref[g]
        for c0 in range(0, n_chunks, SSD_STAGE):
            chunks = list(range(c0, min(c0 + SSD_STAGE, n_chunks)))
            idx = range(len(chunks))
            rows = [slice(c * Q, (c + 1) * Q) for c in chunks]
            bgs = [act[r, GROUP_X:GROUP_X + D_STATE].astype(BF16) for r in rows]
            cgs = [act[r, GROUP_X + D_STATE:GROUP_CONV].astype(BF16) for r in rows]
            rowsT_cs = [rowsT_ref.at[c] for c in chunks]
            ydxs, eacss, xddTs = _ssd_group(
                g, [act[r, 0:GROUP_X] for r in rows], bgs, cgs, [acs_ref[r, :] for r in rows], rowsT_cs,
                dsk_ref[:, g * GROUP_X:(g + 1) * GROUP_X], mask, head_masks)
            updates = [_dot(xddTs[i].astype(BF16), bgs[i]) for i in idx]
            scales = [_head_rows([jnp.exp2(rowsT_cs[i][ROW_ACS, h:h + 1, Q - 1:Q])
                                  for h in range(HEADS_PER_GROUP * g, HEADS_PER_GROUP * (g + 1))]) for i in idx]
            hs = [h_in]
            for i in idx:
                hs.append(scales[i] * hs[i] + updates[i])
            h_in = hs[-1]
            yoffs = [_dot_nt(cgs[i], hs[i].astype(BF16)) for i in idx]
            for i in idx:
                yg_ref[rows[i], g * GROUP_X:(g + 1) * GROUP_X] = _gate_norm(
                    ydxs[i] + eacss[i] * yoffs[i], zxg[rows[i], 0:GROUP_X],
                    ng_ref[:, g * GROUP_X:(g + 1) * GROUP_X]).astype(BF16)
        h_ref[g] = h_in

        hist = hist_ref[g]
        conv_ref[0, :, g * GROUP_X:(g + 1) * GROUP_X] = hist[SUBLANES - (D_CONV - 1):, 0:GROUP_X]
        conv_ref[0, :, D_INNER + g * D_STATE:D_INNER + (g + 1) * D_STATE] = (
            hist[SUBLANES - (D_CONV - 1):, GROUP_X:GROUP_X + D_STATE])
        conv_ref[0, :, D_INNER + (SSM_GROUPS + g) * D_STATE:D_INNER + (SSM_GROUPS + g + 1) * D_STATE] = (
            hist[SUBLANES - (D_CONV - 1):, GROUP_X + D_STATE:GROUP_CONV])

        if g % 2 == 1:
            cols = slice((g - 1) * GROUP_X, (g + 1) * GROUP_X)
            y_ref[0] = (x_ref[0] if g == 1 else y_ref[0]) + _dot(yg_ref[:, cols], wout_ref[cols, :])

    @pl.when(j == pl.num_programs(1) - 1)
    def _():
        ssm_ref[0] = h_ref[...]


def _mamba_weight_specs(w):
    return [_const_spec(a.shape) for a in w]


def _mamba_prompt_call(x, gmix, w, casts=(), tq=512):
    b, seq, _ = x.shape
    nj = seq // tq
    xspec = pl.BlockSpec((1, tq, D_MODEL), lambda i, j: (i, j, 0))
    n_chunks = tq // Q
    cast2d = [a.reshape(-1, a.shape[-1]) for a in casts]
    cast_specs = []
    for a in cast2d:
        rows = a.shape[0] // (b * nj)
        assert rows * b * nj == a.shape[0] and rows % (2 * SUBLANES) == 0
        cast_specs.append(pl.BlockSpec((rows, a.shape[1]), lambda i, j: (i * nj + j, 0)))
    outs = pl.pallas_call(
        functools.partial(_mamba_prompt_body, tq=tq, n_cast=len(casts)),
        grid=(b, nj),
        in_specs=[xspec, _const_spec((1, D_MODEL))] + _mamba_weight_specs(w) + cast_specs,
        out_specs=[xspec,
                   pl.BlockSpec((1, SSM_GROUPS, GROUP_X, D_STATE), lambda i, j: (i, 0, 0, 0)),
                   pl.BlockSpec((1, D_CONV - 1, CONV_DIM), lambda i, j: (i, 0, 0))] + cast_specs,
        out_shape=[jax.ShapeDtypeStruct(x.shape, F32),
                   jax.ShapeDtypeStruct((b, SSM_GROUPS, GROUP_X, D_STATE), F32),
                   jax.ShapeDtypeStruct((b, D_CONV - 1, CONV_DIM), F32)]
        + [jax.ShapeDtypeStruct(a.shape, BF16) for a in cast2d],
        scratch_shapes=[
            pltpu.VMEM((tq, D_MODEL), BF16),
            pltpu.VMEM((tq, LANES), F32),
            pltpu.VMEM((tq, LANES), F32),
            pltpu.VMEM((n_chunks, 3, LANES, Q), F32),
            pltpu.VMEM((2, tq, GROUP_IN), F32),
            pltpu.VMEM((2, tq + SUBLANES, GROUP_CONV), F32),
            pltpu.VMEM((2, tq, GROUP_CONV), F32),
            pltpu.VMEM((SSM_GROUPS, SUBLANES, GROUP_CONV), F32),
            pltpu.VMEM((SSM_GROUPS, GROUP_X, D_STATE), F32),
            pltpu.VMEM((tq, D_INNER), BF16),
        ],
        compiler_params=_params(2),
        name="mamba_prompt",
    )(x, gmix, *w, *cast2d)
    return list(outs[:3]) + [o.reshape(a.shape) for o, a in zip(outs[3:], casts)]


def _mamba_sample_body(x_ref, gmix_ref, wg_ref, wdt_ref, cw_ref, cb_ref, dtb_ref, alog_ref, dsk_ref, ng_ref, wout_ref,
                       cst_ref, sst_ref, gx_ref, wq_ref,
                       y_ref, q_ref, ssm_ref, conv_ref,
                       xn_ref, dt_ref, acs_ref, rowsT_ref, zxg_ref, xpad_ref,
                       ydx_ref, eacs_ref, xddT_ref, bc_ref, z_ref, scale_ref, yg_ref, *, seq, sub_b):
    s = pl.program_id(1)
    nb = Q // seq
    mask = _causal_mask(seq)
    head_masks = _head_lane_masks()

    @pl.when(s == 0)
    def _():
        def seg_last(v):
            v3 = v.reshape(nb, seq, v.shape[-1])
            return jnp.broadcast_to(v3[:, seq - 1:seq, :], v3.shape).reshape(v.shape)

        xn_ref[...] = _rms(x_ref[...], gmix_ref[...]).astype(BF16)
        _dt_stage(xn_ref, wdt_ref, dtb_ref, alog_ref, dt_ref, acs_ref, rowsT_ref, mask, 1, seg_last)
        rowsT_c = rowsT_ref.at[0]
        acsT = rowsT_c[ROW_ACS]
        for bi in range(nb):
            scale_ref[bi] = jnp.broadcast_to(jnp.exp2(acsT[:, (bi + 1) * seq - 1:(bi + 1) * seq]), (LANES, LANES))

        for g in range(SSM_GROUPS):
            zxg_ref[...] = _dot(xn_ref[...], wg_ref[g])
            col_slices = (
                (slice(g * GROUP_X, (g + 1) * GROUP_X), slice(0, GROUP_X)),
                (slice(D_INNER + g * D_STATE, D_INNER + (g + 1) * D_STATE), slice(GROUP_X, GROUP_X + D_STATE)),
                (slice(D_INNER + (SSM_GROUPS + g) * D_STATE, D_INNER + (SSM_GROUPS + g + 1) * D_STATE),
                 slice(GROUP_X + D_STATE, GROUP_CONV)),
            )
            for src, dst in col_slices:
                xpad_ref[:, SUBLANES - (D_CONV - 1):SUBLANES, dst] = cst_ref[:, :, src]
            xpad_ref[:, SUBLANES:SUBLANES + seq, :] = zxg_ref[:, GROUP_X:GROUP_IN].reshape(nb, seq, GROUP_CONV)
            act = _conv_rows(
                lambda k: xpad_ref[:, pl.ds(SUBLANES - (D_CONV - 1) + k, seq), :].reshape(Q, GROUP_CONV),
                cw_ref, cb_ref, g)
            for src, dst in col_slices:
                conv_ref[:, :, src] = xpad_ref[:, pl.ds(SUBLANES + seq - (D_CONV - 1), D_CONV - 1), dst]
            xg = act[:, 0:GROUP_X]
            bg = act[:, GROUP_X:GROUP_X + D_STATE].astype(BF16)
            cg = act[:, GROUP_X + D_STATE:GROUP_CONV].astype(BF16)
            ydxs, eacss, xddTs = _ssd_group(g, [xg], [bg], [cg], [acs_ref[...]], [rowsT_c],
                                            dsk_ref[:, g * GROUP_X:(g + 1) * GROUP_X], mask, head_masks)
            ydx_ref[g] = ydxs[0]
            eacs_ref[g] = eacss[0]
            xddT_ref[g] = xddTs[0]
            bc_ref[g, 0] = bg
            bc_ref[g, 1] = cg
            z_ref[g] = zxg_ref[:, 0:GROUP_X]

    rows_per = sub_b * seq
    r0 = pl.multiple_of(s * rows_per, rows_per)
    rows = pl.ds(r0, rows_per)
    row_b = _div_pow2(lax.broadcasted_iota(jnp.int32, (rows_per, GROUP_X), 0), seq)
    lane_b = _div_pow2(lax.broadcasted_iota(jnp.int32, (GROUP_X, Q), 1), seq)
    for g in range(SSM_GROUPS):
        cg = bc_ref[g, 1, rows, :]
        bg = bc_ref[g, 0]
        xddT = xddT_ref[g]
        yoff = jnp.zeros((rows_per, GROUP_X), F32)
        for bi in range(sub_b):
            hg = sst_ref[bi, g]
            yoff = jnp.where(row_b == bi, _dot_nt(cg, hg.astype(BF16)), yoff)
            sc = scale_ref[s * sub_b + bi]
            scale = _head_rows([sc[HEADS_PER_GROUP * g + r:HEADS_PER_GROUP * g + r + 1, :]
                                for r in range(HEADS_PER_GROUP)])
            xm = jnp.where(lane_b == s * sub_b + bi, xddT, 0.0).astype(BF16)
            ssm_ref[bi, g] = scale * hg + _dot(xm, bg)
        y = ydx_ref[g, rows, :] + eacs_ref[g, rows, :] * yoff
        yg_ref[rows, g * GROUP_X:(g + 1) * GROUP_X] = _gate_norm(
            y, z_ref[g, rows, :], ng_ref[:, g * GROUP_X:(g + 1) * GROUP_X]).astype(BF16)

    @pl.when(s == pl.num_programs(1) - 1)
    def _():
        x1 = x_ref[...] + _dot(yg_ref[...], wout_ref[...])
        y_ref[...] = x1
        q_ref[...] = _dot(_rms(x1, gx_ref[...]).astype(BF16), wq_ref[...])


def _mamba_sample_call(x2d, gmix, w, conv_state, ssm_state, gx, wq, seq, sub_b=4):
    n = x2d.shape[0]
    batch = n // seq
    nb = Q // seq
    n_sub = nb // sub_b
    xspec = pl.BlockSpec((Q, D_MODEL), lambda i, s: (i, 0))
    stspec = pl.BlockSpec((sub_b, SSM_GROUPS, GROUP_X, D_STATE), lambda i, s: (i * n_sub + s, 0, 0, 0))
    cspec = pl.BlockSpec((nb, D_CONV - 1, CONV_DIM), lambda i, s: (i, 0, 0))
    return pl.pallas_call(
        functools.partial(_mamba_sample_body, seq=seq, sub_b=sub_b),
        grid=(n // Q, n_sub),
        in_specs=[xspec, _const_spec((1, D_MODEL))] + _mamba_weight_specs(w)
        + [cspec, stspec, _const_spec((1, D_MODEL)), _const_spec(wq.shape)],
        out_specs=[xspec, xspec, stspec, cspec],
        out_shape=[jax.ShapeDtypeStruct(x2d.shape, F32), jax.ShapeDtypeStruct(x2d.shape, F32),
                   jax.ShapeDtypeStruct((batch, SSM_GROUPS, GROUP_X, D_STATE), F32),
                   jax.ShapeDtypeStruct((batch, D_CONV - 1, CONV_DIM), F32)],
        scratch_shapes=[
            pltpu.VMEM((Q, D_MODEL), BF16),
            pltpu.VMEM((Q, LANES), F32),
            pltpu.VMEM((Q, LANES), F32),
            pltpu.VMEM((1, 3, LANES, Q), F32),
            pltpu.VMEM((Q, GROUP_IN), F32),
            pltpu.VMEM((nb, SUBLANES + seq, GROUP_CONV), F32),
            pltpu.VMEM((SSM_GROUPS, Q, GROUP_X), F32),
            pltpu.VMEM((SSM_GROUPS, Q, GROUP_X), F32),
            pltpu.VMEM((SSM_GROUPS, GROUP_X, Q), F32),
            pltpu.VMEM((SSM_GROUPS, 2, Q, D_STATE), BF16),
            pltpu.VMEM((SSM_GROUPS, Q, GROUP_X), F32),
            pltpu.VMEM((nb, LANES, LANES), F32),
            pltpu.VMEM((Q, D_INNER), BF16),
        ],
        compiler_params=_params(2),
        name="mamba_sample",
    )(x2d, gmix, *w, conv_state, ssm_state, gx, wq)


def _pool_sample_body(x_ref, st_ref, gmix_ref, wpool_ref, pscale_ref, gx_ref, wq_ref,
                      y_ref, q_ref, pool_ref, ext_ref, *, seq, pos0):
    nb = x_ref.shape[0]
    n = nb * seq
    x = x_ref[...].reshape(n, D_MODEL)
    u = _rms(x, gmix_ref[...])
    hist0 = 2 * SUBLANES - POOL_BUF
    ext_ref[:, pl.ds(hist0, POOL_BUF), :] = st_ref[...]
    ext_ref[:, 2 * SUBLANES:2 * SUBLANES + seq, :] = u.reshape(nb, seq, D_MODEL)
    pos = pos0 + lax.broadcasted_iota(jnp.int32, (nb, seq, 1), 1).reshape(n, 1)
    pooled = _pool_windows(
        lambda k, cols: ext_ref[:, pl.ds(2 * SUBLANES - k, seq), cols].reshape(n, POOL_GROUP_DIM), u, pos, n)
    x1 = x + _pool_mix(pooled, wpool_ref, pscale_ref[...])
    pool_ref[...] = ext_ref[:, pl.ds(hist0 + seq, POOL_BUF), :]
    y_ref[...] = x1.reshape(nb, seq, D_MODEL)
    q_ref[...] = _dot(_rms(x1, gx_ref[...]).astype(BF16), wq_ref[...]).reshape(nb, seq, D_MODEL)


def _pool_sample_call(x, pool_state, gmix, wpool, pscale, gx, wq, pos0, nb=16):
    batch, seq, _ = x.shape
    xspec = pl.BlockSpec((nb, seq, D_MODEL), lambda i: (i, 0, 0))
    pspec = pl.BlockSpec((nb, POOL_BUF, D_MODEL), lambda i: (i, 0, 0))
    vec = _const_spec((1, D_MODEL))
    return pl.pallas_call(
        functools.partial(_pool_sample_body, seq=seq, pos0=pos0),
        grid=(batch // nb,),
        in_specs=[xspec, pspec, vec, _const_spec(wpool.shape), vec, vec, _const_spec(wq.shape)],
        out_specs=[xspec, xspec, pspec],
        out_shape=[jax.ShapeDtypeStruct(x.shape, F32), jax.ShapeDtypeStruct(x.shape, F32),
                   jax.ShapeDtypeStruct(pool_state.shape, F32)],
        scratch_shapes=[pltpu.VMEM((nb, 2 * SUBLANES + seq, D_MODEL), F32)],
        compiler_params=_params(1),
        name="pool_sample",
    )(x, pool_state, gmix, wpool, pscale, gx, wq)


HEAD_HALVES = XATTN_HEAD_DIM // LANES
KV_ROWS = N_MEM * XATTN_HEADS * HEAD_HALVES


def _attn_sample_stage(q_ref, k_ref, v_ref, o_ref):
    nb, n_rows, _ = q_ref.shape
    half = n_rows // HEAD_HALVES
    grp = XATTN_HEADS * HEAD_HALVES
    lane = lax.broadcasted_iota(jnp.int32, (half, KV_ROWS), 1)
    rowi = lax.broadcasted_iota(jnp.int32, (half, KV_ROWS), 0)
    valid = (lane & (grp - 1)) == (rowi & (XATTN_HEADS - 1))
    prods = [_dot_nt((q_ref[bi] * (XATTN_HEAD_DIM ** -0.5)).astype(BF16), k_ref[0, bi].astype(BF16))
             for bi in range(nb)]
    ss = [jnp.where(valid, pr[0:half] + pltpu.roll(pr[half:], KV_ROWS - XATTN_HEADS, axis=1), -jnp.inf)
          for pr in prods]
    es = [jnp.exp(s - jnp.max(s, axis=-1, keepdims=True)) for s in ss]
    ps = [e * (1.0 / jnp.sum(e, axis=-1, keepdims=True)) for e in es]
    pes = [jnp.concatenate([p, pltpu.roll(p, XATTN_HEADS, axis=1)], axis=0).astype(BF16) for p in ps]
    for bi in range(nb):
        o_ref[bi] = _dot(pes[bi], v_ref[0, bi].astype(BF16))


def _cache_rows(cache):
    d, b = cache.shape[:2]
    c = cache.reshape(d, b, N_MEM, XATTN_HEADS, HEAD_HALVES, LANES)
    return c.transpose(0, 1, 2, 4, 3, 5).reshape(d, b, KV_ROWS, LANES)


def _attn_sample_rows(q):
    assert HEAD_HALVES == 2
    batch, seq, _ = q.shape
    q5 = q.reshape(batch, seq, XATTN_HEADS, HEAD_HALVES, LANES).transpose(0, 3, 1, 2, 4)
    return q5.reshape(batch, seq * XATTN_HEADS * HEAD_HALVES, LANES)


def _attn_sample_unrows(o2, shape):
    batch, seq, _ = shape
    return o2.reshape(batch, HEAD_HALVES, seq, XATTN_HEADS, LANES).transpose(0, 2, 3, 1, 4).reshape(shape)


def _out_mlp_body(*refs, with_final):
    it = iter(refs)
    x_ref, o_ref, wo_ref, gm_ref, wup_ref, wdn_ref = (next(it) for _ in range(6))
    if with_final:
        gf_ref = next(it)
    y_ref = next(it)
    x = x_ref[...] + _dot(o_ref[...].astype(BF16), wo_ref[...])
    x = _mlp(x, gm_ref[...], wup_ref, wdn_ref)
    if with_final:
        x = _rms(x, gf_ref[...])
    y_ref[...] = x


def _out_mlp_call(x2d, o2d, wo, gm, wup, wdn, gfinal=None, tm=512):
    n = x2d.shape[0]
    tm = min(tm, n)
    row = pl.BlockSpec((tm, D_MODEL), lambda i: (i, 0))
    vec = _const_spec((1, D_MODEL))
    args = [x2d, o2d, wo, gm, wup, wdn]
    specs = [row, row, _const_spec(wo.shape), vec, _const_spec(wup.shape), _const_spec(wdn.shape)]
    if gfinal is not None:
        args.append(gfinal)
        specs.append(vec)
    return pl.pallas_call(
        functools.partial(_out_mlp_body, with_final=gfinal is not None),
        grid=(n // tm,),
        in_specs=specs,
        out_specs=row,
        out_shape=jax.ShapeDtypeStruct(x2d.shape, F32),
        compiler_params=_params(1),
        name="out_mlp",
    )(*args)


def _mamba_weights(w_in, conv_w, conv_b, dt_bias, a_log, d_skip, norm_gated, w_out):
    z, x = w_in[:, :D_INNER], w_in[:, D_INNER:2 * D_INNER]
    bm = w_in[:, 2 * D_INNER:2 * D_INNER + SSM_GROUPS * D_STATE]
    cm = w_in[:, 2 * D_INNER + SSM_GROUPS * D_STATE:D_INNER + CONV_DIM]
    wdt = w_in[:, D_INNER + CONV_DIM:]

    def grp(a, width):
        return a.reshape(a.shape[0], SSM_GROUPS, width)

    wg = jnp.concatenate([grp(z, GROUP_X), grp(x, GROUP_X), grp(bm, D_STATE), grp(cm, D_STATE)], axis=-1)
    wg = wg.transpose(1, 0, 2).astype(BF16)
    pad = LANES - SSM_HEADS
    wdt = jnp.pad(wdt, ((0, 0), (0, pad))).astype(BF16)

    def conv_grp(a):
        return jnp.concatenate([grp(a[:, :D_INNER], GROUP_X), grp(a[:, D_INNER:D_INNER + SSM_GROUPS * D_STATE], D_STATE),
                                grp(a[:, D_INNER + SSM_GROUPS * D_STATE:], D_STATE)], axis=-1).transpose(1, 0, 2)

    cw = conv_grp(conv_w)
    cb = conv_grp(conv_b[None, :])
    dtb = jnp.pad(dt_bias, (0, pad))[None, :]
    alog = jnp.pad(a_log, (0, pad))[None, :]
    dsk = jnp.repeat(d_skip, SSM_HEADDIM)[None, :]
    return (wg, wdt, cw, cb, dtb, alog, dsk, norm_gated[None, :], w_out.astype(BF16))


def kernel(x_prompt, x_sample, cache_mem_k, cache_mem_v, state_ssm, state_conv, state_pool, mem_prompt, norm_mix, norm_xattn, norm_mem, norm_mlp, norm_final, w_in, conv_w, conv_b, dt_bias, a_log, d_skip, norm_gated, w_out, w_pool, pool_scale, w_xq, w_xk, w_xv, w_xo, w_up, w_down):
    bp, seq_p, _ = x_prompt.shape
    bs, seq_s, _ = x_sample.shape
    depth = w_xq.shape[0]
    row = lambda a: a[None, :]

    wpool = w_pool.astype(BF16)
    mw = _mamba_weights(w_in[0], conv_w[0], conv_b[0], dt_bias[0], a_log[0], d_skip[0], norm_gated[0], w_out[0])

    k_p, v_p, kb, vb = _kv_call(mem_prompt.reshape(bp * N_MEM, D_MODEL), norm_mem[:, None, :], w_xk, w_xv)
    kv_shape = (depth, bp, N_MEM, XATTN_HEADS, XATTN_HEAD_DIM)
    kb = kb.reshape(depth, bp, N_MEM, D_MODEL)
    vb = vb.reshape(depth, bp, N_MEM, D_MODEL)

    kc, vc = _cache_rows(cache_mem_k), _cache_rows(cache_mem_v)
    xp, ssm_p, conv_p, wq, wo, wup, wdn = _mamba_prompt_call(
        x_prompt, row(norm_mix[0]), mw, casts=(w_xq, w_xo, w_up, w_down))
    xs = x_sample.reshape(bs * seq_s, D_MODEL)
    ssm_in = state_ssm[0].reshape(bs, SSM_GROUPS, GROUP_X, D_STATE)
    xs, q, ssm_s, conv_s = _mamba_sample_call(xs, row(norm_mix[0]), mw, state_conv[0], ssm_in,
                                              row(norm_xattn[0]), wq[0], seq_s)
    xp, o = _prompt_layer_call(xp, 0, kb, vb, row(norm_xattn[0]), wq[0], wo[0], row(norm_mlp[0]), wup[0], wdn[0],
                               sample=(q.reshape(bs, seq_s, D_MODEL), kc, vc))
    xs = _out_mlp_call(xs, o.reshape(bs * seq_s, D_MODEL), wo[0], row(norm_mlp[0]), wup[0], wdn[0])
    xs, q, pool_s = _pool_sample_call(xs.reshape(bs, seq_s, D_MODEL), state_pool[0], row(norm_mix[1]), wpool[0],
                                      row(pool_scale[0]), row(norm_xattn[1]), wq[1], PAST_LEN)
    y_prompt, pool_p, o = _prompt_layer_call(
        xp, 1, kb, vb, row(norm_xattn[1]), wq[1], wo[1], row(norm_mlp[1]), wup[1], wdn[1],
        pool=(row(norm_mix[1]), wpool[0], row(pool_scale[0])), gfinal=row(norm_final), sample=(q, kc, vc))
    y_sample = _out_mlp_call(xs.reshape(bs * seq_s, D_MODEL), o.reshape(bs * seq_s, D_MODEL), wo[1],
                             row(norm_mlp[1]), wup[1], wdn[1], gfinal=row(norm_final))

    ssm_shape = (1, -1, SSM_HEADS, SSM_HEADDIM, D_STATE)
    return (y_prompt, y_sample.reshape(bs, seq_s, D_MODEL),
            k_p.reshape(kv_shape), v_p.reshape(kv_shape),
            ssm_p.reshape(ssm_shape), conv_p[None], pool_p[None],
            ssm_s.reshape(ssm_shape), conv_s[None], pool_s[None])
```

```python
import functools

import jax
import jax.numpy as jnp
from jax import lax
from jax.experimental import pallas as pl
from jax.experimental.pallas import tpu as pltpu

F32 = jnp.float32
BF16 = jnp.bfloat16

D_MODEL = 1024
D_INNER = 2048
SSM_HEADS = 32
SSM_HEADDIM = 64
SSM_GROUPS = 8
HEADS_PER_GROUP = 4
GROUP_X = HEADS_PER_GROUP * SSM_HEADDIM
D_STATE = 128
D_CONV = 4
CONV_DIM = D_INNER + 2 * SSM_GROUPS * D_STATE
GROUP_CONV = GROUP_X + 2 * D_STATE
GROUP_IN = GROUP_X + GROUP_CONV
POOL_WINDOWS = (2, 4, 8, 16)
POOL_GROUP_DIM = 256
POOL_BUF = 15
N_MEM = 256
XATTN_HEADS = 4
XATTN_HEAD_DIM = 256
D_FF = 4096
EPS = 1e-5
PAST_LEN = 16384
LOG2_E = 1.4426950408889634

LANES = 128
SUBLANES = 8
Q = 128
SSD_STAGE = 2
FF_CHUNK = 1024
VMEM_LIMIT = 56 * 1024 * 1024

_NT = (((1,), (1,)), ((), ()))


def _params(n_grid):
    return pltpu.CompilerParams(dimension_semantics=("arbitrary",) * n_grid, vmem_limit_bytes=VMEM_LIMIT)


def _const_spec(shape):
    nd = len(shape)
    return pl.BlockSpec(shape, lambda *_: (0,) * nd, pipeline_mode=pl.Buffered(1))


def _layer_spec(a, layer):
    nd = a.ndim
    return pl.BlockSpec((None,) + a.shape[1:], lambda *_: (layer,) + (0,) * (nd - 1), pipeline_mode=pl.Buffered(1))


def _dot(a, b):
    return jnp.dot(a, b, preferred_element_type=F32)


def _dot_nt(a, b):
    return lax.dot_general(a, b, _NT, preferred_element_type=F32)


def _rms(x, g):
    return x * lax.rsqrt(jnp.mean(x * x, axis=-1, keepdims=True) + EPS) * g


def _silu(x):
    h = 0.5 * x
    return h + h * jnp.tanh(h)


def _kv_body(mem_ref, g_ref, wk_ref, wv_ref, wz_ref, wx_ref, wb_ref, wc_ref, wo_ref,
             k_ref, v_ref, kb_ref, vb_ref, wg_ref, wob_ref):
    wg_ref[0, :, 0:GROUP_X] = wz_ref[...].astype(BF16)
    wg_ref[0, :, GROUP_X:2 * GROUP_X] = wx_ref[...].astype(BF16)
    wg_ref[0, :, 2 * GROUP_X:2 * GROUP_X + D_STATE] = wb_ref[...].astype(BF16)
    wg_ref[0, :, 2 * GROUP_X + D_STATE:GROUP_IN] = wc_ref[...].astype(BF16)
    wob_ref[...] = wo_ref[...].astype(BF16)
    mn = _rms(mem_ref[...], g_ref[0]).astype(BF16)
    k = _dot(mn, wk_ref[0].astype(BF16))
    v = _dot(mn, wv_ref[0].astype(BF16))
    for h in range(XATTN_HEADS):
        sl = slice(h * XATTN_HEAD_DIM, (h + 1) * XATTN_HEAD_DIM)
        k_ref[0, :, h, :] = k[:, sl]
        v_ref[0, :, h, :] = v[:, sl]
    kb_ref[0] = k.astype(BF16)
    vb_ref[0] = v.astype(BF16)


def _kv_call(mem, g_mem, wk, wv, w_in, w_out):
    n = mem.shape[0]
    depth = wk.shape[0]
    nt = SSM_GROUPS // depth
    tm = n // nt
    assert depth * nt == SSM_GROUPS and tm * nt == n and tm % SUBLANES == 0
    row = pl.BlockSpec((tm, D_MODEL), lambda l, t: (t, 0))
    wspec = pl.BlockSpec((1, D_MODEL, D_MODEL), lambda l, t: (l, 0, 0))
    ospec = pl.BlockSpec((1, tm, D_MODEL), lambda l, t: (l, t, 0))
    hspec = pl.BlockSpec((1, tm, XATTN_HEADS, XATTN_HEAD_DIM), lambda l, t: (l, t, 0, 0))

    def cols(width, first_block):
        return pl.BlockSpec((D_MODEL, width), lambda l, t: (0, first_block + l * nt + t))

    x0 = D_INNER // GROUP_X
    b0 = 2 * D_INNER // D_STATE
    wo_spec = pl.BlockSpec((w_out.shape[0] // SSM_GROUPS, D_MODEL), lambda l, t: (l * nt + t, 0))
    return pl.pallas_call(
        _kv_body,
        grid=(depth, nt),
        in_specs=[row, pl.BlockSpec((1, 1, D_MODEL), lambda l, t: (l, 0, 0)), wspec, wspec,
                  cols(GROUP_X, 0), cols(GROUP_X, x0), cols(D_STATE, b0), cols(D_STATE, b0 + SSM_GROUPS), wo_spec],
        out_specs=[hspec, hspec, ospec, ospec,
                   pl.BlockSpec((1, D_MODEL, GROUP_IN), lambda l, t: (l * nt + t, 0, 0)), wo_spec],
        out_shape=[jax.ShapeDtypeStruct((depth, n, XATTN_HEADS, XATTN_HEAD_DIM), F32)] * 2
        + [jax.ShapeDtypeStruct((depth, n, D_MODEL), BF16)] * 2
        + [jax.ShapeDtypeStruct((SSM_GROUPS, D_MODEL, GROUP_IN), BF16), jax.ShapeDtypeStruct(w_out.shape, BF16)],
        compiler_params=_params(2),
        name="mem_kv",
    )(mem, g_mem, wk, wv, w_in, w_in, w_in, w_in, w_out)


def _attn_heads(q, k_of, v_of):
    outs = []
    for h in range(XATTN_HEADS):
        sl = slice(h * XATTN_HEAD_DIM, (h + 1) * XATTN_HEAD_DIM)
        s = _dot_nt(q[:, sl].astype(BF16), k_of(h)) * (XATTN_HEAD_DIM ** -0.5)
        p = jnp.exp(s - jnp.max(s, axis=-1, keepdims=True))
        p = p * (1.0 / jnp.sum(p, axis=-1, keepdims=True))
        outs.append(_dot(p.astype(BF16), v_of(h)))
    return jnp.concatenate(outs, axis=-1)


def _mlp(x, g, wup_ref, wdn_ref):
    hn = _rms(x, g).astype(BF16)
    acc = x
    for c in range(D_FF // FF_CHUNK):
        a = jnp.maximum(_dot(hn, wup_ref[:, c * FF_CHUNK:(c + 1) * FF_CHUNK]), 0.0)
        acc = acc + _dot((a * a).astype(BF16), wdn_ref[c * FF_CHUNK:(c + 1) * FF_CHUNK, :])
    return acc


def _pool_windows(load, u, pos, n):
    outs = []
    for gi, w in enumerate(POOL_WINDOWS):
        cols = slice(gi * POOL_GROUP_DIM, (gi + 1) * POOL_GROUP_DIM)
        acc = load(0, cols)
        for k in range(1, w):
            acc = acc + load(k, cols)
        cnt = jnp.minimum(pos + 1, w).astype(F32)
        outs.append(acc / cnt - u[:, cols])
    return outs


def _pool_windows_rolled(ext_ref, u, pos, n):
    outs = []
    for gi, w in enumerate(POOL_WINDOWS):
        assert w & (w - 1) == 0 and w <= 2 * SUBLANES
        cols = slice(gi * POOL_GROUP_DIM, (gi + 1) * POOL_GROUP_DIM)
        acc = ext_ref[0:n + 2 * SUBLANES, cols]
        span = 1
        while span < w:
            acc = acc + pltpu.roll(acc, span, axis=0)
            span *= 2
        cnt = jnp.minimum(pos + 1, w).astype(F32)
        outs.append(acc[2 * SUBLANES:, :] / cnt - u[:, cols])
    return outs


def _pool_mix(pooled, wpool_ref, scale):
    outs = [_dot(p.astype(BF16), wpool_ref[gi]) for gi, p in enumerate(pooled)]
    return jnp.concatenate(outs, axis=-1) * scale


def _prompt_layer_body(*refs, tq, with_pool, with_final, with_sample):
    it = iter(refs)
    x_ref = next(it)
    if with_pool:
        gmix_ref, wpool_ref, pscale_ref = next(it), next(it), next(it)
    gx_ref, wq_ref, kb_ref, vb_ref, wo_ref = next(it), next(it), next(it), next(it), next(it)
    gm_ref, wup_ref, wdn_ref = next(it), next(it), next(it)
    if with_final:
        gf_ref = next(it)
    if with_sample:
        qs_ref, ks_ref, vs_ref = next(it), next(it), next(it)
    y_ref = next(it)
    if with_pool:
        pool_ref = next(it)
    if with_sample:
        os_ref = next(it)
    if with_pool:
        ext_ref = next(it)

    x = x_ref[0]
    if with_pool:
        j = pl.program_id(1)

        @pl.when(j == 0)
        def _():
            ext_ref[0:2 * SUBLANES, :] = jnp.zeros((2 * SUBLANES, D_MODEL), F32)

        u = _rms(x, gmix_ref[...])
        ext_ref[pl.ds(2 * SUBLANES, tq), :] = u
        pos = j * tq + lax.broadcasted_iota(jnp.int32, (tq, 1), 0)
        pooled = _pool_windows_rolled(ext_ref, u, pos, tq)
        x = x + _pool_mix(pooled, wpool_ref, pscale_ref[...])
        pool_ref[0] = ext_ref[pl.ds(tq + 1, POOL_BUF), :]
        ext_ref[0:2 * SUBLANES, :] = ext_ref[pl.ds(tq, 2 * SUBLANES), :]

    if with_sample:
        _attn_sample_stage(qs_ref, ks_ref, vs_ref, os_ref)

    q = _dot(_rms(x, gx_ref[...]).astype(BF16), wq_ref[...])
    o = _attn_heads(
        q,
        lambda h: kb_ref[0, 0, :, h * XATTN_HEAD_DIM:(h + 1) * XATTN_HEAD_DIM],
        lambda h: vb_ref[0, 0, :, h * XATTN_HEAD_DIM:(h + 1) * XATTN_HEAD_DIM],
    )
    x = x + _dot(o.astype(BF16), wo_ref[...])
    x = _mlp(x, gm_ref[...], wup_ref, wdn_ref)
    if with_final:
        x = _rms(x, gf_ref[...])
    y_ref[0] = x


def _prompt_layer_call(x, layer, kb, vb, gx, wq, wo, gm, wup, wdn, pool=None, gfinal=None, sample=None, tq=512):
    b, seq, _ = x.shape
    with_pool, with_final, with_sample = pool is not None, gfinal is not None, sample is not None
    nj = seq // tq
    xspec = pl.BlockSpec((1, tq, D_MODEL), lambda i, j: (i, j, 0))
    kvspec = pl.BlockSpec((1, 1, N_MEM, D_MODEL), lambda i, j: (layer, i, 0, 0))
    args, specs = [x], [xspec]
    if with_pool:
        gmix, wpool, pscale = pool
        args += [gmix, wpool, pscale]
        specs += [_layer_spec(gmix, layer), _layer_spec(wpool, layer // 2), _layer_spec(pscale, layer // 2)]
    args += [gx, wq, kb, vb, wo, gm, wup, wdn]
    specs += [_layer_spec(gx, layer), _layer_spec(wq, layer), kvspec, kvspec, _layer_spec(wo, layer),
              _layer_spec(gm, layer), _layer_spec(wup, layer), _layer_spec(wdn, layer)]
    if with_final:
        args.append(gfinal)
        specs.append(_const_spec(gfinal.shape))
    out_shape = [jax.ShapeDtypeStruct(x.shape, F32)]
    out_specs = [xspec]
    scratch = []
    if with_pool:
        out_shape.append(jax.ShapeDtypeStruct((b, POOL_BUF, D_MODEL), F32))
        out_specs.append(pl.BlockSpec((1, POOL_BUF, D_MODEL), lambda i, j: (i, 0, 0)))
        scratch.append(pltpu.VMEM((tq + 2 * SUBLANES, D_MODEL), F32))
    if with_sample:
        q_pieces, k_rows, v_rows = sample
        n_batch = k_rows.shape[1]
        nb = n_batch // (b * nj)
        rows = q_pieces.shape[2] // (b * nj)
        assert nb * b * nj == n_batch and rows == nb * SUBLANES
        qspec = pl.BlockSpec(q_pieces.shape[:2] + (rows, LANES), lambda i, j: (0, 0, i * nj + j, 0))
        cspec = pl.BlockSpec((1, nb, KV_ROWS, LANES), lambda i, j: (layer, i * nj + j, 0, 0))
        args += [q_pieces, k_rows, v_rows]
        specs += [qspec, cspec, cspec]
        out_shape.append(jax.ShapeDtypeStruct(q_pieces.shape, F32))
        out_specs.append(qspec)
    outs = pl.pallas_call(
        functools.partial(_prompt_layer_body, tq=tq, with_pool=with_pool, with_final=with_final,
                          with_sample=with_sample),
        grid=(b, nj),
        in_specs=specs,
        out_specs=out_specs,
        out_shape=out_shape,
        scratch_shapes=scratch,
        compiler_params=_params(2),
        name="prompt_layer_pool" if with_pool else "prompt_layer",
    )(*args)
    return outs if len(outs) > 1 else outs[0]


ROW_ACS, ROW_DECAY, ROW_END = 0, 1, 2


def _dt_stage(xn_ref, wdt_ref, dtb_ref, alog_ref, dt_ref, acs_ref, rowsT_ref, mask, n_chunks, seg_last):
    z = _dot(xn_ref[...], wdt_ref[...]) + dtb_ref[...]
    dt_ref[...] = jnp.maximum(z, 0.0) + jnp.log1p(jnp.exp(-jnp.abs(z)))
    a = -jnp.exp(alog_ref[...]) * LOG2_E
    tri = mask.astype(F32)
    for c in range(n_chunks):
        rows = slice(c * Q, (c + 1) * Q)
        dt_c = dt_ref[rows, :]
        acs = jnp.dot(tri, dt_c * a, precision=lax.Precision.HIGHEST, preferred_element_type=F32)
        acs_ref[rows, :] = acs
        acsT, dtT = acs.T, dt_c.T
        rowsT_ref[c, ROW_ACS] = acsT
        rowsT_ref[c, ROW_DECAY] = acsT - jnp.log2(dtT)
        rowsT_ref[c, ROW_END] = dtT * jnp.exp2(seg_last(acs).T - acsT)


def _conv_rows(load_shifted, cw_ref, cb_ref, g):
    acc = cb_ref[g]
    for k in range(D_CONV):
        acc = acc + cw_ref[g, k:k + 1, :] * load_shifted(k)
    return _silu(acc)


def _conv_rolled(ext, cw_ref, cb_ref, g):
    acc = cb_ref[g] + cw_ref[g, D_CONV - 1:D_CONV, :] * ext
    for s in range(1, D_CONV):
        acc = acc + cw_ref[g, D_CONV - 1 - s:D_CONV - s, :] * pltpu.roll(ext, s, axis=0)
    return _silu(acc[SUBLANES:, :])


def _expand_heads(cols):
    half = lax.broadcasted_iota(jnp.int32, (Q, LANES), 1) < SSM_HEADDIM
    return jnp.concatenate([jnp.where(half, cols[0], cols[1]), jnp.where(half, cols[2], cols[3])], axis=1)


def _head_lane_masks():
    lane_head = _div_pow2(lax.broadcasted_iota(jnp.int32, (Q, GROUP_X), 1), SSM_HEADDIM)
    return [jnp.where(lane_head == r, 1.0, 0.0).astype(BF16) for r in range(HEADS_PER_GROUP)]


def _ssd_group(g, xgs, bgs, cgs, acs_cs, rowsT_cs, dskip_g, mask, head_masks):
    heads = [HEADS_PER_GROUP * g + r for r in range(HEADS_PER_GROUP)]
    n = len(xgs)
    acs_cols = [[jnp.broadcast_to(a[:, h:h + 1], (Q, LANES)) for h in heads] for a in acs_cs]
    cbs = [_dot_nt(cg, bg) for cg, bg in zip(cgs, bgs)]
    xbs = [xg.astype(BF16) for xg in xgs]
    lcats = [jnp.concatenate(
        [(cbs[i] * jnp.exp2(jnp.where(mask, acs_cols[i][r] - rowsT_cs[i][ROW_DECAY, h:h + 1, :], -jnp.inf))).astype(BF16)
         for r, h in enumerate(heads)], axis=1) for i in range(n)]
    xblks = [jnp.concatenate([xb * hm for hm in head_masks], axis=0) for xb in xbs]
    ydxs = [_dot(l, xb) + dskip_g * xg for l, xb, xg in zip(lcats, xblks, xgs)]
    xTs = [xg.T for xg in xgs]
    xddTs = [jnp.concatenate(
        [xTs[i][r * SSM_HEADDIM:(r + 1) * SSM_HEADDIM, :] * rowsT_cs[i][ROW_END, h:h + 1, :]
         for r, h in enumerate(heads)], axis=0) for i in range(n)]
    eacss = [jnp.exp2(_expand_heads(cols)) for cols in acs_cols]
    return ydxs, eacss, xddTs


def _gate_norm(y, z, ng):
    yg = y * _silu(z)
    return yg * lax.rsqrt(jnp.mean(yg * yg, axis=-1, keepdims=True) + EPS) * ng


def _div_pow2(v, d):
    assert d & (d - 1) == 0
    return lax.shift_right_logical(v, d.bit_length() - 1)


def _causal_mask(block):
    li = lax.broadcasted_iota(jnp.int32, (Q, Q), 0)
    si = lax.broadcasted_iota(jnp.int32, (Q, Q), 1)
    m = li >= si
    if block is not None:
        m = jnp.logical_and(m, _div_pow2(li, block) == _div_pow2(si, block))
    return m


def _head_rows(vals):
    return jnp.concatenate([jnp.broadcast_to(v, (SSM_HEADDIM, LANES)) for v in vals], axis=0)


def _mamba_prompt_body(x_ref, gmix_ref, wg_ref, wdt_ref, cw_ref, cb_ref, dtb_ref, alog_ref, dsk_ref, ng_ref, wout_ref,
                       *rest, tq, n_cast):
    cast_in, rest = rest[:n_cast], rest[n_cast:]
    y_ref, ssm_ref, conv_ref = rest[:3]
    cast_out, rest = rest[3:3 + n_cast], rest[3 + n_cast:]
    xn_ref, dt_ref, acs_ref, rowsT_ref, zxg_ref, xpad_ref, act_ref, hist_ref, h_ref, yg_ref = rest
    j = pl.program_id(1)
    n_chunks = tq // Q

    for src, dst in zip(cast_in, cast_out):
        dst[...] = src[...].astype(BF16)

    @pl.when(j == 0)
    def _():
        hist_ref[...] = jnp.zeros_like(hist_ref)
        h_ref[...] = jnp.zeros_like(h_ref)

    mask = _causal_mask(None)
    head_masks = _head_lane_masks()
    xn_ref[...] = _rms(x_ref[0], gmix_ref[...]).astype(BF16)
    _dt_stage(xn_ref, wdt_ref, dtb_ref, alog_ref, dt_ref, acs_ref, rowsT_ref, mask, n_chunks,
              lambda v: jnp.broadcast_to(v[Q - 1:Q, :], v.shape))

    def in_proj(g):
        zxg_ref[g % 2] = _dot(xn_ref[...], wg_ref[g])

    in_proj(0)
    for g in range(SSM_GROUPS):
        if g + 1 < SSM_GROUPS:
            in_proj(g + 1)
        zxg, xpad, act = zxg_ref.at[g % 2], xpad_ref.at[g % 2], act_ref.at[g % 2]
        xpad[0:SUBLANES, :] = hist_ref[g]
        xpad[pl.ds(SUBLANES, tq), :] = zxg[:, GROUP_X:GROUP_IN]

        for c in range(n_chunks):
            act[c * Q:(c + 1) * Q, :] = _conv_rolled(xpad[c * Q:(c + 1) * Q + SUBLANES, :], cw_ref, cb_ref, g)
        hist_ref[g] = xpad[pl.ds(tq, SUBLANES), :]

        h_in = h_ref[g]
        for c0 in range(0, n_chunks, SSD_STAGE):
            chunks = list(range(c0, min(c0 + SSD_STAGE, n_chunks)))
            idx = range(len(chunks))
            rows = [slice(c * Q, (c + 1) * Q) for c in chunks]
            bgs = [act[r, GROUP_X:GROUP_X + D_STATE].astype(BF16) for r in rows]
            cgs = [act[r, GROUP_X + D_STATE:GROUP_CONV].astype(BF16) for r in rows]
            rowsT_cs = [rowsT_ref.at[c] for c in chunks]
            ydxs, eacss, xddTs = _ssd_group(
                g, [act[r, 0:GROUP_X] for r in rows], bgs, cgs, [acs_ref[r, :] for r in rows], rowsT_cs,
                dsk_ref[:, g * GROUP_X:(g + 1) * GROUP_X], mask, head_masks)
            updates = [_dot(xddTs[i].astype(BF16), bgs[i]) for i in idx]
            scales = [_head_rows([jnp.exp2(rowsT_cs[i][ROW_ACS, h:h + 1, Q - 1:Q])
                                  for h in range(HEADS_PER_GROUP * g, HEADS_PER_GROUP * (g + 1))]) for i in idx]
            hs = [h_in]
            for i in idx:
                hs.append(scales[i] * hs[i] + updates[i])
            h_in = hs[-1]
            yoffs = [_dot_nt(cgs[i], hs[i].astype(BF16)) for i in idx]
            for i in idx:
                yg_ref[rows[i], g * GROUP_X:(g + 1) * GROUP_X] = _gate_norm(
                    ydxs[i] + eacss[i] * yoffs[i], zxg[rows[i], 0:GROUP_X],
                    ng_ref[:, g * GROUP_X:(g + 1) * GROUP_X]).astype(BF16)
        h_ref[g] = h_in

        hist = hist_ref[g]
        conv_ref[0, :, g * GROUP_X:(g + 1) * GROUP_X] = hist[SUBLANES - (D_CONV - 1):, 0:GROUP_X]
        conv_ref[0, :, D_INNER + g * D_STATE:D_INNER + (g + 1) * D_STATE] = (
            hist[SUBLANES - (D_CONV - 1):, GROUP_X:GROUP_X + D_STATE])
        conv_ref[0, :, D_INNER + (SSM_GROUPS + g) * D_STATE:D_INNER + (SSM_GROUPS + g + 1) * D_STATE] = (
            hist[SUBLANES - (D_CONV - 1):, GROUP_X + D_STATE:GROUP_CONV])

        if g % 2 == 1:
            cols = slice((g - 1) * GROUP_X, (g + 1) * GROUP_X)
            y_ref[0] = (x_ref[0] if g == 1 else y_ref[0]) + _dot(yg_ref[:, cols], wout_ref[cols, :])

    @pl.when(j == pl.num_programs(1) - 1)
    def _():
        ssm_ref[0] = h_ref[...]


def _mamba_weight_specs(w):
    return [_const_spec(a.shape) for a in w]


def _mamba_prompt_call(x, gmix, layer, w, casts=(), tq=512):
    b, seq, _ = x.shape
    nj = seq // tq
    xspec = pl.BlockSpec((1, tq, D_MODEL), lambda i, j: (i, j, 0))
    n_chunks = tq // Q
    cast2d = [a.reshape(-1, a.shape[-1]) for a in casts]
    cast_specs = []
    for a in cast2d:
        rows = a.shape[0] // (b * nj)
        assert rows * b * nj == a.shape[0] and rows % (2 * SUBLANES) == 0
        cast_specs.append(pl.BlockSpec((rows, a.shape[1]), lambda i, j: (i * nj + j, 0)))
    outs = pl.pallas_call(
        functools.partial(_mamba_prompt_body, tq=tq, n_cast=len(casts)),
        grid=(b, nj),
        in_specs=[xspec, _layer_spec(gmix, layer)] + _mamba_weight_specs(w) + cast_specs,
        out_specs=[xspec,
                   pl.BlockSpec((1, SSM_GROUPS, GROUP_X, D_STATE), lambda i, j: (i, 0, 0, 0)),
                   pl.BlockSpec((1, D_CONV - 1, CONV_DIM), lambda i, j: (i, 0, 0))] + cast_specs,
        out_shape=[jax.ShapeDtypeStruct(x.shape, F32),
                   jax.ShapeDtypeStruct((b, SSM_GROUPS, GROUP_X, D_STATE), F32),
                   jax.ShapeDtypeStruct((b, D_CONV - 1, CONV_DIM), F32)]
        + [jax.ShapeDtypeStruct(a.shape, BF16) for a in cast2d],
        scratch_shapes=[
            pltpu.VMEM((tq, D_MODEL), BF16),
            pltpu.VMEM((tq, LANES), F32),
            pltpu.VMEM((tq, LANES), F32),
            pltpu.VMEM((n_chunks, 3, LANES, Q), F32),
            pltpu.VMEM((2, tq, GROUP_IN), F32),
            pltpu.VMEM((2, tq + SUBLANES, GROUP_CONV), F32),
            pltpu.VMEM((2, tq, GROUP_CONV), F32),
            pltpu.VMEM((SSM_GROUPS, SUBLANES, GROUP_CONV), F32),
            pltpu.VMEM((SSM_GROUPS, GROUP_X, D_STATE), F32),
            pltpu.VMEM((tq, D_INNER), BF16),
        ],
        compiler_params=_params(2),
        name="mamba_prompt",
    )(x, gmix, *w, *cast2d)
    return list(outs[:3]) + [o.reshape(a.shape) for o, a in zip(outs[3:], casts)]


def _mamba_sample_body(x_ref, gmix_ref, wg_ref, wdt_ref, cw_ref, cb_ref, dtb_ref, alog_ref, dsk_ref, ng_ref, wout_ref,
                       cst_ref, sst_ref, gx_ref, wq_ref,
                       y_ref, q_ref, ssm_ref, conv_ref,
                       xn_ref, dt_ref, acs_ref, rowsT_ref, zxg_ref, xpad_ref,
                       ydx_ref, eacs_ref, xddT_ref, bc_ref, z_ref, scale_ref, yg_ref, *, seq, sub_b):
    s = pl.program_id(1)
    nb = Q // seq
    mask = _causal_mask(seq)
    head_masks = _head_lane_masks()

    @pl.when(s == 0)
    def _():
        def seg_last(v):
            v3 = v.reshape(nb, seq, v.shape[-1])
            return jnp.broadcast_to(v3[:, seq - 1:seq, :], v3.shape).reshape(v.shape)

        xn_ref[...] = _rms(x_ref[...], gmix_ref[...]).astype(BF16)
        _dt_stage(xn_ref, wdt_ref, dtb_ref, alog_ref, dt_ref, acs_ref, rowsT_ref, mask, 1, seg_last)
        rowsT_c = rowsT_ref.at[0]
        acsT = rowsT_c[ROW_ACS]
        for bi in range(nb):
            scale_ref[bi] = jnp.broadcast_to(jnp.exp2(acsT[:, (bi + 1) * seq - 1:(bi + 1) * seq]), (LANES, LANES))

        for g in range(SSM_GROUPS):
            zxg_ref[...] = _dot(xn_ref[...], wg_ref[g])
            col_slices = (
                (slice(g * GROUP_X, (g + 1) * GROUP_X), slice(0, GROUP_X)),
                (slice(D_INNER + g * D_STATE, D_INNER + (g + 1) * D_STATE), slice(GROUP_X, GROUP_X + D_STATE)),
                (slice(D_INNER + (SSM_GROUPS + g) * D_STATE, D_INNER + (SSM_GROUPS + g + 1) * D_STATE),
                 slice(GROUP_X + D_STATE, GROUP_CONV)),
            )
            for src, dst in col_slices:
                xpad_ref[:, SUBLANES - (D_CONV - 1):SUBLANES, dst] = cst_ref[:, :, src]
            xpad_ref[:, SUBLANES:SUBLANES + seq, :] = zxg_ref[:, GROUP_X:GROUP_IN].reshape(nb, seq, GROUP_CONV)
            act = _conv_rows(
                lambda k: xpad_ref[:, pl.ds(SUBLANES - (D_CONV - 1) + k, seq), :].reshape(Q, GROUP_CONV),
                cw_ref, cb_ref, g)
            for src, dst in col_slices:
                conv_ref[:, :, src] = xpad_ref[:, pl.ds(SUBLANES + seq - (D_CONV - 1), D_CONV - 1), dst]
            xg = act[:, 0:GROUP_X]
            bg = act[:, GROUP_X:GROUP_X + D_STATE].astype(BF16)
            cg = act[:, GROUP_X + D_STATE:GROUP_CONV].astype(BF16)
            ydxs, eacss, xddTs = _ssd_group(g, [xg], [bg], [cg], [acs_ref[...]], [rowsT_c],
                                            dsk_ref[:, g * GROUP_X:(g + 1) * GROUP_X], mask, head_masks)
            ydx_ref[g] = ydxs[0]
            eacs_ref[g] = eacss[0]
            xddT_ref[g] = xddTs[0]
            bc_ref[g, 0] = bg
            bc_ref[g, 1] = cg
            z_ref[g] = zxg_ref[:, 0:GROUP_X]

    rows_per = sub_b * seq
    r0 = pl.multiple_of(s * rows_per, rows_per)
    rows = pl.ds(r0, rows_per)
    row_b = _div_pow2(lax.broadcasted_iota(jnp.int32, (rows_per, GROUP_X), 0), seq)
    lane_b = _div_pow2(lax.broadcasted_iota(jnp.int32, (GROUP_X, Q), 1), seq)
    for g in range(SSM_GROUPS):
        cg = bc_ref[g, 1, rows, :]
        bg = bc_ref[g, 0]
        xddT = xddT_ref[g]
        yoff = jnp.zeros((rows_per, GROUP_X), F32)
        for bi in range(sub_b):
            hg = sst_ref[bi, g]
            yoff = jnp.where(row_b == bi, _dot_nt(cg, hg.astype(BF16)), yoff)
            sc = scale_ref[s * sub_b + bi]
            scale = _head_rows([sc[HEADS_PER_GROUP * g + r:HEADS_PER_GROUP * g + r + 1, :]
                                for r in range(HEADS_PER_GROUP)])
            xm = jnp.where(lane_b == s * sub_b + bi, xddT, 0.0).astype(BF16)
            ssm_ref[bi, g] = scale * hg + _dot(xm, bg)
        y = ydx_ref[g, rows, :] + eacs_ref[g, rows, :] * yoff
        yg_ref[rows, g * GROUP_X:(g + 1) * GROUP_X] = _gate_norm(
            y, z_ref[g, rows, :], ng_ref[:, g * GROUP_X:(g + 1) * GROUP_X]).astype(BF16)

    @pl.when(s == pl.num_programs(1) - 1)
    def _():
        x1 = x_ref[...] + _dot(yg_ref[...], wout_ref[...])
        y_ref[...] = x1
        _store_pieces(q_ref, _dot(_rms(x1, gx_ref[...]).astype(BF16), wq_ref[...]))


def _mamba_sample_call(x2d, gmix, w, conv_state, ssm_state, gx, wq, layer, seq, sub_b=4):
    n = x2d.shape[0]
    batch = n // seq
    nb = Q // seq
    n_sub = nb // sub_b
    xspec = pl.BlockSpec((Q, D_MODEL), lambda i, s: (i, 0))
    stspec = pl.BlockSpec((sub_b, SSM_GROUPS, GROUP_X, D_STATE), lambda i, s: (i * n_sub + s, 0, 0, 0))
    cspec = pl.BlockSpec((nb, D_CONV - 1, CONV_DIM), lambda i, s: (i, 0, 0))
    return pl.pallas_call(
        functools.partial(_mamba_sample_body, seq=seq, sub_b=sub_b),
        grid=(n // Q, n_sub),
        in_specs=[xspec, _layer_spec(gmix, layer)] + _mamba_weight_specs(w)
        + [cspec, stspec, _layer_spec(gx, layer), _layer_spec(wq, layer)],
        out_specs=[xspec, _pieces_spec(Q, lambda i, s: (0, 0, i, 0)), stspec, cspec],
        out_shape=[jax.ShapeDtypeStruct(x2d.shape, F32), _pieces_shape(n),
                   jax.ShapeDtypeStruct((batch, SSM_GROUPS, GROUP_X, D_STATE), F32),
                   jax.ShapeDtypeStruct((batch, D_CONV - 1, CONV_DIM), F32)],
        scratch_shapes=[
            pltpu.VMEM((Q, D_MODEL), BF16),
            pltpu.VMEM((Q, LANES), F32),
            pltpu.VMEM((Q, LANES), F32),
            pltpu.VMEM((1, 3, LANES, Q), F32),
            pltpu.VMEM((Q, GROUP_IN), F32),
            pltpu.VMEM((nb, SUBLANES + seq, GROUP_CONV), F32),
            pltpu.VMEM((SSM_GROUPS, Q, GROUP_X), F32),
            pltpu.VMEM((SSM_GROUPS, Q, GROUP_X), F32),
            pltpu.VMEM((SSM_GROUPS, GROUP_X, Q), F32),
            pltpu.VMEM((SSM_GROUPS, 2, Q, D_STATE), BF16),
            pltpu.VMEM((SSM_GROUPS, Q, GROUP_X), F32),
            pltpu.VMEM((nb, LANES, LANES), F32),
            pltpu.VMEM((Q, D_INNER), BF16),
        ],
        compiler_params=_params(2),
        name="mamba_sample",
    )(x2d, gmix, *w, conv_state, ssm_state, gx, wq)


def _pool_sample_body(x_ref, st_ref, gmix_ref, wpool_ref, pscale_ref, gx_ref, wq_ref,
                      y_ref, q_ref, pool_ref, ext_ref, *, seq, pos0):
    nb = x_ref.shape[0]
    n = nb * seq
    x = x_ref[...].reshape(n, D_MODEL)
    u = _rms(x, gmix_ref[...])
    hist0 = 2 * SUBLANES - POOL_BUF
    ext_ref[:, pl.ds(hist0, POOL_BUF), :] = st_ref[...]
    ext_ref[:, 2 * SUBLANES:2 * SUBLANES + seq, :] = u.reshape(nb, seq, D_MODEL)
    pos = pos0 + lax.broadcasted_iota(jnp.int32, (nb, seq, 1), 1).reshape(n, 1)
    pooled = _pool_windows(
        lambda k, cols: ext_ref[:, pl.ds(2 * SUBLANES - k, seq), cols].reshape(n, POOL_GROUP_DIM), u, pos, n)
    x1 = x + _pool_mix(pooled, wpool_ref, pscale_ref[...])
    pool_ref[...] = ext_ref[:, pl.ds(hist0 + seq, POOL_BUF), :]
    y_ref[...] = x1.reshape(nb, seq, D_MODEL)
    _store_pieces(q_ref, _dot(_rms(x1, gx_ref[...]).astype(BF16), wq_ref[...]))


def _pool_sample_call(x, pool_state, gmix, wpool, pscale, gx, wq, layer, pos0, nb=16):
    batch, seq, _ = x.shape
    xspec = pl.BlockSpec((nb, seq, D_MODEL), lambda i: (i, 0, 0))
    pspec = pl.BlockSpec((nb, POOL_BUF, D_MODEL), lambda i: (i, 0, 0))
    return pl.pallas_call(
        functools.partial(_pool_sample_body, seq=seq, pos0=pos0),
        grid=(batch // nb,),
        in_specs=[xspec, pspec, _layer_spec(gmix, layer), _layer_spec(wpool, layer // 2),
                  _layer_spec(pscale, layer // 2), _layer_spec(gx, layer), _layer_spec(wq, layer)],
        out_specs=[xspec, _pieces_spec(nb * seq, lambda i: (0, 0, i, 0)), pspec],
        out_shape=[jax.ShapeDtypeStruct(x.shape, F32), _pieces_shape(batch * seq),
                   jax.ShapeDtypeStruct(pool_state.shape, F32)],
        scratch_shapes=[pltpu.VMEM((nb, 2 * SUBLANES + seq, D_MODEL), F32)],
        compiler_params=_params(1),
        name="pool_sample",
    )(x, pool_state, gmix, wpool, pscale, gx, wq)


HEAD_HALVES = XATTN_HEAD_DIM // LANES
KV_ROWS = N_MEM * XATTN_HEADS * HEAD_HALVES


def _attn_sample_stage(q_ref, k_ref, v_ref, o_ref):
    nb = k_ref.shape[1]
    seq = q_ref.shape[2] // nb
    assert seq == SUBLANES
    pieces = [(j, h) for j in range(HEAD_HALVES) for h in range(XATTN_HEADS)]
    half = XATTN_HEADS * seq
    grp = XATTN_HEADS * HEAD_HALVES
    lane = lax.broadcasted_iota(jnp.int32, (half, KV_ROWS), 1)
    rowi = lax.broadcasted_iota(jnp.int32, (half, KV_ROWS), 0)
    valid = (lane & (grp - 1)) == _div_pow2(rowi, seq)
    qs = [jnp.concatenate([q_ref[j, h, bi * seq:(bi + 1) * seq, :] for j, h in pieces], axis=0) for bi in range(nb)]
    prods = [_dot_nt((qs[bi] * (XATTN_HEAD_DIM ** -0.5)).astype(BF16), k_ref[0, bi].astype(BF16))
             for bi in range(nb)]
    ss = [jnp.where(valid, pr[0:half] + pltpu.roll(pr[half:], KV_ROWS - XATTN_HEADS, axis=1), -jnp.inf)
          for pr in prods]
    es = [jnp.exp(s - jnp.max(s, axis=-1, keepdims=True)) for s in ss]
    ps = [e * (1.0 / jnp.sum(e, axis=-1, keepdims=True)) for e in es]
    pes = [jnp.concatenate([p, pltpu.roll(p, XATTN_HEADS, axis=1)], axis=0).astype(BF16) for p in ps]
    for bi in range(nb):
        o = _dot(pes[bi], v_ref[0, bi].astype(BF16))
        for n, (j, h) in enumerate(pieces):
            o_ref[j, h, bi * seq:(bi + 1) * seq, :] = o[n * seq:(n + 1) * seq, :]


def _cache_rows(cache):
    d, b = cache.shape[:2]
    c = cache.reshape(d, b, N_MEM, XATTN_HEADS, HEAD_HALVES, LANES)
    return c.transpose(0, 1, 2, 4, 3, 5).reshape(d, b, KV_ROWS, LANES)


def _store_pieces(ref, v):
    for h in range(XATTN_HEADS):
        for j in range(HEAD_HALVES):
            c0 = (h * HEAD_HALVES + j) * LANES
            ref[j, h] = v[:, c0:c0 + LANES]


def _load_pieces(ref):
    return jnp.concatenate([ref[j, h] for h in range(XATTN_HEADS) for j in range(HEAD_HALVES)], axis=-1)


def _pieces_spec(rows, index_map):
    return pl.BlockSpec((HEAD_HALVES, XATTN_HEADS, rows, LANES), index_map)


def _pieces_shape(n_rows):
    return jax.ShapeDtypeStruct((HEAD_HALVES, XATTN_HEADS, n_rows, LANES), F32)


def _out_mlp_body(*refs, with_final):
    it = iter(refs)
    x_ref, o_ref, wo_ref, gm_ref, wup_ref, wdn_ref = (next(it) for _ in range(6))
    if with_final:
        gf_ref = next(it)
    y_ref = next(it)
    x = x_ref[...] + _dot(_load_pieces(o_ref).astype(BF16), wo_ref[...])
    x = _mlp(x, gm_ref[...], wup_ref, wdn_ref)
    if with_final:
        x = _rms(x, gf_ref[...])
    y_ref[...] = x


def _out_mlp_call(x2d, o_pieces, layer, wo, gm, wup, wdn, gfinal=None, tm=512):
    n = x2d.shape[0]
    tm = min(tm, n)
    row = pl.BlockSpec((tm, D_MODEL), lambda i: (i, 0))
    args = [x2d, o_pieces, wo, gm, wup, wdn]
    specs = [row, _pieces_spec(tm, lambda i: (0, 0, i, 0)), _layer_spec(wo, layer), _layer_spec(gm, layer),
             _layer_spec(wup, layer), _layer_spec(wdn, layer)]
    if gfinal is not None:
        args.append(gfinal)
        specs.append(_const_spec(gfinal.shape))
    return pl.pallas_call(
        functools.partial(_out_mlp_body, with_final=gfinal is not None),
        grid=(n // tm,),
        in_specs=specs,
        out_specs=row,
        out_shape=jax.ShapeDtypeStruct(x2d.shape, F32),
        compiler_params=_params(1),
        name="out_mlp",
    )(*args)


def _mamba_weights(wg, w_in, conv_w, conv_b, dt_bias, a_log, d_skip, norm_gated, w_out_bf16):
    pad = LANES - SSM_HEADS
    wdt = jnp.pad(w_in[:, D_INNER + CONV_DIM:], ((0, 0), (0, pad))).astype(BF16)

    def grp(a, width):
        return a.reshape(a.shape[0], SSM_GROUPS, width)

    def conv_grp(a):
        return jnp.concatenate([grp(a[:, :D_INNER], GROUP_X), grp(a[:, D_INNER:D_INNER + SSM_GROUPS * D_STATE], D_STATE),
                                grp(a[:, D_INNER + SSM_GROUPS * D_STATE:], D_STATE)], axis=-1).transpose(1, 0, 2)

    cw = conv_grp(conv_w)
    cb = conv_grp(conv_b[None, :])
    dtb = jnp.pad(dt_bias, (0, pad))[None, :]
    alog = jnp.pad(a_log, (0, pad))[None, :]
    dsk = jnp.repeat(d_skip, SSM_HEADDIM)[None, :]
    return (wg, wdt, cw, cb, dtb, alog, dsk, norm_gated[None, :], w_out_bf16)


def kernel(x_prompt, x_sample, cache_mem_k, cache_mem_v, state_ssm, state_conv, state_pool, mem_prompt, norm_mix, norm_xattn, norm_mem, norm_mlp, norm_final, w_in, conv_w, conv_b, dt_bias, a_log, d_skip, norm_gated, w_out, w_pool, pool_scale, w_xq, w_xk, w_xv, w_xo, w_up, w_down):
    bp, seq_p, _ = x_prompt.shape
    bs, seq_s, _ = x_sample.shape
    depth = w_xq.shape[0]
    per_layer = lambda a: a[:, None, :]

    wpool = w_pool.astype(BF16)
    g_mix, g_x, g_mlp, p_scale = (per_layer(a) for a in (norm_mix, norm_xattn, norm_mlp, pool_scale))
    g_final = norm_final[None, :]

    k_p, v_p, kb, vb, wg, wout = _kv_call(mem_prompt.reshape(bp * N_MEM, D_MODEL), per_layer(norm_mem), w_xk, w_xv,
                                          w_in[0], w_out[0])
    mw = _mamba_weights(wg, w_in[0], conv_w[0], conv_b[0], dt_bias[0], a_log[0], d_skip[0], norm_gated[0], wout)
    kv_shape = (depth, bp, N_MEM, XATTN_HEADS, XATTN_HEAD_DIM)
    kb = kb.reshape(depth, bp, N_MEM, D_MODEL)
    vb = vb.reshape(depth, bp, N_MEM, D_MODEL)

    kc, vc = _cache_rows(cache_mem_k), _cache_rows(cache_mem_v)
    xp, ssm_p, conv_p, wq, wo, wup, wdn = _mamba_prompt_call(
        x_prompt, g_mix, 0, mw, casts=(w_xq, w_xo, w_up, w_down))
    xs = x_sample.reshape(bs * seq_s, D_MODEL)
    ssm_in = state_ssm[0].reshape(bs, SSM_GROUPS, GROUP_X, D_STATE)
    xs, q, ssm_s, conv_s = _mamba_sample_call(xs, g_mix, mw, state_conv[0], ssm_in, g_x, wq, 0, seq_s)
    xp, o = _prompt_layer_call(xp, 0, kb, vb, g_x, wq, wo, g_mlp, wup, wdn, sample=(q, kc, vc))
    xs = _out_mlp_call(xs, o, 0, wo, g_mlp, wup, wdn)
    xs, q, pool_s = _pool_sample_call(xs.reshape(bs, seq_s, D_MODEL), state_pool[0], g_mix, wpool, p_scale,
                                      g_x, wq, 1, PAST_LEN)
    y_prompt, pool_p, o = _prompt_layer_call(
        xp, 1, kb, vb, g_x, wq, wo, g_mlp, wup, wdn,
        pool=(g_mix, wpool, p_scale), gfinal=g_final, sample=(q, kc, vc))
    y_sample = _out_mlp_call(xs.reshape(bs * seq_s, D_MODEL), o, 1, wo, g_mlp, wup, wdn, gfinal=g_final)

    ssm_shape = (1, -1, SSM_HEADS, SSM_HEADDIM, D_STATE)
    return (y_prompt, y_sample.reshape(bs, seq_s, D_MODEL),
            k_p.reshape(kv_shape), v_p.reshape(kv_shape),
            ssm_p.reshape(ssm_shape), conv_p[None], pool_p[None],
            ssm_s.reshape(ssm_shape), conv_s[None], pool_s[None])
```

```python
import functools

import jax
import jax.numpy as jnp
from jax import lax
from jax.experimental import pallas as pl
from jax.experimental.pallas import tpu as pltpu

F32 = jnp.float32
BF16 = jnp.bfloat16

D_MODEL = 1024
D_INNER = 2048
SSM_HEADS = 32
SSM_HEADDIM = 64
SSM_GROUPS = 8
HEADS_PER_GROUP = 4
GROUP_X = HEADS_PER_GROUP * SSM_HEADDIM
D_STATE = 128
D_CONV = 4
CONV_DIM = D_INNER + 2 * SSM_GROUPS * D_STATE
GROUP_CONV = GROUP_X + 2 * D_STATE
GROUP_IN = GROUP_X + GROUP_CONV
POOL_WINDOWS = (2, 4, 8, 16)
POOL_GROUP_DIM = 256
POOL_BUF = 15
N_MEM = 256
XATTN_HEADS = 4
XATTN_HEAD_DIM = 256
D_FF = 4096
EPS = 1e-5
PAST_LEN = 16384
LOG2_E = 1.4426950408889634

LANES = 128
SUBLANES = 8
Q = 128
SSD_STAGE = 2
FF_CHUNK = 1024
VMEM_LIMIT = 56 * 1024 * 1024

_NT = (((1,), (1,)), ((), ()))


def _params(n_grid):
    return pltpu.CompilerParams(dimension_semantics=("arbitrary",) * n_grid, vmem_limit_bytes=VMEM_LIMIT)


def _const_spec(shape):
    nd = len(shape)
    return pl.BlockSpec(shape, lambda *_: (0,) * nd, pipeline_mode=pl.Buffered(1))


def _layer_spec(a, layer):
    nd = a.ndim
    return pl.BlockSpec((None,) + a.shape[1:], lambda *_: (layer,) + (0,) * (nd - 1), pipeline_mode=pl.Buffered(1))


def _dot(a, b):
    return jnp.dot(a, b, preferred_element_type=F32)


def _dot_nt(a, b):
    return lax.dot_general(a, b, _NT, preferred_element_type=F32)


def _rms(x, g):
    return x * lax.rsqrt(jnp.mean(x * x, axis=-1, keepdims=True) + EPS) * g


def _silu(x):
    h = 0.5 * x
    return h + h * jnp.tanh(h)


def _kv_body(mem_ref, g_ref, wk_ref, wv_ref, wz_ref, wx_ref, wb_ref, wc_ref, wo_ref,
             k_ref, v_ref, kb_ref, vb_ref, wg_ref, wob_ref):
    wg_ref[0, :, 0:GROUP_X] = wz_ref[...].T.astype(BF16)
    wg_ref[0, :, GROUP_X:2 * GROUP_X] = wx_ref[...].T.astype(BF16)
    wg_ref[0, :, 2 * GROUP_X:2 * GROUP_X + D_STATE] = wb_ref[...].T.astype(BF16)
    wg_ref[0, :, 2 * GROUP_X + D_STATE:GROUP_IN] = wc_ref[...].T.astype(BF16)
    wob_ref[...] = wo_ref[...].astype(BF16)
    mn = _rms(mem_ref[...], g_ref[0]).astype(BF16)
    k = _dot(mn, wk_ref[0].astype(BF16))
    v = _dot(mn, wv_ref[0].astype(BF16))
    for h in range(XATTN_HEADS):
        sl = slice(h * XATTN_HEAD_DIM, (h + 1) * XATTN_HEAD_DIM)
        k_ref[0, :, h, :] = k[:, sl]
        v_ref[0, :, h, :] = v[:, sl]
    kb_ref[0] = k.astype(BF16)
    vb_ref[0] = v.astype(BF16)


def _kv_call(mem, g_mem, wk, wv, w_in_t, w_out):
    n = mem.shape[0]
    depth = wk.shape[0]
    nt = SSM_GROUPS // depth
    tm = n // nt
    assert depth * nt == SSM_GROUPS and tm * nt == n and tm % SUBLANES == 0
    row = pl.BlockSpec((tm, D_MODEL), lambda l, t: (t, 0))
    wspec = pl.BlockSpec((1, D_MODEL, D_MODEL), lambda l, t: (l, 0, 0))
    ospec = pl.BlockSpec((1, tm, D_MODEL), lambda l, t: (l, t, 0))
    hspec = pl.BlockSpec((1, tm, XATTN_HEADS, XATTN_HEAD_DIM), lambda l, t: (l, t, 0, 0))

    def rows(height, first_block):
        return pl.BlockSpec((height, D_MODEL), lambda l, t: (first_block + l * nt + t, 0))

    x0 = D_INNER // GROUP_X
    b0 = 2 * D_INNER // D_STATE
    wo_spec = pl.BlockSpec((w_out.shape[0] // SSM_GROUPS, D_MODEL), lambda l, t: (l * nt + t, 0))
    return pl.pallas_call(
        _kv_body,
        grid=(depth, nt),
        in_specs=[row, pl.BlockSpec((1, 1, D_MODEL), lambda l, t: (l, 0, 0)), wspec, wspec,
                  rows(GROUP_X, 0), rows(GROUP_X, x0), rows(D_STATE, b0), rows(D_STATE, b0 + SSM_GROUPS), wo_spec],
        out_specs=[hspec, hspec, ospec, ospec,
                   pl.BlockSpec((1, D_MODEL, GROUP_IN), lambda l, t: (l * nt + t, 0, 0)), wo_spec],
        out_shape=[jax.ShapeDtypeStruct((depth, n, XATTN_HEADS, XATTN_HEAD_DIM), F32)] * 2
        + [jax.ShapeDtypeStruct((depth, n, D_MODEL), BF16)] * 2
        + [jax.ShapeDtypeStruct((SSM_GROUPS, D_MODEL, GROUP_IN), BF16), jax.ShapeDtypeStruct(w_out.shape, BF16)],
        compiler_params=_params(2),
        name="mem_kv",
    )(mem, g_mem, wk, wv, w_in_t, w_in_t, w_in_t, w_in_t, w_out)


def _attn_heads(q, k_of, v_of):
    outs = []
    for h in range(XATTN_HEADS):
        sl = slice(h * XATTN_HEAD_DIM, (h + 1) * XATTN_HEAD_DIM)
        s = _dot_nt(q[:, sl].astype(BF16), k_of(h)) * (XATTN_HEAD_DIM ** -0.5)
        p = jnp.exp(s - jnp.max(s, axis=-1, keepdims=True))
        p = p * (1.0 / jnp.sum(p, axis=-1, keepdims=True))
        outs.append(_dot(p.astype(BF16), v_of(h)))
    return jnp.concatenate(outs, axis=-1)


def _mlp(x, g, wup_ref, wdn_ref):
    hn = _rms(x, g).astype(BF16)
    acc = x
    for c in range(D_FF // FF_CHUNK):
        a = jnp.maximum(_dot(hn, wup_ref[:, c * FF_CHUNK:(c + 1) * FF_CHUNK]), 0.0)
        acc = acc + _dot((a * a).astype(BF16), wdn_ref[c * FF_CHUNK:(c + 1) * FF_CHUNK, :])
    return acc


def _pool_windows(load, u, pos, n):
    outs = []
    for gi, w in enumerate(POOL_WINDOWS):
        cols = slice(gi * POOL_GROUP_DIM, (gi + 1) * POOL_GROUP_DIM)
        acc = load(0, cols)
        for k in range(1, w):
            acc = acc + load(k, cols)
        cnt = jnp.minimum(pos + 1, w).astype(F32)
        outs.append(acc / cnt - u[:, cols])
    return outs


def _pool_windows_rolled(ext_ref, u, pos, n):
    outs = []
    for gi, w in enumerate(POOL_WINDOWS):
        assert w & (w - 1) == 0 and w <= 2 * SUBLANES
        cols = slice(gi * POOL_GROUP_DIM, (gi + 1) * POOL_GROUP_DIM)
        acc = ext_ref[0:n + 2 * SUBLANES, cols]
        span = 1
        while span < w:
            acc = acc + pltpu.roll(acc, span, axis=0)
            span *= 2
        cnt = jnp.minimum(pos + 1, w).astype(F32)
        outs.append(acc[2 * SUBLANES:, :] / cnt - u[:, cols])
    return outs


def _pool_mix(pooled, wpool_ref, scale):
    outs = [_dot(p.astype(BF16), wpool_ref[gi]) for gi, p in enumerate(pooled)]
    return jnp.concatenate(outs, axis=-1) * scale


def _prompt_layer_body(*refs, tq, with_pool, with_final, with_sample):
    it = iter(refs)
    x_ref = next(it)
    if with_pool:
        gmix_ref, wpool_ref, pscale_ref = next(it), next(it), next(it)
    gx_ref, wq_ref, kb_ref, vb_ref, wo_ref = next(it), next(it), next(it), next(it), next(it)
    gm_ref, wup_ref, wdn_ref = next(it), next(it), next(it)
    if with_final:
        gf_ref = next(it)
    if with_sample:
        qs_ref, ks_ref, vs_ref = next(it), next(it), next(it)
    y_ref = next(it)
    if with_pool:
        pool_ref = next(it)
    if with_sample:
        os_ref = next(it)
    if with_pool:
        ext_ref = next(it)

    x = x_ref[0]
    if with_pool:
        j = pl.program_id(1)

        @pl.when(j == 0)
        def _():
            ext_ref[0:2 * SUBLANES, :] = jnp.zeros((2 * SUBLANES, D_MODEL), F32)

        u = _rms(x, gmix_ref[...])
        ext_ref[pl.ds(2 * SUBLANES, tq), :] = u
        pos = j * tq + lax.broadcasted_iota(jnp.int32, (tq, 1), 0)
        pooled = _pool_windows_rolled(ext_ref, u, pos, tq)
        x = x + _pool_mix(pooled, wpool_ref, pscale_ref[...])
        pool_ref[0] = ext_ref[pl.ds(tq + 1, POOL_BUF), :]
        ext_ref[0:2 * SUBLANES, :] = ext_ref[pl.ds(tq, 2 * SUBLANES), :]

    if with_sample:
        _attn_sample_stage(qs_ref, ks_ref, vs_ref, os_ref)

    q = _dot(_rms(x, gx_ref[...]).astype(BF16), wq_ref[...])
    o = _attn_heads(
        q,
        lambda h: kb_ref[0, 0, :, h * XATTN_HEAD_DIM:(h + 1) * XATTN_HEAD_DIM],
        lambda h: vb_ref[0, 0, :, h * XATTN_HEAD_DIM:(h + 1) * XATTN_HEAD_DIM],
    )
    x = x + _dot(o.astype(BF16), wo_ref[...])
    x = _mlp(x, gm_ref[...], wup_ref, wdn_ref)
    if with_final:
        x = _rms(x, gf_ref[...])
    y_ref[0] = x


def _prompt_layer_call(x, layer, kb, vb, gx, wq, wo, gm, wup, wdn, pool=None, gfinal=None, sample=None, tq=512):
    b, seq, _ = x.shape
    with_pool, with_final, with_sample = pool is not None, gfinal is not None, sample is not None
    nj = seq // tq
    xspec = pl.BlockSpec((1, tq, D_MODEL), lambda i, j: (i, j, 0))
    kvspec = pl.BlockSpec((1, 1, N_MEM, D_MODEL), lambda i, j: (layer, i, 0, 0))
    args, specs = [x], [xspec]
    if with_pool:
        gmix, wpool, pscale = pool
        args += [gmix, wpool, pscale]
        specs += [_layer_spec(gmix, layer), _layer_spec(wpool, layer // 2), _layer_spec(pscale, layer // 2)]
    args += [gx, wq, kb, vb, wo, gm, wup, wdn]
    specs += [_layer_spec(gx, layer), _layer_spec(wq, layer), kvspec, kvspec, _layer_spec(wo, layer),
              _layer_spec(gm, layer), _layer_spec(wup, layer), _layer_spec(wdn, layer)]
    if with_final:
        args.append(gfinal)
        specs.append(_const_spec(gfinal.shape))
    out_shape = [jax.ShapeDtypeStruct(x.shape, F32)]
    out_specs = [xspec]
    scratch = []
    if with_pool:
        out_shape.append(jax.ShapeDtypeStruct((b, POOL_BUF, D_MODEL), F32))
        out_specs.append(pl.BlockSpec((1, POOL_BUF, D_MODEL), lambda i, j: (i, 0, 0)))
        scratch.append(pltpu.VMEM((tq + 2 * SUBLANES, D_MODEL), F32))
    if with_sample:
        q_pieces, k_rows, v_rows = sample
        n_batch = k_rows.shape[1]
        nb = n_batch // (b * nj)
        rows = q_pieces.shape[2] // (b * nj)
        assert nb * b * nj == n_batch and rows == nb * SUBLANES
        qspec = pl.BlockSpec(q_pieces.shape[:2] + (rows, LANES), lambda i, j: (0, 0, i * nj + j, 0))
        cspec = pl.BlockSpec((1, nb, KV_ROWS, LANES), lambda i, j: (layer, i * nj + j, 0, 0))
        args += [q_pieces, k_rows, v_rows]
        specs += [qspec, cspec, cspec]
        out_shape.append(jax.ShapeDtypeStruct(q_pieces.shape, F32))
        out_specs.append(qspec)
    outs = pl.pallas_call(
        functools.partial(_prompt_layer_body, tq=tq, with_pool=with_pool, with_final=with_final,
                          with_sample=with_sample),
        grid=(b, nj),
        in_specs=specs,
        out_specs=out_specs,
        out_shape=out_shape,
        scratch_shapes=scratch,
        compiler_params=_params(2),
        name="prompt_layer_pool" if with_pool else "prompt_layer",
    )(*args)
    return outs if len(outs) > 1 else outs[0]


ROW_ACS, ROW_DECAY, ROW_END = 0, 1, 2


def _dt_stage(xn_ref, wdt_ref, dtb_ref, alog_ref, dt_ref, acs_ref, rowsT_ref, mask, n_chunks, seg_last):
    z = _dot(xn_ref[...], wdt_ref[...]) + dtb_ref[...]
    dt_ref[...] = jnp.maximum(z, 0.0) + jnp.log1p(jnp.exp(-jnp.abs(z)))
    a = -jnp.exp(alog_ref[...]) * LOG2_E
    tri = mask.astype(F32)
    for c in range(n_chunks):
        rows = slice(c * Q, (c + 1) * Q)
        dt_c = dt_ref[rows, :]
        acs = jnp.dot(tri, dt_c * a, precision=lax.Precision.HIGHEST, preferred_element_type=F32)
        acs_ref[rows, :] = acs
        acsT, dtT = acs.T, dt_c.T
        rowsT_ref[c, ROW_ACS] = acsT
        rowsT_ref[c, ROW_DECAY] = acsT - jnp.log2(dtT)
        rowsT_ref[c, ROW_END] = dtT * jnp.exp2(seg_last(acs).T - acsT)


def _conv_rows(load_shifted, cw_ref, cb_ref, g):
    acc = cb_ref[g]
    for k in range(D_CONV):
        acc = acc + cw_ref[g, k:k + 1, :] * load_shifted(k)
    return _silu(acc)


def _conv_rolled(ext, cw_ref, cb_ref, g):
    acc = cb_ref[g] + cw_ref[g, D_CONV - 1:D_CONV, :] * ext
    for s in range(1, D_CONV):
        acc = acc + cw_ref[g, D_CONV - 1 - s:D_CONV - s, :] * pltpu.roll(ext, s, axis=0)
    return _silu(acc[SUBLANES:, :])


def _expand_heads(cols):
    half = lax.broadcasted_iota(jnp.int32, (Q, LANES), 1) < SSM_HEADDIM
    return jnp.concatenate([jnp.where(half, cols[0], cols[1]), jnp.where(half, cols[2], cols[3])], axis=1)


def _head_lane_masks():
    lane_head = _div_pow2(lax.broadcasted_iota(jnp.int32, (Q, GROUP_X), 1), SSM_HEADDIM)
    return [jnp.where(lane_head == r, 1.0, 0.0).astype(BF16) for r in range(HEADS_PER_GROUP)]


def _ssd_group(xgs, bgs, cgs, acs_cols, row_of, dskip_g, mask, head_masks):
    n = len(xgs)
    heads = range(HEADS_PER_GROUP)
    cbs = [_dot_nt(cg, bg) for cg, bg in zip(cgs, bgs)]
    xbs = [xg.astype(BF16) for xg in xgs]
    lcats = [jnp.concatenate(
        [(cbs[i] * jnp.exp2(jnp.where(mask, acs_cols[i][r] - row_of(i, ROW_DECAY, r), -jnp.inf))).astype(BF16)
         for r in heads], axis=1) for i in range(n)]
    xblks = [jnp.concatenate([xb * hm for hm in head_masks], axis=0) for xb in xbs]
    ydxs = [_dot(l, xb) + dskip_g * xg for l, xb, xg in zip(lcats, xblks, xgs)]
    xTs = [xg.T for xg in xgs]
    xddTs = [jnp.concatenate(
        [xTs[i][r * SSM_HEADDIM:(r + 1) * SSM_HEADDIM, :] * row_of(i, ROW_END, r) for r in heads], axis=0)
        for i in range(n)]
    eacss = [jnp.exp2(_expand_heads(cols)) for cols in acs_cols]
    return ydxs, eacss, xddTs


def _gate_norm(y, z, ng):
    yg = y * _silu(z)
    return yg * lax.rsqrt(jnp.mean(yg * yg, axis=-1, keepdims=True) + EPS) * ng


def _div_pow2(v, d):
    assert d & (d - 1) == 0
    return lax.shift_right_logical(v, d.bit_length() - 1)


def _causal_mask(block):
    li = lax.broadcasted_iota(jnp.int32, (Q, Q), 0)
    si = lax.broadcasted_iota(jnp.int32, (Q, Q), 1)
    m = li >= si
    if block is not None:
        m = jnp.logical_and(m, _div_pow2(li, block) == _div_pow2(si, block))
    return m


def _head_rows(vals):
    return jnp.concatenate([jnp.broadcast_to(v, (SSM_HEADDIM, LANES)) for v in vals], axis=0)


def _mamba_prompt_body(x_ref, gmix_ref, wg_ref, wdt_ref, cw_ref, cb_ref, dtb_ref, alog_ref, dsk_ref, ng_ref, wout_ref,
                       *rest, tq, n_cast):
    cast_in, rest = rest[:n_cast], rest[n_cast:]
    y_ref, ssm_ref, conv_ref = rest[:3]
    cast_out, rest = rest[3:3 + n_cast], rest[3 + n_cast:]
    xn_ref, dt_ref, acs_ref, rowsT_ref, zxg_ref, xpad_ref, act_ref, hist_ref, h_ref, yg_ref = rest
    j = pl.program_id(1)
    n_chunks = tq // Q

    for src, dst in zip(cast_in, cast_out):
        dst[...] = src[...].astype(BF16)

    @pl.when(j == 0)
    def _():
        hist_ref[...] = jnp.zeros_like(hist_ref)
        h_ref[...] = jnp.zeros_like(h_ref)

    mask = _causal_mask(None)
    head_masks = _head_lane_masks()
    xn_ref[...] = _rms(x_ref[0], gmix_ref[...]).astype(BF16)
    _dt_stage(xn_ref, wdt_ref, dtb_ref, alog_ref, dt_ref, acs_ref, rowsT_ref, mask, n_chunks,
              lambda v: jnp.broadcast_to(v[Q - 1:Q, :], v.shape))

    def in_proj(g):
        zxg_ref[g % 2] = _dot(xn_ref[...], wg_ref[g])

    in_proj(0)
    for g in range(SSM_GROUPS):
        if g + 1 < SSM_GROUPS:
            in_proj(g + 1)
        zxg, xpad, act = zxg_ref.at[g % 2], xpad_ref.at[g % 2], act_ref.at[g % 2]
        xpad[0:SUBLANES, :] = hist_ref[g]
        xpad[pl.ds(SUBLANES, tq), :] = zxg[:, GROUP_X:GROUP_IN]

        for c in range(n_chunks):
            act[c * Q:(c + 1) * Q, :] = _conv_rolled(xpad[c * Q:(c + 1) * Q + SUBLANES, :], cw_ref, cb_ref, g)
        hist_ref[g] = xpad[pl.ds(tq, SUBLANES), :]

        h_in = h_ref[g]
        for c0 in range(0, n_chunks, SSD_STAGE):
            chunks = list(range(c0, min(c0 + SSD_STAGE, n_chunks)))
            idx = range(len(chunks))
            rows = [slice(c * Q, (c + 1) * Q) for c in chunks]
            bgs = [act[r, GROUP_X:GROUP_X + D_STATE].astype(BF16) for r in rows]
            cgs = [act[r, GROUP_X + D_STATE:GROUP_CONV].astype(BF16) for r in rows]
            rowsT_cs = [rowsT_ref.at[c] for c in chunks]
            heads = range(HEADS_PER_GROUP * g, HEADS_PER_GROUP * (g + 1))
            acs_cols = [[jnp.broadcast_to(acs_ref[r, h:h + 1], (Q, LANES)) for h in heads] for r in rows]
            ydxs, eacss, xddTs = _ssd_group(
                [act[r, 0:GROUP_X] for r in rows], bgs, cgs, acs_cols,
                lambda i, slot, r: rowsT_cs[i][slot, heads[r]:heads[r] + 1, :], dsk_ref[g], mask, head_masks)
            updates = [_dot(xddTs[i].astype(BF16), bgs[i]) for i in idx]
            scales = [_head_rows([jnp.exp2(rowsT_cs[i][ROW_ACS, h:h + 1, Q - 1:Q]) for h in heads]) for i in idx]
            hs = [h_in]
            for i in idx:
                hs.append(scales[i] * hs[i] + updates[i])
            h_in = hs[-1]
            yoffs = [_dot_nt(cgs[i], hs[i].astype(BF16)) for i in idx]
            for i in idx:
                yg_ref[rows[i], g * GROUP_X:(g + 1) * GROUP_X] = _gate_norm(
                    ydxs[i] + eacss[i] * yoffs[i], zxg[rows[i], 0:GROUP_X], ng_ref[g]).astype(BF16)
        h_ref[g] = h_in

        hist = hist_ref[g]
        conv_ref[0, :, g * GROUP_X:(g + 1) * GROUP_X] = hist[SUBLANES - (D_CONV - 1):, 0:GROUP_X]
        conv_ref[0, :, D_INNER + g * D_STATE:D_INNER + (g + 1) * D_STATE] = (
            hist[SUBLANES - (D_CONV - 1):, GROUP_X:GROUP_X + D_STATE])
        conv_ref[0, :, D_INNER + (SSM_GROUPS + g) * D_STATE:D_INNER + (SSM_GROUPS + g + 1) * D_STATE] = (
            hist[SUBLANES - (D_CONV - 1):, GROUP_X + D_STATE:GROUP_CONV])

        if g % 2 == 1:
            cols = slice((g - 1) * GROUP_X, (g + 1) * GROUP_X)
            y_ref[0] = (x_ref[0] if g == 1 else y_ref[0]) + _dot(yg_ref[:, cols], wout_ref[cols, :])

    @pl.when(j == pl.num_programs(1) - 1)
    def _():
        ssm_ref[0] = h_ref[...]


def _mamba_weight_specs(w):
    return [_const_spec(a.shape) for a in w]


def _mamba_prompt_call(x, gmix, layer, w, casts=(), tq=512):
    b, seq, _ = x.shape
    nj = seq // tq
    xspec = pl.BlockSpec((1, tq, D_MODEL), lambda i, j: (i, j, 0))
    n_chunks = tq // Q
    cast2d = [a.reshape(-1, a.shape[-1]) for a in casts]
    cast_specs = []
    for a in cast2d:
        rows = a.shape[0] // (b * nj)
        assert rows * b * nj == a.shape[0] and rows % (2 * SUBLANES) == 0
        cast_specs.append(pl.BlockSpec((rows, a.shape[1]), lambda i, j: (i * nj + j, 0)))
    outs = pl.pallas_call(
        functools.partial(_mamba_prompt_body, tq=tq, n_cast=len(casts)),
        grid=(b, nj),
        in_specs=[xspec, _layer_spec(gmix, layer)] + _mamba_weight_specs(w) + cast_specs,
        out_specs=[xspec,
                   pl.BlockSpec((1, SSM_GROUPS, GROUP_X, D_STATE), lambda i, j: (i, 0, 0, 0)),
                   pl.BlockSpec((1, D_CONV - 1, CONV_DIM), lambda i, j: (i, 0, 0))] + cast_specs,
        out_shape=[jax.ShapeDtypeStruct(x.shape, F32),
                   jax.ShapeDtypeStruct((b, SSM_GROUPS, GROUP_X, D_STATE), F32),
                   jax.ShapeDtypeStruct((b, D_CONV - 1, CONV_DIM), F32)]
        + [jax.ShapeDtypeStruct(a.shape, BF16) for a in cast2d],
        scratch_shapes=[
            pltpu.VMEM((tq, D_MODEL), BF16),
            pltpu.VMEM((tq, LANES), F32),
            pltpu.VMEM((tq, LANES), F32),
            pltpu.VMEM((n_chunks, 3, LANES, Q), F32),
            pltpu.VMEM((2, tq, GROUP_IN), F32),
            pltpu.VMEM((2, tq + SUBLANES, GROUP_CONV), F32),
            pltpu.VMEM((2, tq, GROUP_CONV), F32),
            pltpu.VMEM((SSM_GROUPS, SUBLANES, GROUP_CONV), F32),
            pltpu.VMEM((SSM_GROUPS, GROUP_X, D_STATE), F32),
            pltpu.VMEM((tq, D_INNER), BF16),
        ],
        compiler_params=_params(2),
        name="mamba_prompt",
    )(x, gmix, *w, *cast2d)
    return list(outs[:3]) + [o.reshape(a.shape) for o, a in zip(outs[3:], casts)]


def _mamba_sample_body(x_ref, gmix_ref, wg_ref, wdt_ref, cw_ref, cb_ref, dtb_ref, alog_ref, dsk_ref, ng_ref, wout_ref,
                       cx_ref, cbm_ref, ccm_ref, sst_ref, gx_ref, wq_ref,
                       y_ref, q_ref, ssm_ref, ox_ref, obm_ref, ocm_ref,
                       xn_ref, dt_ref, acs_ref, rowsT_ref, colb_ref, scale_ref, zxg_ref, xpad_ref, yg_ref, *, seq, gp):
    s = pl.program_id(1)
    nb = Q // seq
    tail = SUBLANES - (D_CONV - 1)
    mask = _causal_mask(seq)
    head_masks = _head_lane_masks()

    @pl.when(s == 0)
    def _():
        def seg_last(v):
            v3 = v.reshape(nb, seq, v.shape[-1])
            return jnp.broadcast_to(v3[:, seq - 1:seq, :], v3.shape).reshape(v.shape)

        xn_ref[...] = _rms(x_ref[...], gmix_ref[...]).astype(BF16)
        _dt_stage(xn_ref, wdt_ref, dtb_ref, alog_ref, dt_ref, acs_ref, rowsT_ref, mask, 1, seg_last)
        acs = acs_ref[...]
        for h in range(SSM_HEADS):
            colb_ref[h] = jnp.broadcast_to(acs[:, h:h + 1], (Q, LANES))
        acsT = rowsT_ref[0, ROW_ACS]
        for bi in range(nb):
            scale_ref[bi] = jnp.broadcast_to(jnp.exp2(acsT[:, (bi + 1) * seq - 1:(bi + 1) * seq]), (LANES, LANES))

    pair_b = _div_pow2(lax.broadcasted_iota(jnp.int32, (2 * seq, GROUP_X), 0), seq)
    lane_b = _div_pow2(lax.broadcasted_iota(jnp.int32, (GROUP_X, Q), 1), seq)
    for gi in range(gp):
        g = s * gp + gi
        h0 = g * HEADS_PER_GROUP
        zxg_ref[...] = _dot(xn_ref[...], wg_ref[g])
        parts = ((cx_ref, ox_ref, GROUP_X, slice(0, GROUP_X)),
                 (cbm_ref, obm_ref, D_STATE, slice(GROUP_X, GROUP_X + D_STATE)),
                 (ccm_ref, ocm_ref, D_STATE, slice(GROUP_X + D_STATE, GROUP_CONV)))
        for src, _, w, dst in parts:
            xpad_ref[:, tail:SUBLANES, dst] = src[:, :, gi * w:(gi + 1) * w]
        xpad_ref[:, SUBLANES:SUBLANES + seq, :] = zxg_ref[:, GROUP_X:GROUP_IN].reshape(nb, seq, GROUP_CONV)
        act = _conv_rows(lambda k: xpad_ref[:, pl.ds(tail + k, seq), :].reshape(Q, GROUP_CONV), cw_ref, cb_ref, g)
        for _, out, w, dst in parts:
            out[:, :, gi * w:(gi + 1) * w] = xpad_ref[:, pl.ds(tail + seq, D_CONV - 1), dst]
        xg = act[:, 0:GROUP_X]
        bg = act[:, GROUP_X:GROUP_X + D_STATE].astype(BF16)
        cg = act[:, GROUP_X + D_STATE:GROUP_CONV].astype(BF16)
        ydxs, eacss, xddTs = _ssd_group(
            [xg], [bg], [cg], [[colb_ref[h0 + r] for r in range(HEADS_PER_GROUP)]],
            lambda i, slot, r: rowsT_ref[0, slot, pl.ds(h0 + r, 1), :], dsk_ref[g], mask, head_masks)
        xddT = xddTs[0]
        yoffs = []
        for p in range(nb // 2):
            cp = cg[2 * p * seq:2 * (p + 1) * seq, :]
            yoffs.append(jnp.where(pair_b == 0, _dot_nt(cp, sst_ref[2 * p, gi].astype(BF16)),
                                   _dot_nt(cp, sst_ref[2 * p + 1, gi].astype(BF16))))
        yoff = jnp.concatenate(yoffs, axis=0)
        for bi in range(nb):
            scale = _head_rows([scale_ref[bi, pl.ds(h0 + r, 1), :] for r in range(HEADS_PER_GROUP)])
            xm = jnp.where(lane_b == bi, xddT, 0.0).astype(BF16)
            ssm_ref[bi, gi] = scale * sst_ref[bi, gi] + _dot(xm, bg)
        yg_ref[g] = _gate_norm(ydxs[0] + eacss[0] * yoff, zxg_ref[:, 0:GROUP_X], ng_ref[g]).astype(BF16)

    @pl.when(s == pl.num_programs(1) - 1)
    def _():
        x1 = x_ref[...]
        for g in range(SSM_GROUPS):
            x1 = x1 + _dot(yg_ref[g], wout_ref[g * GROUP_X:(g + 1) * GROUP_X, :])
        y_ref[...] = x1
        _store_pieces(q_ref, _dot(_rms(x1, gx_ref[...]).astype(BF16), wq_ref[...]))


def _mamba_sample_call(x2d, gmix, w, conv_state, ssm_state, gx, wq, layer, seq, gp=2):
    n = x2d.shape[0]
    batch = n // seq
    nb = Q // seq
    bc_cols = SSM_GROUPS * D_STATE
    xspec = pl.BlockSpec((Q, D_MODEL), lambda i, s: (i, 0))
    stspec = pl.BlockSpec((nb, gp, GROUP_X, D_STATE), lambda i, s: (i, s, 0, 0))

    def cols(width, first):
        assert first % (gp * width) == 0
        return pl.BlockSpec((nb, D_CONV - 1, gp * width), lambda i, s: (i, 0, first // (gp * width) + s))

    conv_in = [cols(GROUP_X, 0), cols(D_STATE, D_INNER), cols(D_STATE, D_INNER + bc_cols)]
    conv_out = [cols(GROUP_X, 0), cols(D_STATE, 0), cols(D_STATE, 0)]
    y, q, ssm, ox, obm, ocm = pl.pallas_call(
        functools.partial(_mamba_sample_body, seq=seq, gp=gp),
        grid=(n // Q, SSM_GROUPS // gp),
        in_specs=[xspec, _layer_spec(gmix, layer)] + _mamba_weight_specs(w)
        + conv_in + [stspec, _layer_spec(gx, layer), _layer_spec(wq, layer)],
        out_specs=[xspec, _pieces_spec(Q, lambda i, s: (0, 0, i, 0)), stspec] + conv_out,
        out_shape=[jax.ShapeDtypeStruct(x2d.shape, F32), _pieces_shape(n),
                   jax.ShapeDtypeStruct((batch, SSM_GROUPS, GROUP_X, D_STATE), F32),
                   jax.ShapeDtypeStruct((batch, D_CONV - 1, D_INNER), F32),
                   jax.ShapeDtypeStruct((batch, D_CONV - 1, bc_cols), F32),
                   jax.ShapeDtypeStruct((batch, D_CONV - 1, bc_cols), F32)],
        scratch_shapes=[
            pltpu.VMEM((Q, D_MODEL), BF16),
            pltpu.VMEM((Q, LANES), F32),
            pltpu.VMEM((Q, LANES), F32),
            pltpu.VMEM((1, 3, LANES, Q), F32),
            pltpu.VMEM((SSM_HEADS, Q, LANES), F32),
            pltpu.VMEM((nb, LANES, LANES), F32),
            pltpu.VMEM((Q, GROUP_IN), F32),
            pltpu.VMEM((nb, SUBLANES + seq, GROUP_CONV), F32),
            pltpu.VMEM((SSM_GROUPS, Q, GROUP_X), BF16),
        ],
        compiler_params=_params(2),
        name="mamba_sample",
    )(x2d, gmix, *w, conv_state, conv_state, conv_state, ssm_state, gx, wq)
    return y, q, ssm, jnp.concatenate([ox, obm, ocm], axis=-1)


def _pool_sample_body(x_ref, st_ref, gmix_ref, wpool_ref, pscale_ref, gx_ref, wq_ref,
                      y_ref, q_ref, pool_ref, ext_ref, *, seq, pos0):
    nb = x_ref.shape[0]
    n = nb * seq
    x = x_ref[...].reshape(n, D_MODEL)
    u = _rms(x, gmix_ref[...])
    hist0 = 2 * SUBLANES - POOL_BUF
    ext_ref[:, pl.ds(hist0, POOL_BUF), :] = st_ref[...]
    ext_ref[:, 2 * SUBLANES:2 * SUBLANES + seq, :] = u.reshape(nb, seq, D_MODEL)
    pos = pos0 + lax.broadcasted_iota(jnp.int32, (nb, seq, 1), 1).reshape(n, 1)
    pooled = _pool_windows(
        lambda k, cols: ext_ref[:, pl.ds(2 * SUBLANES - k, seq), cols].reshape(n, POOL_GROUP_DIM), u, pos, n)
    x1 = x + _pool_mix(pooled, wpool_ref, pscale_ref[...])
    pool_ref[...] = ext_ref[:, pl.ds(hist0 + seq, POOL_BUF), :]
    y_ref[...] = x1.reshape(nb, seq, D_MODEL)
    _store_pieces(q_ref, _dot(_rms(x1, gx_ref[...]).astype(BF16), wq_ref[...]))


def _pool_sample_call(x, pool_state, gmix, wpool, pscale, gx, wq, layer, pos0, nb=16):
    batch, seq, _ = x.shape
    xspec = pl.BlockSpec((nb, seq, D_MODEL), lambda i: (i, 0, 0))
    pspec = pl.BlockSpec((nb, POOL_BUF, D_MODEL), lambda i: (i, 0, 0))
    return pl.pallas_call(
        functools.partial(_pool_sample_body, seq=seq, pos0=pos0),
        grid=(batch // nb,),
        in_specs=[xspec, pspec, _layer_spec(gmix, layer), _layer_spec(wpool, layer // 2),
                  _layer_spec(pscale, layer // 2), _layer_spec(gx, layer), _layer_spec(wq, layer)],
        out_specs=[xspec, _pieces_spec(nb * seq, lambda i: (0, 0, i, 0)), pspec],
        out_shape=[jax.ShapeDtypeStruct(x.shape, F32), _pieces_shape(batch * seq),
                   jax.ShapeDtypeStruct(pool_state.shape, F32)],
        scratch_shapes=[pltpu.VMEM((nb, 2 * SUBLANES + seq, D_MODEL), F32)],
        compiler_params=_params(1),
        name="pool_sample",
    )(x, pool_state, gmix, wpool, pscale, gx, wq)


HEAD_HALVES = XATTN_HEAD_DIM // LANES
KV_ROWS = N_MEM * XATTN_HEADS * HEAD_HALVES


def _attn_sample_stage(q_ref, k_ref, v_ref, o_ref):
    nb = k_ref.shape[1]
    seq = q_ref.shape[2] // nb
    assert seq == SUBLANES
    pieces = [(j, h) for j in range(HEAD_HALVES) for h in range(XATTN_HEADS)]
    half = XATTN_HEADS * seq
    grp = XATTN_HEADS * HEAD_HALVES
    lane = lax.broadcasted_iota(jnp.int32, (half, KV_ROWS), 1)
    rowi = lax.broadcasted_iota(jnp.int32, (half, KV_ROWS), 0)
    valid = (lane & (grp - 1)) == _div_pow2(rowi, seq)
    qs = [jnp.concatenate([q_ref[j, h, bi * seq:(bi + 1) * seq, :] for j, h in pieces], axis=0) for bi in range(nb)]
    prods = [_dot_nt((qs[bi] * (XATTN_HEAD_DIM ** -0.5)).astype(BF16), k_ref[0, bi].astype(BF16))
             for bi in range(nb)]
    ss = [jnp.where(valid, pr[0:half] + pltpu.roll(pr[half:], KV_ROWS - XATTN_HEADS, axis=1), -jnp.inf)
          for pr in prods]
    es = [jnp.exp(s - jnp.max(s, axis=-1, keepdims=True)) for s in ss]
    ps = [e * (1.0 / jnp.sum(e, axis=-1, keepdims=True)) for e in es]
    pes = [jnp.concatenate([p, pltpu.roll(p, XATTN_HEADS, axis=1)], axis=0).astype(BF16) for p in ps]
    for bi in range(nb):
        o = _dot(pes[bi], v_ref[0, bi].astype(BF16))
        for n, (j, h) in enumerate(pieces):
            o_ref[j, h, bi * seq:(bi + 1) * seq, :] = o[n * seq:(n + 1) * seq, :]


def _cache_rows(cache):
    d, b = cache.shape[:2]
    c = cache.reshape(d, b, N_MEM, XATTN_HEADS, HEAD_HALVES, LANES)
    return c.transpose(0, 1, 2, 4, 3, 5).reshape(d, b, KV_ROWS, LANES)


def _store_pieces(ref, v):
    for h in range(XATTN_HEADS):
        for j in range(HEAD_HALVES):
            c0 = (h * HEAD_HALVES + j) * LANES
            ref[j, h] = v[:, c0:c0 + LANES]


def _load_pieces(ref):
    return jnp.concatenate([ref[j, h] for h in range(XATTN_HEADS) for j in range(HEAD_HALVES)], axis=-1)


def _pieces_spec(rows, index_map):
    return pl.BlockSpec((HEAD_HALVES, XATTN_HEADS, rows, LANES), index_map)


def _pieces_shape(n_rows):
    return jax.ShapeDtypeStruct((HEAD_HALVES, XATTN_HEADS, n_rows, LANES), F32)


def _out_mlp_body(*refs, with_final):
    it = iter(refs)
    x_ref, o_ref, wo_ref, gm_ref, wup_ref, wdn_ref = (next(it) for _ in range(6))
    if with_final:
        gf_ref = next(it)
    y_ref = next(it)
    x = x_ref[...] + _dot(_load_pieces(o_ref).astype(BF16), wo_ref[...])
    x = _mlp(x, gm_ref[...], wup_ref, wdn_ref)
    if with_final:
        x = _rms(x, gf_ref[...])
    y_ref[...] = x


def _out_mlp_call(x2d, o_pieces, layer, wo, gm, wup, wdn, gfinal=None, tm=512):
    n = x2d.shape[0]
    tm = min(tm, n)
    row = pl.BlockSpec((tm, D_MODEL), lambda i: (i, 0))
    args = [x2d, o_pieces, wo, gm, wup, wdn]
    specs = [row, _pieces_spec(tm, lambda i: (0, 0, i, 0)), _layer_spec(wo, layer), _layer_spec(gm, layer),
             _layer_spec(wup, layer), _layer_spec(wdn, layer)]
    if gfinal is not None:
        args.append(gfinal)
        specs.append(_const_spec(gfinal.shape))
    return pl.pallas_call(
        functools.partial(_out_mlp_body, with_final=gfinal is not None),
        grid=(n // tm,),
        in_specs=specs,
        out_specs=row,
        out_shape=jax.ShapeDtypeStruct(x2d.shape, F32),
        compiler_params=_params(1),
        name="out_mlp",
    )(*args)


def _mamba_weights(wg, w_in_t, conv_w, conv_b, dt_bias, a_log, d_skip, norm_gated, w_out_bf16):
    pad = LANES - SSM_HEADS
    wdt = jnp.pad(w_in_t[D_INNER + CONV_DIM:, :].T, ((0, 0), (0, pad))).astype(BF16)

    def grp(a, width):
        return a.reshape(a.shape[0], SSM_GROUPS, width)

    def conv_grp(a):
        return jnp.concatenate([grp(a[:, :D_INNER], GROUP_X), grp(a[:, D_INNER:D_INNER + SSM_GROUPS * D_STATE], D_STATE),
                                grp(a[:, D_INNER + SSM_GROUPS * D_STATE:], D_STATE)], axis=-1).transpose(1, 0, 2)

    cw = conv_grp(conv_w)
    cb = conv_grp(conv_b[None, :])
    dtb = jnp.pad(dt_bias, (0, pad))[None, :]
    alog = jnp.pad(a_log, (0, pad))[None, :]
    dsk = jnp.repeat(d_skip, SSM_HEADDIM).reshape(SSM_GROUPS, 1, GROUP_X)
    return (wg, wdt, cw, cb, dtb, alog, dsk, norm_gated.reshape(SSM_GROUPS, 1, GROUP_X), w_out_bf16)


def kernel(x_prompt, x_sample, cache_mem_k, cache_mem_v, state_ssm, state_conv, state_pool, mem_prompt, norm_mix, norm_xattn, norm_mem, norm_mlp, norm_final, w_in, conv_w, conv_b, dt_bias, a_log, d_skip, norm_gated, w_out, w_pool, pool_scale, w_xq, w_xk, w_xv, w_xo, w_up, w_down):
    bp, seq_p, _ = x_prompt.shape
    bs, seq_s, _ = x_sample.shape
    depth = w_xq.shape[0]
    per_layer = lambda a: a[:, None, :]

    wpool = w_pool.astype(BF16)
    g_mix, g_x, g_mlp, p_scale = (per_layer(a) for a in (norm_mix, norm_xattn, norm_mlp, pool_scale))
    g_final = norm_final[None, :]

    w_in_t = w_in[0].T
    k_p, v_p, kb, vb, wg, wout = _kv_call(mem_prompt.reshape(bp * N_MEM, D_MODEL), per_layer(norm_mem), w_xk, w_xv,
                                          w_in_t, w_out[0])
    mw = _mamba_weights(wg, w_in_t, conv_w[0], conv_b[0], dt_bias[0], a_log[0], d_skip[0], norm_gated[0], wout)
    kv_shape = (depth, bp, N_MEM, XATTN_HEADS, XATTN_HEAD_DIM)
    kb = kb.reshape(depth, bp, N_MEM, D_MODEL)
    vb = vb.reshape(depth, bp, N_MEM, D_MODEL)

    kc, vc = _cache_rows(cache_mem_k), _cache_rows(cache_mem_v)
    xp, ssm_p, conv_p, wq, wo, wup, wdn = _mamba_prompt_call(
        x_prompt, g_mix, 0, mw, casts=(w_xq, w_xo, w_up, w_down))
    xs = x_sample.reshape(bs * seq_s, D_MODEL)
    ssm_in = state_ssm[0].reshape(bs, SSM_GROUPS, GROUP_X, D_STATE)
    xs, q, ssm_s, conv_s = _mamba_sample_call(xs, g_mix, mw, state_conv[0], ssm_in, g_x, wq, 0, seq_s)
    xp, o = _prompt_layer_call(xp, 0, kb, vb, g_x, wq, wo, g_mlp, wup, wdn, sample=(q, kc, vc))
    xs = _out_mlp_call(xs, o, 0, wo, g_mlp, wup, wdn)
    xs, q, pool_s = _pool_sample_call(xs.reshape(bs, seq_s, D_MODEL), state_pool[0], g_mix, wpool, p_scale,
                                      g_x, wq, 1, PAST_LEN)
    y_prompt, pool_p, o = _prompt_layer_call(
        xp, 1, kb, vb, g_x, wq, wo, g_mlp, wup, wdn,
        pool=(g_mix, wpool, p_scale), gfinal=g_final, sample=(q, kc, vc))
    y_sample = _out_mlp_call(xs.reshape(bs * seq_s, D_MODEL), o, 1, wo, g_mlp, wup, wdn, gfinal=g_final)

    ssm_shape = (1, -1, SSM_HEADS, SSM_HEADDIM, D_STATE)
    return (y_prompt, y_sample.reshape(bs, seq_s, D_MODEL),
            k_p.reshape(kv_shape), v_p.reshape(kv_shape),
            ssm_p.reshape(ssm_shape), conv_p[None], pool_p[None],
            ssm_s.reshape(ssm_shape), conv_s[None], pool_s[None])
```

```python
import functools

import jax
import jax.numpy as jnp
from jax import lax
from jax.experimental import pallas as pl
from jax.experimental.pallas import tpu as pltpu

F32 = jnp.float32
BF16 = jnp.bfloat16

D_MODEL = 1024
D_INNER = 2048
SSM_HEADS = 32
SSM_HEADDIM = 64
SSM_GROUPS = 8
HEADS_PER_GROUP = 4
GROUP_X = HEADS_PER_GROUP * SSM_HEADDIM
D_STATE = 128
D_CONV = 4
CONV_DIM = D_INNER + 2 * SSM_GROUPS * D_STATE
GROUP_CONV = GROUP_X + 2 * D_STATE
GROUP_IN = GROUP_X + GROUP_CONV
POOL_WINDOWS = (2, 4, 8, 16)
POOL_GROUP_DIM = 256
POOL_BUF = 15
N_MEM = 256
XATTN_HEADS = 4
XATTN_HEAD_DIM = 256
D_FF = 4096
EPS = 1e-5
PAST_LEN = 16384
LOG2_E = 1.4426950408889634

LANES = 128
SUBLANES = 8
Q = 128
SSD_STAGE = 2
FF_CHUNK = 1024
VMEM_LIMIT = 56 * 1024 * 1024

_NT = (((1,), (1,)), ((), ()))


def _params(n_grid):
    return pltpu.CompilerParams(dimension_semantics=("arbitrary",) * n_grid, vmem_limit_bytes=VMEM_LIMIT)


def _const_spec(shape):
    nd = len(shape)
    return pl.BlockSpec(shape, lambda *_: (0,) * nd, pipeline_mode=pl.Buffered(1))


def _layer_spec(a, layer):
    nd = a.ndim
    return pl.BlockSpec((None,) + a.shape[1:], lambda *_: (layer,) + (0,) * (nd - 1), pipeline_mode=pl.Buffered(1))


def _dot(a, b):
    return jnp.dot(a, b, preferred_element_type=F32)


def _dot_nt(a, b):
    return lax.dot_general(a, b, _NT, preferred_element_type=F32)


def _rms(x, g):
    return x * lax.rsqrt(jnp.mean(x * x, axis=-1, keepdims=True) + EPS) * g


def _silu(x):
    h = 0.5 * x
    return h + h * jnp.tanh(h)


def _kv_body(mem_ref, g_ref, wk_ref, wv_ref, wz_ref, wx_ref, wb_ref, wc_ref, wt_ref, wo_ref,
             k_ref, v_ref, kb_ref, vb_ref, wg_ref, wdt_ref, wob_ref):
    @pl.when(jnp.logical_and(pl.program_id(0) == 0, pl.program_id(1) == 0))
    def _():
        pad = jnp.zeros((LANES - SSM_HEADS, D_MODEL), F32)
        wdt_ref[...] = jnp.concatenate([wt_ref[0:SSM_HEADS, :], pad], axis=0).T.astype(BF16)

    wg_ref[0, :, 0:GROUP_X] = wz_ref[...].T.astype(BF16)
    wg_ref[0, :, GROUP_X:2 * GROUP_X] = wx_ref[...].T.astype(BF16)
    wg_ref[0, :, 2 * GROUP_X:2 * GROUP_X + D_STATE] = wb_ref[...].T.astype(BF16)
    wg_ref[0, :, 2 * GROUP_X + D_STATE:GROUP_IN] = wc_ref[...].T.astype(BF16)
    wob_ref[...] = wo_ref[...].astype(BF16)
    mn = _rms(mem_ref[...], g_ref[0]).astype(BF16)
    k = _dot(mn, wk_ref[0].astype(BF16))
    v = _dot(mn, wv_ref[0].astype(BF16))
    for h in range(XATTN_HEADS):
        sl = slice(h * XATTN_HEAD_DIM, (h + 1) * XATTN_HEAD_DIM)
        k_ref[0, :, h, :] = k[:, sl]
        v_ref[0, :, h, :] = v[:, sl]
    kb_ref[0] = k.astype(BF16)
    vb_ref[0] = v.astype(BF16)


def _kv_call(mem, g_mem, wk, wv, w_in_t, w_out):
    n = mem.shape[0]
    depth = wk.shape[0]
    nt = SSM_GROUPS // depth
    tm = n // nt
    assert depth * nt == SSM_GROUPS and tm * nt == n and tm % SUBLANES == 0
    row = pl.BlockSpec((tm, D_MODEL), lambda l, t: (t, 0))
    wspec = pl.BlockSpec((1, D_MODEL, D_MODEL), lambda l, t: (l, 0, 0))
    ospec = pl.BlockSpec((1, tm, D_MODEL), lambda l, t: (l, t, 0))
    hspec = pl.BlockSpec((1, tm, XATTN_HEADS, XATTN_HEAD_DIM), lambda l, t: (l, t, 0, 0))

    def rows(height, first_block):
        return pl.BlockSpec((height, D_MODEL), lambda l, t: (first_block + l * nt + t, 0))

    x0 = D_INNER // GROUP_X
    b0 = 2 * D_INNER // D_STATE
    wo_spec = pl.BlockSpec((w_out.shape[0] // SSM_GROUPS, D_MODEL), lambda l, t: (l * nt + t, 0))
    dt0 = D_INNER + CONV_DIM
    assert dt0 % LANES == 0 and w_in_t.shape[0] - dt0 == SSM_HEADS
    dt_spec = pl.BlockSpec((LANES, D_MODEL), lambda l, t: (dt0 // LANES, 0))
    return pl.pallas_call(
        _kv_body,
        grid=(depth, nt),
        in_specs=[row, pl.BlockSpec((1, 1, D_MODEL), lambda l, t: (l, 0, 0)), wspec, wspec,
                  rows(GROUP_X, 0), rows(GROUP_X, x0), rows(D_STATE, b0), rows(D_STATE, b0 + SSM_GROUPS),
                  dt_spec, wo_spec],
        out_specs=[hspec, hspec, ospec, ospec,
                   pl.BlockSpec((1, D_MODEL, GROUP_IN), lambda l, t: (l * nt + t, 0, 0)),
                   pl.BlockSpec((D_MODEL, LANES), lambda l, t: (0, 0)), wo_spec],
        out_shape=[jax.ShapeDtypeStruct((depth, n, XATTN_HEADS, XATTN_HEAD_DIM), F32)] * 2
        + [jax.ShapeDtypeStruct((depth, n, D_MODEL), BF16)] * 2
        + [jax.ShapeDtypeStruct((SSM_GROUPS, D_MODEL, GROUP_IN), BF16),
           jax.ShapeDtypeStruct((D_MODEL, LANES), BF16), jax.ShapeDtypeStruct(w_out.shape, BF16)],
        compiler_params=_params(2),
        name="mem_kv",
    )(mem, g_mem, wk, wv, w_in_t, w_in_t, w_in_t, w_in_t, w_in_t, w_out)


def _attn_heads(q, k_of, v_of):
    outs = []
    for h in range(XATTN_HEADS):
        sl = slice(h * XATTN_HEAD_DIM, (h + 1) * XATTN_HEAD_DIM)
        s = _dot_nt(q[:, sl].astype(BF16), k_of(h)) * (XATTN_HEAD_DIM ** -0.5)
        p = jnp.exp(s - jnp.max(s, axis=-1, keepdims=True))
        p = p * (1.0 / jnp.sum(p, axis=-1, keepdims=True))
        outs.append(_dot(p.astype(BF16), v_of(h)))
    return jnp.concatenate(outs, axis=-1)


def _mlp(x, g, wup_ref, wdn_ref):
    hn = _rms(x, g).astype(BF16)
    acc = x
    for c in range(D_FF // FF_CHUNK):
        a = jnp.maximum(_dot(hn, wup_ref[:, c * FF_CHUNK:(c + 1) * FF_CHUNK]), 0.0)
        acc = acc + _dot((a * a).astype(BF16), wdn_ref[c * FF_CHUNK:(c + 1) * FF_CHUNK, :])
    return acc


def _pool_windows(load, u, pos, n):
    outs = []
    for gi, w in enumerate(POOL_WINDOWS):
        cols = slice(gi * POOL_GROUP_DIM, (gi + 1) * POOL_GROUP_DIM)
        acc = load(0, cols)
        for k in range(1, w):
            acc = acc + load(k, cols)
        cnt = jnp.minimum(pos + 1, w).astype(F32)
        outs.append(acc / cnt - u[:, cols])
    return outs


def _pool_windows_rolled(ext_ref, u, pos, n):
    outs = []
    for gi, w in enumerate(POOL_WINDOWS):
        assert w & (w - 1) == 0 and w <= 2 * SUBLANES
        cols = slice(gi * POOL_GROUP_DIM, (gi + 1) * POOL_GROUP_DIM)
        acc = ext_ref[0:n + 2 * SUBLANES, cols]
        span = 1
        while span < w:
            acc = acc + pltpu.roll(acc, span, axis=0)
            span *= 2
        cnt = jnp.minimum(pos + 1, w).astype(F32)
        outs.append(acc[2 * SUBLANES:, :] / cnt - u[:, cols])
    return outs


def _pool_mix(pooled, wpool_ref, scale):
    outs = [_dot(p.astype(BF16), wpool_ref[gi]) for gi, p in enumerate(pooled)]
    return jnp.concatenate(outs, axis=-1) * scale


def _prompt_layer_body(*refs, tq, with_pool, with_final, with_sample):
    it = iter(refs)
    x_ref = next(it)
    if with_pool:
        gmix_ref, wpool_ref, pscale_ref = next(it), next(it), next(it)
    gx_ref, wq_ref, kb_ref, vb_ref, wo_ref = next(it), next(it), next(it), next(it), next(it)
    gm_ref, wup_ref, wdn_ref = next(it), next(it), next(it)
    if with_final:
        gf_ref = next(it)
    if with_sample:
        qs_ref, ks_ref, vs_ref = next(it), next(it), next(it)
    y_ref = next(it)
    if with_pool:
        pool_ref = next(it)
    if with_sample:
        os_ref = next(it)
    if with_pool:
        ext_ref = next(it)

    x = x_ref[0]
    if with_pool:
        j = pl.program_id(1)

        @pl.when(j == 0)
        def _():
            ext_ref[0:2 * SUBLANES, :] = jnp.zeros((2 * SUBLANES, D_MODEL), F32)

        u = _rms(x, gmix_ref[...])
        ext_ref[pl.ds(2 * SUBLANES, tq), :] = u
        pos = j * tq + lax.broadcasted_iota(jnp.int32, (tq, 1), 0)
        pooled = _pool_windows_rolled(ext_ref, u, pos, tq)
        x = x + _pool_mix(pooled, wpool_ref, pscale_ref[...])
        pool_ref[0] = ext_ref[pl.ds(tq + 1, POOL_BUF), :]
        ext_ref[0:2 * SUBLANES, :] = ext_ref[pl.ds(tq, 2 * SUBLANES), :]

    if with_sample:
        _attn_sample_stage(qs_ref, ks_ref, vs_ref, os_ref)

    q = _dot(_rms(x, gx_ref[...]).astype(BF16), wq_ref[...])
    o = _attn_heads(
        q,
        lambda h: kb_ref[0, 0, :, h * XATTN_HEAD_DIM:(h + 1) * XATTN_HEAD_DIM],
        lambda h: vb_ref[0, 0, :, h * XATTN_HEAD_DIM:(h + 1) * XATTN_HEAD_DIM],
    )
    x = x + _dot(o.astype(BF16), wo_ref[...])
    x = _mlp(x, gm_ref[...], wup_ref, wdn_ref)
    if with_final:
        x = _rms(x, gf_ref[...])
    y_ref[0] = x


def _prompt_layer_call(x, layer, kb, vb, gx, wq, wo, gm, wup, wdn, pool=None, gfinal=None, sample=None, tq=512):
    b, seq, _ = x.shape
    with_pool, with_final, with_sample = pool is not None, gfinal is not None, sample is not None
    nj = seq // tq
    xspec = pl.BlockSpec((1, tq, D_MODEL), lambda i, j: (i, j, 0))
    kvspec = pl.BlockSpec((1, 1, N_MEM, D_MODEL), lambda i, j: (layer, i, 0, 0))
    args, specs = [x], [xspec]
    if with_pool:
        gmix, wpool, pscale = pool
        args += [gmix, wpool, pscale]
        specs += [_layer_spec(gmix, layer), _layer_spec(wpool, layer // 2), _layer_spec(pscale, layer // 2)]
    args += [gx, wq, kb, vb, wo, gm, wup, wdn]
    specs += [_layer_spec(gx, layer), _layer_spec(wq, layer), kvspec, kvspec, _layer_spec(wo, layer),
              _layer_spec(gm, layer), _layer_spec(wup, layer), _layer_spec(wdn, layer)]
    if with_final:
        args.append(gfinal)
        specs.append(_const_spec(gfinal.shape))
    out_shape = [jax.ShapeDtypeStruct(x.shape, F32)]
    out_specs = [xspec]
    scratch = []
    if with_pool:
        out_shape.append(jax.ShapeDtypeStruct((b, POOL_BUF, D_MODEL), F32))
        out_specs.append(pl.BlockSpec((1, POOL_BUF, D_MODEL), lambda i, j: (i, 0, 0)))
        scratch.append(pltpu.VMEM((tq + 2 * SUBLANES, D_MODEL), F32))
    if with_sample:
        q_pieces, k_rows, v_rows = sample
        n_batch = k_rows.shape[1]
        nb = n_batch // (b * nj)
        rows = q_pieces.shape[2] // (b * nj)
        assert nb * b * nj == n_batch and rows == nb * SUBLANES
        qspec = pl.BlockSpec(q_pieces.shape[:2] + (rows, LANES), lambda i, j: (0, 0, i * nj + j, 0))
        cspec = pl.BlockSpec((1, nb, KV_ROWS, LANES), lambda i, j: (layer, i * nj + j, 0, 0))
        args += [q_pieces, k_rows, v_rows]
        specs += [qspec, cspec, cspec]
        out_shape.append(jax.ShapeDtypeStruct(q_pieces.shape, F32))
        out_specs.append(qspec)
    outs = pl.pallas_call(
        functools.partial(_prompt_layer_body, tq=tq, with_pool=with_pool, with_final=with_final,
                          with_sample=with_sample),
        grid=(b, nj),
        in_specs=specs,
        out_specs=out_specs,
        out_shape=out_shape,
        scratch_shapes=scratch,
        compiler_params=_params(2),
        name="prompt_layer_pool" if with_pool else "prompt_layer",
    )(*args)
    return outs if len(outs) > 1 else outs[0]


ROW_ACS, ROW_DECAY, ROW_END = 0, 1, 2


def _dt_stage(xn_ref, wdt_ref, dtb_ref, alog_ref, dt_ref, acs_ref, rowsT_ref, mask, n_chunks, seg_last):
    z = _dot(xn_ref[...], wdt_ref[...]) + dtb_ref[...]
    dt_ref[...] = jnp.maximum(z, 0.0) + jnp.log1p(jnp.exp(-jnp.abs(z)))
    a = -jnp.exp(alog_ref[...]) * LOG2_E
    tri = mask.astype(F32)
    for c in range(n_chunks):
        rows = slice(c * Q, (c + 1) * Q)
        dt_c = dt_ref[rows, :]
        acs = jnp.dot(tri, dt_c * a, precision=lax.Precision.HIGHEST, preferred_element_type=F32)
        acs_ref[rows, :] = acs
        acsT, dtT = acs.T, dt_c.T
        rowsT_ref[c, ROW_ACS] = acsT
        rowsT_ref[c, ROW_DECAY] = acsT - jnp.log2(dtT)
        rowsT_ref[c, ROW_END] = dtT * jnp.exp2(seg_last(acs).T - acsT)


def _conv_rows(load_shifted, cw_ref, cb_ref, g):
    acc = cb_ref[g]
    for k in range(D_CONV):
        acc = acc + cw_ref[g, k:k + 1, :] * load_shifted(k)
    return _silu(acc)


def _conv_rolled(ext, cw_ref, cb_ref, g):
    acc = cb_ref[g] + cw_ref[g, D_CONV - 1:D_CONV, :] * ext
    for s in range(1, D_CONV):
        acc = acc + cw_ref[g, D_CONV - 1 - s:D_CONV - s, :] * pltpu.roll(ext, s, axis=0)
    return _silu(acc[SUBLANES:, :])


def _expand_heads(cols):
    half = lax.broadcasted_iota(jnp.int32, (Q, LANES), 1) < SSM_HEADDIM
    return jnp.concatenate([jnp.where(half, cols[0], cols[1]), jnp.where(half, cols[2], cols[3])], axis=1)


def _head_lane_masks():
    lane_head = _div_pow2(lax.broadcasted_iota(jnp.int32, (Q, GROUP_X), 1), SSM_HEADDIM)
    return [jnp.where(lane_head == r, 1.0, 0.0).astype(BF16) for r in range(HEADS_PER_GROUP)]


def _ssd_group(xgs, bgs, cgs, acs_cols, row_of, dskips, mask, head_masks):
    n = len(xgs)
    heads = range(HEADS_PER_GROUP)
    cbs = [_dot_nt(cg, bg) for cg, bg in zip(cgs, bgs)]
    xbs = [xg.astype(BF16) for xg in xgs]
    lcats = [jnp.concatenate(
        [(cbs[i] * jnp.exp2(jnp.where(mask, acs_cols[i][r] - row_of(i, ROW_DECAY, r), -jnp.inf))).astype(BF16)
         for r in heads], axis=1) for i in range(n)]
    xblks = [jnp.concatenate([xb * hm for hm in head_masks], axis=0) for xb in xbs]
    ydxs = [_dot(l, xb) + dsk * xg for l, xb, xg, dsk in zip(lcats, xblks, xgs, dskips)]
    xTs = [xg.T for xg in xgs]
    xddTs = [jnp.concatenate(
        [xTs[i][r * SSM_HEADDIM:(r + 1) * SSM_HEADDIM, :] * row_of(i, ROW_END, r) for r in heads], axis=0)
        for i in range(n)]
    eacss = [jnp.exp2(_expand_heads(cols)) for cols in acs_cols]
    return ydxs, eacss, xddTs


def _gate_norm(y, z, ng):
    yg = y * _silu(z)
    return yg * lax.rsqrt(jnp.mean(yg * yg, axis=-1, keepdims=True) + EPS) * ng


def _div_pow2(v, d):
    assert d & (d - 1) == 0
    return lax.shift_right_logical(v, d.bit_length() - 1)


def _causal_mask(block):
    li = lax.broadcasted_iota(jnp.int32, (Q, Q), 0)
    si = lax.broadcasted_iota(jnp.int32, (Q, Q), 1)
    m = li >= si
    if block is not None:
        m = jnp.logical_and(m, _div_pow2(li, block) == _div_pow2(si, block))
    return m


def _head_rows(vals):
    return jnp.concatenate([jnp.broadcast_to(v, (SSM_HEADDIM, LANES)) for v in vals], axis=0)


def _mamba_prompt_body(x_ref, gmix_ref, wg_ref, wdt_ref, cw_ref, cb_ref, dtb_ref, alog_ref, dsk_ref, ng_ref, wout_ref,
                       *rest, tq, n_cast):
    cast_in, rest = rest[:n_cast], rest[n_cast:]
    y_ref, ssm_ref, conv_ref = rest[:3]
    cast_out, rest = rest[3:3 + n_cast], rest[3 + n_cast:]
    xn_ref, dt_ref, acs_ref, rowsT_ref, zxg_ref, xpad_ref, act_ref, hist_ref, h_ref, yg_ref = rest
    j = pl.program_id(1)
    n_chunks = tq // Q

    for src, dst in zip(cast_in, cast_out):
        dst[...] = src[...].astype(BF16)

    @pl.when(j == 0)
    def _():
        hist_ref[...] = jnp.zeros_like(hist_ref)
        h_ref[...] = jnp.zeros_like(h_ref)

    mask = _causal_mask(None)
    head_masks = _head_lane_masks()
    xn_ref[...] = _rms(x_ref[0], gmix_ref[...]).astype(BF16)
    _dt_stage(xn_ref, wdt_ref, dtb_ref, alog_ref, dt_ref, acs_ref, rowsT_ref, mask, n_chunks,
              lambda v: jnp.broadcast_to(v[Q - 1:Q, :], v.shape))

    def in_proj(g):
        zxg_ref[g % 2] = _dot(xn_ref[...], wg_ref[g])

    in_proj(0)
    for g in range(SSM_GROUPS):
        if g + 1 < SSM_GROUPS:
            in_proj(g + 1)
        zxg, xpad, act = zxg_ref.at[g % 2], xpad_ref.at[g % 2], act_ref.at[g % 2]
        xpad[0:SUBLANES, :] = hist_ref[g]
        xpad[pl.ds(SUBLANES, tq), :] = zxg[:, GROUP_X:GROUP_IN]

        for c in range(n_chunks):
            act[c * Q:(c + 1) * Q, :] = _conv_rolled(xpad[c * Q:(c + 1) * Q + SUBLANES, :], cw_ref, cb_ref, g)
        hist_ref[g] = xpad[pl.ds(tq, SUBLANES), :]

        h_in = h_ref[g]
        for c0 in range(0, n_chunks, SSD_STAGE):
            chunks = list(range(c0, min(c0 + SSD_STAGE, n_chunks)))
            idx = range(len(chunks))
            rows = [slice(c * Q, (c + 1) * Q) for c in chunks]
            bgs = [act[r, GROUP_X:GROUP_X + D_STATE].astype(BF16) for r in rows]
            cgs = [act[r, GROUP_X + D_STATE:GROUP_CONV].astype(BF16) for r in rows]
            rowsT_cs = [rowsT_ref.at[c] for c in chunks]
            heads = range(HEADS_PER_GROUP * g, HEADS_PER_GROUP * (g + 1))
            acs_cols = [[jnp.broadcast_to(acs_ref[r, h:h + 1], (Q, LANES)) for h in heads] for r in rows]
            ydxs, eacss, xddTs = _ssd_group(
                [act[r, 0:GROUP_X] for r in rows], bgs, cgs, acs_cols,
                lambda i, slot, r: rowsT_cs[i][slot, heads[r]:heads[r] + 1, :], [dsk_ref[g]] * len(rows),
                mask, head_masks)
            updates = [_dot(xddTs[i].astype(BF16), bgs[i]) for i in idx]
            scales = [_head_rows([jnp.exp2(rowsT_cs[i][ROW_ACS, h:h + 1, Q - 1:Q]) for h in heads]) for i in idx]
            hs = [h_in]
            for i in idx:
                hs.append(scales[i] * hs[i] + updates[i])
            h_in = hs[-1]
            yoffs = [_dot_nt(cgs[i], hs[i].astype(BF16)) for i in idx]
            for i in idx:
                yg_ref[rows[i], g * GROUP_X:(g + 1) * GROUP_X] = _gate_norm(
                    ydxs[i] + eacss[i] * yoffs[i], zxg[rows[i], 0:GROUP_X], ng_ref[g]).astype(BF16)
        h_ref[g] = h_in

        hist = hist_ref[g]
        conv_ref[0, :, g * GROUP_X:(g + 1) * GROUP_X] = hist[SUBLANES - (D_CONV - 1):, 0:GROUP_X]
        conv_ref[0, :, D_INNER + g * D_STATE:D_INNER + (g + 1) * D_STATE] = (
            hist[SUBLANES - (D_CONV - 1):, GROUP_X:GROUP_X + D_STATE])
        conv_ref[0, :, D_INNER + (SSM_GROUPS + g) * D_STATE:D_INNER + (SSM_GROUPS + g + 1) * D_STATE] = (
            hist[SUBLANES - (D_CONV - 1):, GROUP_X + D_STATE:GROUP_CONV])

        if g % 2 == 1:
            cols = slice((g - 1) * GROUP_X, (g + 1) * GROUP_X)
            y_ref[0] = (x_ref[0] if g == 1 else y_ref[0]) + _dot(yg_ref[:, cols], wout_ref[cols, :])

    @pl.when(j == pl.num_programs(1) - 1)
    def _():
        ssm_ref[0] = h_ref[...]


def _mamba_weight_specs(w):
    return [_const_spec(a.shape) for a in w]


def _mamba_prompt_call(x, gmix, layer, w, casts=(), tq=512):
    b, seq, _ = x.shape
    nj = seq // tq
    xspec = pl.BlockSpec((1, tq, D_MODEL), lambda i, j: (i, j, 0))
    n_chunks = tq // Q
    cast2d = [a.reshape(-1, a.shape[-1]) for a in casts]
    cast_specs = []
    for a in cast2d:
        rows = a.shape[0] // (b * nj)
        assert rows * b * nj == a.shape[0] and rows % (2 * SUBLANES) == 0
        cast_specs.append(pl.BlockSpec((rows, a.shape[1]), lambda i, j: (i * nj + j, 0)))
    outs = pl.pallas_call(
        functools.partial(_mamba_prompt_body, tq=tq, n_cast=len(casts)),
        grid=(b, nj),
        in_specs=[xspec, _layer_spec(gmix, layer)] + _mamba_weight_specs(w) + cast_specs,
        out_specs=[xspec,
                   pl.BlockSpec((1, SSM_GROUPS, GROUP_X, D_STATE), lambda i, j: (i, 0, 0, 0)),
                   pl.BlockSpec((1, D_CONV - 1, CONV_DIM), lambda i, j: (i, 0, 0))] + cast_specs,
        out_shape=[jax.ShapeDtypeStruct(x.shape, F32),
                   jax.ShapeDtypeStruct((b, SSM_GROUPS, GROUP_X, D_STATE), F32),
                   jax.ShapeDtypeStruct((b, D_CONV - 1, CONV_DIM), F32)]
        + [jax.ShapeDtypeStruct(a.shape, BF16) for a in cast2d],
        scratch_shapes=[
            pltpu.VMEM((tq, D_MODEL), BF16),
            pltpu.VMEM((tq, LANES), F32),
            pltpu.VMEM((tq, LANES), F32),
            pltpu.VMEM((n_chunks, 3, LANES, Q), F32),
            pltpu.VMEM((2, tq, GROUP_IN), F32),
            pltpu.VMEM((2, tq + SUBLANES, GROUP_CONV), F32),
            pltpu.VMEM((2, tq, GROUP_CONV), F32),
            pltpu.VMEM((SSM_GROUPS, SUBLANES, GROUP_CONV), F32),
            pltpu.VMEM((SSM_GROUPS, GROUP_X, D_STATE), F32),
            pltpu.VMEM((tq, D_INNER), BF16),
        ],
        compiler_params=_params(2),
        name="mamba_prompt",
    )(x, gmix, *w, *cast2d)
    return list(outs[:3]) + [o.reshape(a.shape) for o, a in zip(outs[3:], casts)]


def _mamba_sample_body(x_ref, gmix_ref, wg_ref, wdt_ref, cw_ref, cb_ref, dtb_ref, alog_ref, dsk_ref, ng_ref, wout_ref,
                       cx_ref, cbm_ref, ccm_ref, sst_ref, gx_ref, wq_ref,
                       y_ref, q_ref, ssm_ref, ox_ref, obm_ref, ocm_ref,
                       xn_ref, dt_ref, acs_ref, rowsT_ref, colb_ref, scale_ref, zxg_ref, xpad_ref, yg_ref, *, seq, gp):
    s = pl.program_id(1)
    nb = Q // seq
    tail = SUBLANES - (D_CONV - 1)
    mask = _causal_mask(seq)
    head_masks = _head_lane_masks()

    @pl.when(s == 0)
    def _():
        def seg_last(v):
            v3 = v.reshape(nb, seq, v.shape[-1])
            return jnp.broadcast_to(v3[:, seq - 1:seq, :], v3.shape).reshape(v.shape)

        xn_ref[...] = _rms(x_ref[...], gmix_ref[...]).astype(BF16)
        _dt_stage(xn_ref, wdt_ref, dtb_ref, alog_ref, dt_ref, acs_ref, rowsT_ref, mask, 1, seg_last)
        acs = acs_ref[...]
        for h in range(SSM_HEADS):
            colb_ref[h] = jnp.broadcast_to(acs[:, h:h + 1], (Q, LANES))
        acsT = rowsT_ref[0, ROW_ACS]
        for bi in range(nb):
            scale_ref[bi] = jnp.broadcast_to(jnp.exp2(acsT[:, (bi + 1) * seq - 1:(bi + 1) * seq]), (LANES, LANES))

    pair_b = _div_pow2(lax.broadcasted_iota(jnp.int32, (2 * seq, GROUP_X), 0), seq)
    row_b = _div_pow2(lax.broadcasted_iota(jnp.int32, (Q, D_STATE), 0), seq)
    row_masks = [jnp.where(row_b == bi, 1.0, 0.0).astype(BF16) for bi in range(nb)]
    gis = range(gp)
    gs = [s * gp + gi for gi in gis]
    h0s = [g * HEADS_PER_GROUP for g in gs]
    parts = ((cx_ref, ox_ref, GROUP_X, slice(0, GROUP_X)),
             (cbm_ref, obm_ref, D_STATE, slice(GROUP_X, GROUP_X + D_STATE)),
             (ccm_ref, ocm_ref, D_STATE, slice(GROUP_X + D_STATE, GROUP_CONV)))
    for gi in gis:
        zxg_ref[gi] = _dot(xn_ref[...], wg_ref[gs[gi]])
    for gi in gis:
        for src, _, w, dst in parts:
            xpad_ref[gi, :, tail:SUBLANES, dst] = src[:, :, gi * w:(gi + 1) * w]
        xpad_ref[gi, :, SUBLANES:SUBLANES + seq, :] = zxg_ref[gi, :, GROUP_X:GROUP_IN].reshape(nb, seq, GROUP_CONV)
    acts = [_conv_rows(lambda k: xpad_ref[gi, :, pl.ds(tail + k, seq), :].reshape(Q, GROUP_CONV),
                       cw_ref, cb_ref, gs[gi]) for gi in gis]
    for gi in gis:
        for _, out, w, dst in parts:
            out[:, :, gi * w:(gi + 1) * w] = xpad_ref[gi, :, pl.ds(tail + seq, D_CONV - 1), dst]
    bgs = [a[:, GROUP_X:GROUP_X + D_STATE].astype(BF16) for a in acts]
    cgs = [a[:, GROUP_X + D_STATE:GROUP_CONV].astype(BF16) for a in acts]
    ydxs, eacss, xddTs = _ssd_group(
        [a[:, 0:GROUP_X] for a in acts], bgs, cgs,
        [[colb_ref[h0 + r] for r in range(HEADS_PER_GROUP)] for h0 in h0s],
        lambda i, slot, r: rowsT_ref[0, slot, pl.ds(h0s[i] + r, 1), :], [dsk_ref[g] for g in gs], mask, head_masks)
    yoffs = [jnp.concatenate(
        [jnp.where(pair_b == 0,
                   _dot_nt(cgs[gi][2 * p * seq:2 * (p + 1) * seq, :], sst_ref[2 * p, gi].astype(BF16)),
                   _dot_nt(cgs[gi][2 * p * seq:2 * (p + 1) * seq, :], sst_ref[2 * p + 1, gi].astype(BF16)))
         for p in range(nb // 2)], axis=0) for gi in gis]
    updates = [_dot(xddTs[gi].astype(BF16), jnp.concatenate([bgs[gi] * row_masks[bi] for bi in range(nb)], axis=1))
               for gi in gis]
    for gi in gis:
        for bi in range(nb):
            scale = _head_rows([scale_ref[bi, pl.ds(h0s[gi] + r, 1), :] for r in range(HEADS_PER_GROUP)])
            ssm_ref[bi, gi] = scale * sst_ref[bi, gi] + updates[gi][:, bi * D_STATE:(bi + 1) * D_STATE]
    for gi in gis:
        yg_ref[gs[gi]] = _gate_norm(ydxs[gi] + eacss[gi] * yoffs[gi], zxg_ref[gi, :, 0:GROUP_X],
                                    ng_ref[gs[gi]]).astype(BF16)

    @pl.when(s == pl.num_programs(1) - 1)
    def _():
        x1 = x_ref[...]
        for g in range(SSM_GROUPS):
            x1 = x1 + _dot(yg_ref[g], wout_ref[g * GROUP_X:(g + 1) * GROUP_X, :])
        y_ref[...] = x1
        _store_pieces(q_ref, _dot(_rms(x1, gx_ref[...]).astype(BF16), wq_ref[...]))


def _mamba_sample_call(x2d, gmix, w, conv_state, ssm_state, gx, wq, layer, seq, gp=2):
    n = x2d.shape[0]
    batch = n // seq
    nb = Q // seq
    bc_cols = SSM_GROUPS * D_STATE
    xspec = pl.BlockSpec((Q, D_MODEL), lambda i, s: (i, 0))
    stspec = pl.BlockSpec((nb, gp, GROUP_X, D_STATE), lambda i, s: (i, s, 0, 0))

    def cols(width, first):
        assert first % (gp * width) == 0
        return pl.BlockSpec((nb, D_CONV - 1, gp * width), lambda i, s: (i, 0, first // (gp * width) + s))

    conv_in = [cols(GROUP_X, 0), cols(D_STATE, D_INNER), cols(D_STATE, D_INNER + bc_cols)]
    conv_out = [cols(GROUP_X, 0), cols(D_STATE, 0), cols(D_STATE, 0)]
    y, q, ssm, ox, obm, ocm = pl.pallas_call(
        functools.partial(_mamba_sample_body, seq=seq, gp=gp),
        grid=(n // Q, SSM_GROUPS // gp),
        in_specs=[xspec, _layer_spec(gmix, layer)] + _mamba_weight_specs(w)
        + conv_in + [stspec, _layer_spec(gx, layer), _layer_spec(wq, layer)],
        out_specs=[xspec, _pieces_spec(Q, lambda i, s: (0, 0, i, 0)), stspec] + conv_out,
        out_shape=[jax.ShapeDtypeStruct(x2d.shape, F32), _pieces_shape(n),
                   jax.ShapeDtypeStruct((batch, SSM_GROUPS, GROUP_X, D_STATE), F32),
                   jax.ShapeDtypeStruct((batch, D_CONV - 1, D_INNER), F32),
                   jax.ShapeDtypeStruct((batch, D_CONV - 1, bc_cols), F32),
                   jax.ShapeDtypeStruct((batch, D_CONV - 1, bc_cols), F32)],
        scratch_shapes=[
            pltpu.VMEM((Q, D_MODEL), BF16),
            pltpu.VMEM((Q, LANES), F32),
            pltpu.VMEM((Q, LANES), F32),
            pltpu.VMEM((1, 3, LANES, Q), F32),
            pltpu.VMEM((SSM_HEADS, Q, LANES), F32),
            pltpu.VMEM((nb, LANES, LANES), F32),
            pltpu.VMEM((gp, Q, GROUP_IN), F32),
            pltpu.VMEM((gp, nb, SUBLANES + seq, GROUP_CONV), F32),
            pltpu.VMEM((SSM_GROUPS, Q, GROUP_X), BF16),
        ],
        compiler_params=_params(2),
        name="mamba_sample",
    )(x2d, gmix, *w, conv_state, conv_state, conv_state, ssm_state, gx, wq)
    return y, q, ssm, jnp.concatenate([ox, obm, ocm], axis=-1)


def _pool_sample_body(x_ref, st_ref, gmix_ref, wpool_ref, pscale_ref, gx_ref, wq_ref,
                      y_ref, q_ref, pool_ref, ext_ref, *, seq, pos0):
    nb = x_ref.shape[0]
    n = nb * seq
    x = x_ref[...].reshape(n, D_MODEL)
    u = _rms(x, gmix_ref[...])
    hist0 = 2 * SUBLANES - POOL_BUF
    ext_ref[:, pl.ds(hist0, POOL_BUF), :] = st_ref[...]
    ext_ref[:, 2 * SUBLANES:2 * SUBLANES + seq, :] = u.reshape(nb, seq, D_MODEL)
    pos = pos0 + lax.broadcasted_iota(jnp.int32, (nb, seq, 1), 1).reshape(n, 1)
    pooled = _pool_windows(
        lambda k, cols: ext_ref[:, pl.ds(2 * SUBLANES - k, seq), cols].reshape(n, POOL_GROUP_DIM), u, pos, n)
    x1 = x + _pool_mix(pooled, wpool_ref, pscale_ref[...])
    pool_ref[...] = ext_ref[:, pl.ds(hist0 + seq, POOL_BUF), :]
    y_ref[...] = x1.reshape(nb, seq, D_MODEL)
    _store_pieces(q_ref, _dot(_rms(x1, gx_ref[...]).astype(BF16), wq_ref[...]))


def _pool_sample_call(x, pool_state, gmix, wpool, pscale, gx, wq, layer, pos0, nb=16):
    batch, seq, _ = x.shape
    xspec = pl.BlockSpec((nb, seq, D_MODEL), lambda i: (i, 0, 0))
    pspec = pl.BlockSpec((nb, POOL_BUF, D_MODEL), lambda i: (i, 0, 0))
    return pl.pallas_call(
        functools.partial(_pool_sample_body, seq=seq, pos0=pos0),
        grid=(batch // nb,),
        in_specs=[xspec, pspec, _layer_spec(gmix, layer), _layer_spec(wpool, layer // 2),
                  _layer_spec(pscale, layer // 2), _layer_spec(gx, layer), _layer_spec(wq, layer)],
        out_specs=[xspec, _pieces_spec(nb * seq, lambda i: (0, 0, i, 0)), pspec],
        out_shape=[jax.ShapeDtypeStruct(x.shape, F32), _pieces_shape(batch * seq),
                   jax.ShapeDtypeStruct(pool_state.shape, F32)],
        scratch_shapes=[pltpu.VMEM((nb, 2 * SUBLANES + seq, D_MODEL), F32)],
        compiler_params=_params(1),
        name="pool_sample",
    )(x, pool_state, gmix, wpool, pscale, gx, wq)


HEAD_HALVES = XATTN_HEAD_DIM // LANES
KV_ROWS = N_MEM * XATTN_HEADS * HEAD_HALVES


def _attn_sample_stage(q_ref, k_ref, v_ref, o_ref):
    nb = k_ref.shape[1]
    seq = q_ref.shape[2] // nb
    assert seq == SUBLANES
    pieces = [(j, h) for j in range(HEAD_HALVES) for h in range(XATTN_HEADS)]
    half = XATTN_HEADS * seq
    grp = XATTN_HEADS * HEAD_HALVES
    lane = lax.broadcasted_iota(jnp.int32, (half, KV_ROWS), 1)
    rowi = lax.broadcasted_iota(jnp.int32, (half, KV_ROWS), 0)
    valid = (lane & (grp - 1)) == _div_pow2(rowi, seq)
    qs = [jnp.concatenate([q_ref[j, h, bi * seq:(bi + 1) * seq, :] for j, h in pieces], axis=0) for bi in range(nb)]
    prods = [_dot_nt((qs[bi] * (XATTN_HEAD_DIM ** -0.5)).astype(BF16), k_ref[0, bi].astype(BF16))
             for bi in range(nb)]
    ss = [jnp.where(valid, pr[0:half] + pltpu.roll(pr[half:], KV_ROWS - XATTN_HEADS, axis=1), -jnp.inf)
          for pr in prods]
    es = [jnp.exp(s - jnp.max(s, axis=-1, keepdims=True)) for s in ss]
    ps = [e * (1.0 / jnp.sum(e, axis=-1, keepdims=True)) for e in es]
    pes = [jnp.concatenate([p, pltpu.roll(p, XATTN_HEADS, axis=1)], axis=0).astype(BF16) for p in ps]
    for bi in range(nb):
        o = _dot(pes[bi], v_ref[0, bi].astype(BF16))
        for n, (j, h) in enumerate(pieces):
            o_ref[j, h, bi * seq:(bi + 1) * seq, :] = o[n * seq:(n + 1) * seq, :]


def _cache_rows(cache):
    d, b = cache.shape[:2]
    c = cache.reshape(d, b, N_MEM, XATTN_HEADS, HEAD_HALVES, LANES)
    return c.transpose(0, 1, 2, 4, 3, 5).reshape(d, b, KV_ROWS, LANES)


def _store_pieces(ref, v):
    for h in range(XATTN_HEADS):
        for j in range(HEAD_HALVES):
            c0 = (h * HEAD_HALVES + j) * LANES
            ref[j, h] = v[:, c0:c0 + LANES]


def _load_pieces(ref):
    return jnp.concatenate([ref[j, h] for h in range(XATTN_HEADS) for j in range(HEAD_HALVES)], axis=-1)


def _pieces_spec(rows, index_map):
    return pl.BlockSpec((HEAD_HALVES, XATTN_HEADS, rows, LANES), index_map)


def _pieces_shape(n_rows):
    return jax.ShapeDtypeStruct((HEAD_HALVES, XATTN_HEADS, n_rows, LANES), F32)


def _out_mlp_body(*refs, with_final):
    it = iter(refs)
    x_ref, o_ref, wo_ref, gm_ref, wup_ref, wdn_ref = (next(it) for _ in range(6))
    if with_final:
        gf_ref = next(it)
    y_ref = next(it)
    x = x_ref[...] + _dot(_load_pieces(o_ref).astype(BF16), wo_ref[...])
    x = _mlp(x, gm_ref[...], wup_ref, wdn_ref)
    if with_final:
        x = _rms(x, gf_ref[...])
    y_ref[...] = x


def _out_mlp_call(x2d, o_pieces, layer, wo, gm, wup, wdn, gfinal=None, tm=512):
    n = x2d.shape[0]
    tm = min(tm, n)
    row = pl.BlockSpec((tm, D_MODEL), lambda i: (i, 0))
    args = [x2d, o_pieces, wo, gm, wup, wdn]
    specs = [row, _pieces_spec(tm, lambda i: (0, 0, i, 0)), _layer_spec(wo, layer), _layer_spec(gm, layer),
             _layer_spec(wup, layer), _layer_spec(wdn, layer)]
    if gfinal is not None:
        args.append(gfinal)
        specs.append(_const_spec(gfinal.shape))
    return pl.pallas_call(
        functools.partial(_out_mlp_body, with_final=gfinal is not None),
        grid=(n // tm,),
        in_specs=specs,
        out_specs=row,
        out_shape=jax.ShapeDtypeStruct(x2d.shape, F32),
        compiler_params=_params(1),
        name="out_mlp",
    )(*args)


def _mamba_weights(wg, wdt, conv_w, conv_b, dt_bias, a_log, d_skip, norm_gated, w_out_bf16):
    pad = LANES - SSM_HEADS

    def grp(a, width):
        return a.reshape(a.shape[0], SSM_GROUPS, width)

    def conv_grp(a):
        return jnp.concatenate([grp(a[:, :D_INNER], GROUP_X), grp(a[:, D_INNER:D_INNER + SSM_GROUPS * D_STATE], D_STATE),
                                grp(a[:, D_INNER + SSM_GROUPS * D_STATE:], D_STATE)], axis=-1).transpose(1, 0, 2)

    cw = conv_grp(conv_w)
    cb = conv_grp(conv_b[None, :])
    dtb = jnp.pad(dt_bias, (0, pad))[None, :]
    alog = jnp.pad(a_log, (0, pad))[None, :]
    dsk = jnp.repeat(d_skip, SSM_HEADDIM).reshape(SSM_GROUPS, 1, GROUP_X)
    return (wg, wdt, cw, cb, dtb, alog, dsk, norm_gated.reshape(SSM_GROUPS, 1, GROUP_X), w_out_bf16)


def kernel(x_prompt, x_sample, cache_mem_k, cache_mem_v, state_ssm, state_conv, state_pool, mem_prompt, norm_mix, norm_xattn, norm_mem, norm_mlp, norm_final, w_in, conv_w, conv_b, dt_bias, a_log, d_skip, norm_gated, w_out, w_pool, pool_scale, w_xq, w_xk, w_xv, w_xo, w_up, w_down):
    bp, seq_p, _ = x_prompt.shape
    bs, seq_s, _ = x_sample.shape
    depth = w_xq.shape[0]
    per_layer = lambda a: a[:, None, :]

    wpool = w_pool.astype(BF16)
    g_mix, g_x, g_mlp, p_scale = (per_layer(a) for a in (norm_mix, norm_xattn, norm_mlp, pool_scale))
    g_final = norm_final[None, :]

    k_p, v_p, kb, vb, wg, wdt, wout = _kv_call(mem_prompt.reshape(bp * N_MEM, D_MODEL), per_layer(norm_mem),
                                               w_xk, w_xv, w_in[0].T, w_out[0])
    mw = _mamba_weights(wg, wdt, conv_w[0], conv_b[0], dt_bias[0], a_log[0], d_skip[0], norm_gated[0], wout)
    kv_shape = (depth, bp, N_MEM, XATTN_HEADS, XATTN_HEAD_DIM)
    kb = kb.reshape(depth, bp, N_MEM, D_MODEL)
    vb = vb.reshape(depth, bp, N_MEM, D_MODEL)

    kc, vc = _cache_rows(cache_mem_k), _cache_rows(cache_mem_v)
    xp, ssm_p, conv_p, wq, wo, wup, wdn = _mamba_prompt_call(
        x_prompt, g_mix, 0, mw, casts=(w_xq, w_xo, w_up, w_down))
    xs = x_sample.reshape(bs * seq_s, D_MODEL)
    ssm_in = state_ssm[0].reshape(bs, SSM_GROUPS, GROUP_X, D_STATE)
    xs, q, ssm_s, conv_s = _mamba_sample_call(xs, g_mix, mw, state_conv[0], ssm_in, g_x, wq, 0, seq_s)
    xp, o = _prompt_layer_call(xp, 0, kb, vb, g_x, wq, wo, g_mlp, wup, wdn, sample=(q, kc, vc))
    xs = _out_mlp_call(xs, o, 0, wo, g_mlp, wup, wdn)
    xs, q, pool_s = _pool_sample_call(xs.reshape(bs, seq_s, D_MODEL), state_pool[0], g_mix, wpool, p_scale,
                                      g_x, wq, 1, PAST_LEN)
    y_prompt, pool_p, o = _prompt_layer_call(
        xp, 1, kb, vb, g_x, wq, wo, g_mlp, wup, wdn,
        pool=(g_mix, wpool, p_scale), gfinal=g_final, sample=(q, kc, vc))
    y_sample = _out_mlp_call(xs.reshape(bs * seq_s, D_MODEL), o, 1, wo, g_mlp, wup, wdn, gfinal=g_final)

    ssm_shape = (1, -1, SSM_HEADS, SSM_HEADDIM, D_STATE)
    return (y_prompt, y_sample.reshape(bs, seq_s, D_MODEL),
            k_p.reshape(kv_shape), v_p.reshape(kv_shape),
            ssm_p.reshape(ssm_shape), conv_p[None], pool_p[None],
            ssm_s.reshape(ssm_shape), conv_s[None], pool_s[None])
```

```python
import functools

import jax
import jax.numpy as jnp
from jax import lax
from jax.experimental import pallas as pl
from jax.experimental.pallas import tpu as pltpu

F32 = jnp.float32
BF16 = jnp.bfloat16

D_MODEL = 1024
D_INNER = 2048
SSM_HEADS = 32
SSM_HEADDIM = 64
SSM_GROUPS = 8
HEADS_PER_GROUP = 4
GROUP_X = HEADS_PER_GROUP * SSM_HEADDIM
D_STATE = 128
D_CONV = 4
CONV_DIM = D_INNER + 2 * SSM_GROUPS * D_STATE
GROUP_CONV = GROUP_X + 2 * D_STATE
GROUP_IN = GROUP_X + GROUP_CONV
POOL_WINDOWS = (2, 4, 8, 16)
POOL_GROUP_DIM = 256
POOL_BUF = 15
N_MEM = 256
XATTN_HEADS = 4
XATTN_HEAD_DIM = 256
D_FF = 4096
EPS = 1e-5
PAST_LEN = 16384
LOG2_E = 1.4426950408889634

LANES = 128
SUBLANES = 8
Q = 128
SSD_STAGE = 2
FF_CHUNK = 1024
VMEM_LIMIT = 56 * 1024 * 1024

_NT = (((1,), (1,)), ((), ()))


def _params(n_grid):
    return pltpu.CompilerParams(dimension_semantics=("arbitrary",) * n_grid, vmem_limit_bytes=VMEM_LIMIT)


def _const_spec(shape):
    nd = len(shape)
    return pl.BlockSpec(shape, lambda *_: (0,) * nd, pipeline_mode=pl.Buffered(1))


def _layer_spec(a, layer):
    nd = a.ndim
    return pl.BlockSpec((None,) + a.shape[1:], lambda *_: (layer,) + (0,) * (nd - 1), pipeline_mode=pl.Buffered(1))


def _dot(a, b):
    return jnp.dot(a, b, preferred_element_type=F32)


def _dot_nt(a, b):
    return lax.dot_general(a, b, _NT, preferred_element_type=F32)


def _rms(x, g):
    return x * lax.rsqrt(jnp.mean(x * x, axis=-1, keepdims=True) + EPS) * g


def _silu(x):
    h = 0.5 * x
    return h + h * jnp.tanh(h)


def _kv_body(mem_ref, g_ref, wk_ref, wv_ref, wz_ref, wx_ref, wb_ref, wc_ref, wt_ref, wo_ref,
             k_ref, v_ref, kb_ref, vb_ref, wg_ref, wdt_ref, wob_ref):
    @pl.when(jnp.logical_and(pl.program_id(0) == 0, pl.program_id(1) == 0))
    def _():
        pad = jnp.zeros((LANES - SSM_HEADS, D_MODEL), F32)
        wdt_ref[...] = jnp.concatenate([wt_ref[0:SSM_HEADS, :], pad], axis=0).T.astype(BF16)

    wg_ref[0, :, 0:GROUP_X] = wz_ref[...].T.astype(BF16)
    wg_ref[0, :, GROUP_X:2 * GROUP_X] = wx_ref[...].T.astype(BF16)
    wg_ref[0, :, 2 * GROUP_X:2 * GROUP_X + D_STATE] = wb_ref[...].T.astype(BF16)
    wg_ref[0, :, 2 * GROUP_X + D_STATE:GROUP_IN] = wc_ref[...].T.astype(BF16)
    wob_ref[...] = wo_ref[...].astype(BF16)
    mn = _rms(mem_ref[...], g_ref[0]).astype(BF16)
    k = _dot(mn, wk_ref[0].astype(BF16))
    v = _dot(mn, wv_ref[0].astype(BF16))
    for h in range(XATTN_HEADS):
        sl = slice(h * XATTN_HEAD_DIM, (h + 1) * XATTN_HEAD_DIM)
        k_ref[0, :, h, :] = k[:, sl]
        v_ref[0, :, h, :] = v[:, sl]
    kb_ref[0] = k.astype(BF16)
    vb_ref[0] = v.astype(BF16)


def _kv_call(mem, g_mem, wk, wv, w_in_t, w_out):
    n = mem.shape[0]
    depth = wk.shape[0]
    nt = SSM_GROUPS // depth
    tm = n // nt
    assert depth * nt == SSM_GROUPS and tm * nt == n and tm % SUBLANES == 0
    row = pl.BlockSpec((tm, D_MODEL), lambda l, t: (t, 0))
    wspec = pl.BlockSpec((1, D_MODEL, D_MODEL), lambda l, t: (l, 0, 0))
    ospec = pl.BlockSpec((1, tm, D_MODEL), lambda l, t: (l, t, 0))
    hspec = pl.BlockSpec((1, tm, XATTN_HEADS, XATTN_HEAD_DIM), lambda l, t: (l, t, 0, 0))

    def rows(height, first_block):
        return pl.BlockSpec((height, D_MODEL), lambda l, t: (first_block + l * nt + t, 0))

    x0 = D_INNER // GROUP_X
    b0 = 2 * D_INNER // D_STATE
    wo_spec = pl.BlockSpec((w_out.shape[0] // SSM_GROUPS, D_MODEL), lambda l, t: (l * nt + t, 0))
    dt0 = D_INNER + CONV_DIM
    assert dt0 % LANES == 0 and w_in_t.shape[0] - dt0 == SSM_HEADS
    dt_spec = pl.BlockSpec((LANES, D_MODEL), lambda l, t: (dt0 // LANES, 0))
    return pl.pallas_call(
        _kv_body,
        grid=(depth, nt),
        in_specs=[row, pl.BlockSpec((1, 1, D_MODEL), lambda l, t: (l, 0, 0)), wspec, wspec,
                  rows(GROUP_X, 0), rows(GROUP_X, x0), rows(D_STATE, b0), rows(D_STATE, b0 + SSM_GROUPS),
                  dt_spec, wo_spec],
        out_specs=[hspec, hspec, ospec, ospec,
                   pl.BlockSpec((1, D_MODEL, GROUP_IN), lambda l, t: (l * nt + t, 0, 0)),
                   pl.BlockSpec((D_MODEL, LANES), lambda l, t: (0, 0)), wo_spec],
        out_shape=[jax.ShapeDtypeStruct((depth, n, XATTN_HEADS, XATTN_HEAD_DIM), F32)] * 2
        + [jax.ShapeDtypeStruct((depth, n, D_MODEL), BF16)] * 2
        + [jax.ShapeDtypeStruct((SSM_GROUPS, D_MODEL, GROUP_IN), BF16),
           jax.ShapeDtypeStruct((D_MODEL, LANES), BF16), jax.ShapeDtypeStruct(w_out.shape, BF16)],
        compiler_params=_params(2),
        name="mem_kv",
    )(mem, g_mem, wk, wv, w_in_t, w_in_t, w_in_t, w_in_t, w_in_t, w_out)


def _attn_heads(q, k_of, v_of):
    heads = range(XATTN_HEADS)
    ss = [_dot_nt(q[:, h * XATTN_HEAD_DIM:(h + 1) * XATTN_HEAD_DIM].astype(BF16), k_of(h)) * (XATTN_HEAD_DIM ** -0.5)
          for h in heads]
    es = [jnp.exp(s - jnp.max(s, axis=-1, keepdims=True)) for s in ss]
    ps = [e * (1.0 / jnp.sum(e, axis=-1, keepdims=True)) for e in es]
    return jnp.concatenate([_dot(ps[h].astype(BF16), v_of(h)) for h in heads], axis=-1)


def _mlp(x, g, wup_ref, wdn_ref):
    hn = _rms(x, g).astype(BF16)
    acc = x
    for c in range(D_FF // FF_CHUNK):
        a = jnp.maximum(_dot(hn, wup_ref[:, c * FF_CHUNK:(c + 1) * FF_CHUNK]), 0.0)
        acc = acc + _dot((a * a).astype(BF16), wdn_ref[c * FF_CHUNK:(c + 1) * FF_CHUNK, :])
    return acc


def _pool_windows(load, u, pos, n):
    outs = []
    for gi, w in enumerate(POOL_WINDOWS):
        cols = slice(gi * POOL_GROUP_DIM, (gi + 1) * POOL_GROUP_DIM)
        acc = load(0, cols)
        for k in range(1, w):
            acc = acc + load(k, cols)
        cnt = jnp.minimum(pos + 1, w).astype(F32)
        outs.append(acc / cnt - u[:, cols])
    return outs


def _pool_windows_rolled(ext_ref, u, pos, n):
    outs = []
    for gi, w in enumerate(POOL_WINDOWS):
        assert w & (w - 1) == 0 and w <= 2 * SUBLANES
        cols = slice(gi * POOL_GROUP_DIM, (gi + 1) * POOL_GROUP_DIM)
        acc = ext_ref[0:n + 2 * SUBLANES, cols]
        span = 1
        while span < w:
            acc = acc + pltpu.roll(acc, span, axis=0)
            span *= 2
        cnt = jnp.minimum(pos + 1, w).astype(F32)
        outs.append(acc[2 * SUBLANES:, :] / cnt - u[:, cols])
    return outs


def _pool_mix(pooled, wpool_ref, scale):
    outs = [_dot(p.astype(BF16), wpool_ref[gi]) for gi, p in enumerate(pooled)]
    return jnp.concatenate(outs, axis=-1) * scale


def _prompt_layer_body(*refs, tq, with_pool, with_final, with_sample):
    it = iter(refs)
    x_ref = next(it)
    if with_pool:
        gmix_ref, wpool_ref, pscale_ref = next(it), next(it), next(it)
    gx_ref, wq_ref, kb_ref, vb_ref, wo_ref = next(it), next(it), next(it), next(it), next(it)
    gm_ref, wup_ref, wdn_ref = next(it), next(it), next(it)
    if with_final:
        gf_ref = next(it)
    if with_sample:
        qs_ref, ks_ref, vs_ref = next(it), next(it), next(it)
    y_ref = next(it)
    if with_pool:
        pool_ref = next(it)
    if with_sample:
        os_ref = next(it)
    if with_pool:
        ext_ref = next(it)

    x = x_ref[0]
    if with_pool:
        j = pl.program_id(1)

        @pl.when(j == 0)
        def _():
            ext_ref[0:2 * SUBLANES, :] = jnp.zeros((2 * SUBLANES, D_MODEL), F32)

        u = _rms(x, gmix_ref[...])
        ext_ref[pl.ds(2 * SUBLANES, tq), :] = u
        pos = j * tq + lax.broadcasted_iota(jnp.int32, (tq, 1), 0)
        pooled = _pool_windows_rolled(ext_ref, u, pos, tq)
        x = x + _pool_mix(pooled, wpool_ref, pscale_ref[...])
        pool_ref[0] = ext_ref[pl.ds(tq + 1, POOL_BUF), :]
        ext_ref[0:2 * SUBLANES, :] = ext_ref[pl.ds(tq, 2 * SUBLANES), :]

    if with_sample:
        _attn_sample_stage(qs_ref, ks_ref, vs_ref, os_ref)

    q = _dot(_rms(x, gx_ref[...]).astype(BF16), wq_ref[...])
    o = _attn_heads(
        q,
        lambda h: kb_ref[0, 0, :, h * XATTN_HEAD_DIM:(h + 1) * XATTN_HEAD_DIM],
        lambda h: vb_ref[0, 0, :, h * XATTN_HEAD_DIM:(h + 1) * XATTN_HEAD_DIM],
    )
    x = x + _dot(o.astype(BF16), wo_ref[...])
    x = _mlp(x, gm_ref[...], wup_ref, wdn_ref)
    if with_final:
        x = _rms(x, gf_ref[...])
    y_ref[0] = x


def _prompt_layer_call(x, layer, kb, vb, gx, wq, wo, gm, wup, wdn, pool=None, gfinal=None, sample=None, tq=512):
    b, seq, _ = x.shape
    with_pool, with_final, with_sample = pool is not None, gfinal is not None, sample is not None
    nj = seq // tq
    xspec = pl.BlockSpec((1, tq, D_MODEL), lambda i, j: (i, j, 0))
    kvspec = pl.BlockSpec((1, 1, N_MEM, D_MODEL), lambda i, j: (layer, i, 0, 0))
    args, specs = [x], [xspec]
    if with_pool:
        gmix, wpool, pscale = pool
        args += [gmix, wpool, pscale]
        specs += [_layer_spec(gmix, layer), _layer_spec(wpool, layer // 2), _layer_spec(pscale, layer // 2)]
    args += [gx, wq, kb, vb, wo, gm, wup, wdn]
    specs += [_layer_spec(gx, layer), _layer_spec(wq, layer), kvspec, kvspec, _layer_spec(wo, layer),
              _layer_spec(gm, layer), _layer_spec(wup, layer), _layer_spec(wdn, layer)]
    if with_final:
        args.append(gfinal)
        specs.append(_const_spec(gfinal.shape))
    out_shape = [jax.ShapeDtypeStruct(x.shape, F32)]
    out_specs = [xspec]
    scratch = []
    if with_pool:
        out_shape.append(jax.ShapeDtypeStruct((b, POOL_BUF, D_MODEL), F32))
        out_specs.append(pl.BlockSpec((1, POOL_BUF, D_MODEL), lambda i, j: (i, 0, 0)))
        scratch.append(pltpu.VMEM((tq + 2 * SUBLANES, D_MODEL), F32))
    if with_sample:
        q_pieces, k_rows, v_rows = sample
        n_batch = k_rows.shape[1]
        nb = n_batch // (b * nj)
        rows = q_pieces.shape[2] // (b * nj)
        assert nb * b * nj == n_batch and rows == nb * SUBLANES
        qspec = pl.BlockSpec(q_pieces.shape[:2] + (rows, LANES), lambda i, j: (0, 0, i * nj + j, 0))
        cspec = pl.BlockSpec((1, nb, KV_ROWS, LANES), lambda i, j: (layer, i * nj + j, 0, 0))
        args += [q_pieces, k_rows, v_rows]
        specs += [qspec, cspec, cspec]
        out_shape.append(jax.ShapeDtypeStruct(q_pieces.shape, F32))
        out_specs.append(qspec)
    outs = pl.pallas_call(
        functools.partial(_prompt_layer_body, tq=tq, with_pool=with_pool, with_final=with_final,
                          with_sample=with_sample),
        grid=(b, nj),
        in_specs=specs,
        out_specs=out_specs,
        out_shape=out_shape,
        scratch_shapes=scratch,
        compiler_params=_params(2),
        name="prompt_layer_pool" if with_pool else "prompt_layer",
    )(*args)
    return outs if len(outs) > 1 else outs[0]


ROW_ACS, ROW_DECAY, ROW_END = 0, 1, 2


def _dt_stage(xn_ref, wdt_ref, dtb_ref, alog_ref, dt_ref, acs_ref, rowsT_ref, mask, n_chunks, seg_last):
    z = _dot(xn_ref[...], wdt_ref[...]) + dtb_ref[...]
    dt_ref[...] = jnp.maximum(z, 0.0) + jnp.log1p(jnp.exp(-jnp.abs(z)))
    a = -jnp.exp(alog_ref[...]) * LOG2_E
    tri = mask.astype(F32)
    for c in range(n_chunks):
        rows = slice(c * Q, (c + 1) * Q)
        dt_c = dt_ref[rows, :]
        acs = jnp.dot(tri, dt_c * a, precision=lax.Precision.HIGHEST, preferred_element_type=F32)
        acs_ref[rows, :] = acs
        acsT, dtT = acs.T, dt_c.T
        rowsT_ref[c, ROW_ACS] = acsT
        rowsT_ref[c, ROW_DECAY] = acsT - jnp.log2(dtT)
        rowsT_ref[c, ROW_END] = dtT * jnp.exp2(seg_last(acs).T - acsT)


def _conv_rows(load_shifted, cw_ref, cb_ref, g):
    acc = cb_ref[g]
    for k in range(D_CONV):
        acc = acc + cw_ref[g, k:k + 1, :] * load_shifted(k)
    return _silu(acc)


def _conv_rolled(ext, cw_ref, cb_ref, g):
    acc = cb_ref[g] + cw_ref[g, D_CONV - 1:D_CONV, :] * ext
    for s in range(1, D_CONV):
        acc = acc + cw_ref[g, D_CONV - 1 - s:D_CONV - s, :] * pltpu.roll(ext, s, axis=0)
    return _silu(acc[SUBLANES:, :])


def _expand_heads(cols):
    half = lax.broadcasted_iota(jnp.int32, (Q, LANES), 1) < SSM_HEADDIM
    return jnp.concatenate([jnp.where(half, cols[0], cols[1]), jnp.where(half, cols[2], cols[3])], axis=1)


def _head_lane_masks():
    lane_head = _div_pow2(lax.broadcasted_iota(jnp.int32, (Q, GROUP_X), 1), SSM_HEADDIM)
    return [jnp.where(lane_head == r, 1.0, 0.0).astype(BF16) for r in range(HEADS_PER_GROUP)]


def _ssd_group(xgs, bgs, cgs, acs_cols, row_of, dskips, mask, head_masks):
    n = len(xgs)
    heads = range(HEADS_PER_GROUP)
    cbs = [_dot_nt(cg, bg) for cg, bg in zip(cgs, bgs)]
    xbs = [xg.astype(BF16) for xg in xgs]
    lcats = [jnp.concatenate(
        [(cbs[i] * jnp.exp2(jnp.where(mask, acs_cols[i][r] - row_of(i, ROW_DECAY, r), -jnp.inf))).astype(BF16)
         for r in heads], axis=1) for i in range(n)]
    xblks = [jnp.concatenate([xb * hm for hm in head_masks], axis=0) for xb in xbs]
    ydxs = [_dot(l, xb) + dsk * xg for l, xb, xg, dsk in zip(lcats, xblks, xgs, dskips)]
    xTs = [xg.T for xg in xgs]
    xddTs = [jnp.concatenate(
        [xTs[i][r * SSM_HEADDIM:(r + 1) * SSM_HEADDIM, :] * row_of(i, ROW_END, r) for r in heads], axis=0)
        for i in range(n)]
    eacss = [jnp.exp2(_expand_heads(cols)) for cols in acs_cols]
    return ydxs, eacss, xddTs


def _gate_norm(y, z, ng):
    yg = y * _silu(z)
    return yg * lax.rsqrt(jnp.mean(yg * yg, axis=-1, keepdims=True) + EPS) * ng


def _div_pow2(v, d):
    assert d & (d - 1) == 0
    return lax.shift_right_logical(v, d.bit_length() - 1)


def _causal_mask(block):
    li = lax.broadcasted_iota(jnp.int32, (Q, Q), 0)
    si = lax.broadcasted_iota(jnp.int32, (Q, Q), 1)
    m = li >= si
    if block is not None:
        m = jnp.logical_and(m, _div_pow2(li, block) == _div_pow2(si, block))
    return m


def _head_rows(vals):
    return jnp.concatenate([jnp.broadcast_to(v, (SSM_HEADDIM, LANES)) for v in vals], axis=0)


def _mamba_prompt_body(x_ref, gmix_ref, wg_ref, wdt_ref, cw_ref, cb_ref, dtb_ref, alog_ref, dsk_ref, ng_ref, wout_ref,
                       *rest, tq, n_cast):
    cast_in, rest = rest[:n_cast], rest[n_cast:]
    y_ref, ssm_ref, conv_ref = rest[:3]
    cast_out, rest = rest[3:3 + n_cast], rest[3 + n_cast:]
    xn_ref, dt_ref, acs_ref, rowsT_ref, zxg_ref, xpad_ref, act_ref, hist_ref, h_ref, yg_ref = rest
    j = pl.program_id(1)
    n_chunks = tq // Q

    for src, dst in zip(cast_in, cast_out):
        dst[...] = src[...].astype(BF16)

    @pl.when(j == 0)
    def _():
        hist_ref[...] = jnp.zeros_like(hist_ref)
        h_ref[...] = jnp.zeros_like(h_ref)

    mask = _causal_mask(None)
    head_masks = _head_lane_masks()
    xn_ref[...] = _rms(x_ref[0], gmix_ref[...]).astype(BF16)

    def in_proj(g):
        zxg_ref[g % 2] = _dot(xn_ref[...], wg_ref[g])

    in_proj(0)
    _dt_stage(xn_ref, wdt_ref, dtb_ref, alog_ref, dt_ref, acs_ref, rowsT_ref, mask, n_chunks,
              lambda v: jnp.broadcast_to(v[Q - 1:Q, :], v.shape))
    for g in range(SSM_GROUPS):
        if g + 1 < SSM_GROUPS:
            in_proj(g + 1)
        zxg, xpad, act = zxg_ref.at[g % 2], xpad_ref.at[g % 2], act_ref.at[g % 2]
        xpad[0:SUBLANES, :] = hist_ref[g]
        xpad[pl.ds(SUBLANES, tq), :] = zxg[:, GROUP_X:GROUP_IN]

        for c in range(n_chunks):
            act[c * Q:(c + 1) * Q, :] = _conv_rolled(xpad[c * Q:(c + 1) * Q + SUBLANES, :], cw_ref, cb_ref, g)
        hist_ref[g] = xpad[pl.ds(tq, SUBLANES), :]

        h_in = h_ref[g]
        for c0 in range(0, n_chunks, SSD_STAGE):
            chunks = list(range(c0, min(c0 + SSD_STAGE, n_chunks)))
            idx = range(len(chunks))
            rows = [slice(c * Q, (c + 1) * Q) for c in chunks]
            bgs = [act[r, GROUP_X:GROUP_X + D_STATE].astype(BF16) for r in rows]
            cgs = [act[r, GROUP_X + D_STATE:GROUP_CONV].astype(BF16) for r in rows]
            rowsT_cs = [rowsT_ref.at[c] for c in chunks]
            heads = range(HEADS_PER_GROUP * g, HEADS_PER_GROUP * (g + 1))
            acs_cols = [[jnp.broadcast_to(acs_ref[r, h:h + 1], (Q, LANES)) for h in heads] for r in rows]
            ydxs, eacss, xddTs = _ssd_group(
                [act[r, 0:GROUP_X] for r in rows], bgs, cgs, acs_cols,
                lambda i, slot, r: rowsT_cs[i][slot, heads[r]:heads[r] + 1, :], [dsk_ref[g]] * len(rows),
                mask, head_masks)
            updates = [_dot(xddTs[i].astype(BF16), bgs[i]) for i in idx]
            scales = [_head_rows([jnp.exp2(rowsT_cs[i][ROW_ACS, h:h + 1, Q - 1:Q]) for h in heads]) for i in idx]
            hs = [h_in]
            for i in idx:
                hs.append(scales[i] * hs[i] + updates[i])
            h_in = hs[-1]
            yoffs = [_dot_nt(cgs[i], hs[i].astype(BF16)) for i in idx]
            for i in idx:
                yg_ref[rows[i], g * GROUP_X:(g + 1) * GROUP_X] = _gate_norm(
                    ydxs[i] + eacss[i] * yoffs[i], zxg[rows[i], 0:GROUP_X], ng_ref[g]).astype(BF16)
        h_ref[g] = h_in

        hist = hist_ref[g]
        conv_ref[0, :, g * GROUP_X:(g + 1) * GROUP_X] = hist[SUBLANES - (D_CONV - 1):, 0:GROUP_X]
        conv_ref[0, :, D_INNER + g * D_STATE:D_INNER + (g + 1) * D_STATE] = (
            hist[SUBLANES - (D_CONV - 1):, GROUP_X:GROUP_X + D_STATE])
        conv_ref[0, :, D_INNER + (SSM_GROUPS + g) * D_STATE:D_INNER + (SSM_GROUPS + g + 1) * D_STATE] = (
            hist[SUBLANES - (D_CONV - 1):, GROUP_X + D_STATE:GROUP_CONV])

        if g % 2 == 1:
            cols = slice((g - 1) * GROUP_X, (g + 1) * GROUP_X)
            y_ref[0] = (x_ref[0] if g == 1 else y_ref[0]) + _dot(yg_ref[:, cols], wout_ref[cols, :])

    @pl.when(j == pl.num_programs(1) - 1)
    def _():
        ssm_ref[0] = h_ref[...]


def _mamba_weight_specs(w):
    return [_const_spec(a.shape) for a in w]


def _mamba_prompt_call(x, gmix, layer, w, casts=(), tq=512):
    b, seq, _ = x.shape
    nj = seq // tq
    xspec = pl.BlockSpec((1, tq, D_MODEL), lambda i, j: (i, j, 0))
    n_chunks = tq // Q
    cast2d = [a.reshape(-1, a.shape[-1]) for a in casts]
    cast_specs = []
    for a in cast2d:
        rows = a.shape[0] // (b * nj)
        assert rows * b * nj == a.shape[0] and rows % (2 * SUBLANES) == 0
        cast_specs.append(pl.BlockSpec((rows, a.shape[1]), lambda i, j: (i * nj + j, 0)))
    outs = pl.pallas_call(
        functools.partial(_mamba_prompt_body, tq=tq, n_cast=len(casts)),
        grid=(b, nj),
        in_specs=[xspec, _layer_spec(gmix, layer)] + _mamba_weight_specs(w) + cast_specs,
        out_specs=[xspec,
                   pl.BlockSpec((1, SSM_GROUPS, GROUP_X, D_STATE), lambda i, j: (i, 0, 0, 0)),
                   pl.BlockSpec((1, D_CONV - 1, CONV_DIM), lambda i, j: (i, 0, 0))] + cast_specs,
        out_shape=[jax.ShapeDtypeStruct(x.shape, F32),
                   jax.ShapeDtypeStruct((b, SSM_GROUPS, GROUP_X, D_STATE), F32),
                   jax.ShapeDtypeStruct((b, D_CONV - 1, CONV_DIM), F32)]
        + [jax.ShapeDtypeStruct(a.shape, BF16) for a in cast2d],
        scratch_shapes=[
            pltpu.VMEM((tq, D_MODEL), BF16),
            pltpu.VMEM((tq, LANES), F32),
            pltpu.VMEM((tq, LANES), F32),
            pltpu.VMEM((n_chunks, 3, LANES, Q), F32),
            pltpu.VMEM((2, tq, GROUP_IN), F32),
            pltpu.VMEM((2, tq + SUBLANES, GROUP_CONV), F32),
            pltpu.VMEM((2, tq, GROUP_CONV), F32),
            pltpu.VMEM((SSM_GROUPS, SUBLANES, GROUP_CONV), F32),
            pltpu.VMEM((SSM_GROUPS, GROUP_X, D_STATE), F32),
            pltpu.VMEM((tq, D_INNER), BF16),
        ],
        compiler_params=_params(2),
        name="mamba_prompt",
    )(x, gmix, *w, *cast2d)
    return list(outs[:3]) + [o.reshape(a.shape) for o, a in zip(outs[3:], casts)]


def _mamba_sample_body(x_ref, gmix_ref, wg_ref, wdt_ref, cw_ref, cb_ref, dtb_ref, alog_ref, dsk_ref, ng_ref, wout_ref,
                       cx_ref, cbm_ref, ccm_ref, sst_ref, gx_ref, wq_ref,
                       y_ref, q_ref, ssm_ref, ox_ref, obm_ref, ocm_ref,
                       xn_ref, dt_ref, acs_ref, rowsT_ref, colb_ref, scale_ref, zxg_ref, xpad_ref, yg_ref, *, seq, gp):
    s = pl.program_id(1)
    nb = Q // seq
    tail = SUBLANES - (D_CONV - 1)
    mask = _causal_mask(seq)
    head_masks = _head_lane_masks()

    @pl.when(s == 0)
    def _():
        def seg_last(v):
            v3 = v.reshape(nb, seq, v.shape[-1])
            return jnp.broadcast_to(v3[:, seq - 1:seq, :], v3.shape).reshape(v.shape)

        xn_ref[...] = _rms(x_ref[...], gmix_ref[...]).astype(BF16)
        _dt_stage(xn_ref, wdt_ref, dtb_ref, alog_ref, dt_ref, acs_ref, rowsT_ref, mask, 1, seg_last)
        acs = acs_ref[...]
        for h in range(SSM_HEADS):
            colb_ref[h] = jnp.broadcast_to(acs[:, h:h + 1], (Q, LANES))
        acsT = rowsT_ref[0, ROW_ACS]
        for bi in range(nb):
            scale_ref[bi] = jnp.broadcast_to(jnp.exp2(acsT[:, (bi + 1) * seq - 1:(bi + 1) * seq]), (LANES, LANES))

    pair_b = _div_pow2(lax.broadcasted_iota(jnp.int32, (2 * seq, GROUP_X), 0), seq)
    row_b = _div_pow2(lax.broadcasted_iota(jnp.int32, (Q, D_STATE), 0), seq)
    row_masks = [jnp.where(row_b == bi, 1.0, 0.0).astype(BF16) for bi in range(nb)]
    gis = range(gp)
    gs = [s * gp + gi for gi in gis]
    h0s = [g * HEADS_PER_GROUP for g in gs]
    parts = ((cx_ref, ox_ref, GROUP_X, slice(0, GROUP_X)),
             (cbm_ref, obm_ref, D_STATE, slice(GROUP_X, GROUP_X + D_STATE)),
             (ccm_ref, ocm_ref, D_STATE, slice(GROUP_X + D_STATE, GROUP_CONV)))
    for gi in gis:
        zxg_ref[gi] = _dot(xn_ref[...], wg_ref[gs[gi]])
    for gi in gis:
        for src, _, w, dst in parts:
            xpad_ref[gi, :, tail:SUBLANES, dst] = src[:, :, gi * w:(gi + 1) * w]
        xpad_ref[gi, :, SUBLANES:SUBLANES + seq, :] = zxg_ref[gi, :, GROUP_X:GROUP_IN].reshape(nb, seq, GROUP_CONV)
    acts = [_conv_rows(lambda k: xpad_ref[gi, :, pl.ds(tail + k, seq), :].reshape(Q, GROUP_CONV),
                       cw_ref, cb_ref, gs[gi]) for gi in gis]
    for gi in gis:
        for _, out, w, dst in parts:
            out[:, :, gi * w:(gi + 1) * w] = xpad_ref[gi, :, pl.ds(tail + seq, D_CONV - 1), dst]
    bgs = [a[:, GROUP_X:GROUP_X + D_STATE].astype(BF16) for a in acts]
    cgs = [a[:, GROUP_X + D_STATE:GROUP_CONV].astype(BF16) for a in acts]
    ydxs, eacss, xddTs = _ssd_group(
        [a[:, 0:GROUP_X] for a in acts], bgs, cgs,
        [[colb_ref[h0 + r] for r in range(HEADS_PER_GROUP)] for h0 in h0s],
        lambda i, slot, r: rowsT_ref[0, slot, pl.ds(h0s[i] + r, 1), :], [dsk_ref[g] for g in gs], mask, head_masks)
    yoffs = [jnp.concatenate(
        [jnp.where(pair_b == 0,
                   _dot_nt(cgs[gi][2 * p * seq:2 * (p + 1) * seq, :], sst_ref[2 * p, gi].astype(BF16)),
                   _dot_nt(cgs[gi][2 * p * seq:2 * (p + 1) * seq, :], sst_ref[2 * p + 1, gi].astype(BF16)))
         for p in range(nb // 2)], axis=0) for gi in gis]
    updates = [_dot(xddTs[gi].astype(BF16), jnp.concatenate([bgs[gi] * row_masks[bi] for bi in range(nb)], axis=1))
               for gi in gis]
    for gi in gis:
        for bi in range(nb):
            scale = _head_rows([scale_ref[bi, pl.ds(h0s[gi] + r, 1), :] for r in range(HEADS_PER_GROUP)])
            ssm_ref[bi, gi] = scale * sst_ref[bi, gi] + updates[gi][:, bi * D_STATE:(bi + 1) * D_STATE]
    for gi in gis:
        yg_ref[gs[gi]] = _gate_norm(ydxs[gi] + eacss[gi] * yoffs[gi], zxg_ref[gi, :, 0:GROUP_X],
                                    ng_ref[gs[gi]]).astype(BF16)

    @pl.when(s == pl.num_programs(1) - 1)
    def _():
        x1 = x_ref[...]
        for g in range(SSM_GROUPS):
            x1 = x1 + _dot(yg_ref[g], wout_ref[g * GROUP_X:(g + 1) * GROUP_X, :])
        y_ref[...] = x1
        _store_pieces(q_ref, _dot(_rms(x1, gx_ref[...]).astype(BF16), wq_ref[...]))


def _mamba_sample_call(x2d, gmix, w, conv_state, ssm_state, gx, wq, layer, seq, gp=2):
    n = x2d.shape[0]
    batch = n // seq
    nb = Q // seq
    bc_cols = SSM_GROUPS * D_STATE
    xspec = pl.BlockSpec((Q, D_MODEL), lambda i, s: (i, 0))
    stspec = pl.BlockSpec((nb, gp, GROUP_X, D_STATE), lambda i, s: (i, s, 0, 0))

    def cols(width, first):
        assert first % (gp * width) == 0
        return pl.BlockSpec((nb, D_CONV - 1, gp * width), lambda i, s: (i, 0, first // (gp * width) + s))

    conv_in = [cols(GROUP_X, 0), cols(D_STATE, D_INNER), cols(D_STATE, D_INNER + bc_cols)]
    conv_out = [cols(GROUP_X, 0), cols(D_STATE, 0), cols(D_STATE, 0)]
    y, q, ssm, ox, obm, ocm = pl.pallas_call(
        functools.partial(_mamba_sample_body, seq=seq, gp=gp),
        grid=(n // Q, SSM_GROUPS // gp),
        in_specs=[xspec, _layer_spec(gmix, layer)] + _mamba_weight_specs(w)
        + conv_in + [stspec, _layer_spec(gx, layer), _layer_spec(wq, layer)],
        out_specs=[xspec, _pieces_spec(Q, lambda i, s: (0, 0, i, 0)), stspec] + conv_out,
        out_shape=[jax.ShapeDtypeStruct(x2d.shape, F32), _pieces_shape(n),
                   jax.ShapeDtypeStruct((batch, SSM_GROUPS, GROUP_X, D_STATE), F32),
                   jax.ShapeDtypeStruct((batch, D_CONV - 1, D_INNER), F32),
                   jax.ShapeDtypeStruct((batch, D_CONV - 1, bc_cols), F32),
                   jax.ShapeDtypeStruct((batch, D_CONV - 1, bc_cols), F32)],
        scratch_shapes=[
            pltpu.VMEM((Q, D_MODEL), BF16),
            pltpu.VMEM((Q, LANES), F32),
            pltpu.VMEM((Q, LANES), F32),
            pltpu.VMEM((1, 3, LANES, Q), F32),
            pltpu.VMEM((SSM_HEADS, Q, LANES), F32),
            pltpu.VMEM((nb, LANES, LANES), F32),
            pltpu.VMEM((gp, Q, GROUP_IN), F32),
            pltpu.VMEM((gp, nb, SUBLANES + seq, GROUP_CONV), F32),
            pltpu.VMEM((SSM_GROUPS, Q, GROUP_X), BF16),
        ],
        compiler_params=_params(2),
        name="mamba_sample",
    )(x2d, gmix, *w, conv_state, conv_state, conv_state, ssm_state, gx, wq)
    return y, q, ssm, jnp.concatenate([ox, obm, ocm], axis=-1)


def _pool_sample_body(x_ref, st_ref, gmix_ref, wpool_ref, pscale_ref, gx_ref, wq_ref,
                      y_ref, q_ref, pool_ref, ext_ref, *, seq, pos0):
    nb = x_ref.shape[0]
    n = nb * seq
    x = x_ref[...].reshape(n, D_MODEL)
    u = _rms(x, gmix_ref[...])
    hist0 = 2 * SUBLANES - POOL_BUF
    ext_ref[:, pl.ds(hist0, POOL_BUF), :] = st_ref[...]
    ext_ref[:, 2 * SUBLANES:2 * SUBLANES + seq, :] = u.reshape(nb, seq, D_MODEL)
    pos = pos0 + lax.broadcasted_iota(jnp.int32, (nb, seq, 1), 1).reshape(n, 1)
    pooled = _pool_windows(
        lambda k, cols: ext_ref[:, pl.ds(2 * SUBLANES - k, seq), cols].reshape(n, POOL_GROUP_DIM), u, pos, n)
    x1 = x + _pool_mix(pooled, wpool_ref, pscale_ref[...])
    pool_ref[...] = ext_ref[:, pl.ds(hist0 + seq, POOL_BUF), :]
    y_ref[...] = x1.reshape(nb, seq, D_MODEL)
    _store_pieces(q_ref, _dot(_rms(x1, gx_ref[...]).astype(BF16), wq_ref[...]))


def _pool_sample_call(x, pool_state, gmix, wpool, pscale, gx, wq, layer, pos0, nb=16):
    batch, seq, _ = x.shape
    xspec = pl.BlockSpec((nb, seq, D_MODEL), lambda i: (i, 0, 0))
    pspec = pl.BlockSpec((nb, POOL_BUF, D_MODEL), lambda i: (i, 0, 0))
    return pl.pallas_call(
        functools.partial(_pool_sample_body, seq=seq, pos0=pos0),
        grid=(batch // nb,),
        in_specs=[xspec, pspec, _layer_spec(gmix, layer), _layer_spec(wpool, layer // 2),
                  _layer_spec(pscale, layer // 2), _layer_spec(gx, layer), _layer_spec(wq, layer)],
        out_specs=[xspec, _pieces_spec(nb * seq, lambda i: (0, 0, i, 0)), pspec],
        out_shape=[jax.ShapeDtypeStruct(x.shape, F32), _pieces_shape(batch * seq),
                   jax.ShapeDtypeStruct(pool_state.shape, F32)],
        scratch_shapes=[pltpu.VMEM((nb, 2 * SUBLANES + seq, D_MODEL), F32)],
        compiler_params=_params(1),
        name="pool_sample",
    )(x, pool_state, gmix, wpool, pscale, gx, wq)


HEAD_HALVES = XATTN_HEAD_DIM // LANES
KV_ROWS = N_MEM * XATTN_HEADS * HEAD_HALVES


def _attn_sample_stage(q_ref, k_ref, v_ref, o_ref):
    nb = k_ref.shape[1]
    seq = q_ref.shape[2] // nb
    assert seq == SUBLANES
    pieces = [(j, h) for j in range(HEAD_HALVES) for h in range(XATTN_HEADS)]
    half = XATTN_HEADS * seq
    grp = XATTN_HEADS * HEAD_HALVES
    lane = lax.broadcasted_iota(jnp.int32, (half, KV_ROWS), 1)
    rowi = lax.broadcasted_iota(jnp.int32, (half, KV_ROWS), 0)
    valid = (lane & (grp - 1)) == _div_pow2(rowi, seq)
    qs = [jnp.concatenate([q_ref[j, h, bi * seq:(bi + 1) * seq, :] for j, h in pieces], axis=0) for bi in range(nb)]
    prods = [_dot_nt((qs[bi] * (XATTN_HEAD_DIM ** -0.5)).astype(BF16), k_ref[0, bi].astype(BF16))
             for bi in range(nb)]
    ss = [jnp.where(valid, pr[0:half] + pltpu.roll(pr[half:], KV_ROWS - XATTN_HEADS, axis=1), -jnp.inf)
          for pr in prods]
    es = [jnp.exp(s - jnp.max(s, axis=-1, keepdims=True)) for s in ss]
    ps = [e * (1.0 / jnp.sum(e, axis=-1, keepdims=True)) for e in es]
    pes = [jnp.concatenate([p, pltpu.roll(p, XATTN_HEADS, axis=1)], axis=0).astype(BF16) for p in ps]
    for bi in range(nb):
        o = _dot(pes[bi], v_ref[0, bi].astype(BF16))
        for n, (j, h) in enumerate(pieces):
            o_ref[j, h, bi * seq:(bi + 1) * seq, :] = o[n * seq:(n + 1) * seq, :]


def _cache_rows(cache):
    d, b = cache.shape[:2]
    c = cache.reshape(d, b, N_MEM, XATTN_HEADS, HEAD_HALVES, LANES)
    return c.transpose(0, 1, 2, 4, 3, 5).reshape(d, b, KV_ROWS, LANES)


def _store_pieces(ref, v):
    for h in range(XATTN_HEADS):
        for j in range(HEAD_HALVES):
            c0 = (h * HEAD_HALVES + j) * LANES
            ref[j, h] = v[:, c0:c0 + LANES]


def _load_pieces(ref):
    return jnp.concatenate([ref[j, h] for h in range(XATTN_HEADS) for j in range(HEAD_HALVES)], axis=-1)


def _pieces_spec(rows, index_map):
    return pl.BlockSpec((HEAD_HALVES, XATTN_HEADS, rows, LANES), index_map)


def _pieces_shape(n_rows):
    return jax.ShapeDtypeStruct((HEAD_HALVES, XATTN_HEADS, n_rows, LANES), F32)


def _out_mlp_body(*refs, with_final):
    it = iter(refs)
    x_ref, o_ref, wo_ref, gm_ref, wup_ref, wdn_ref = (next(it) for _ in range(6))
    if with_final:
        gf_ref = next(it)
    y_ref = next(it)
    x = x_ref[...] + _dot(_load_pieces(o_ref).astype(BF16), wo_ref[...])
    x = _mlp(x, gm_ref[...], wup_ref, wdn_ref)
    if with_final:
        x = _rms(x, gf_ref[...])
    y_ref[...] = x


def _out_mlp_call(x2d, o_pieces, layer, wo, gm, wup, wdn, gfinal=None, tm=512):
    n = x2d.shape[0]
    tm = min(tm, n)
    row = pl.BlockSpec((tm, D_MODEL), lambda i: (i, 0))
    args = [x2d, o_pieces, wo, gm, wup, wdn]
    specs = [row, _pieces_spec(tm, lambda i: (0, 0, i, 0)), _layer_spec(wo, layer), _layer_spec(gm, layer),
             _layer_spec(wup, layer), _layer_spec(wdn, layer)]
    if gfinal is not None:
        args.append(gfinal)
        specs.append(_const_spec(gfinal.shape))
    return pl.pallas_call(
        functools.partial(_out_mlp_body, with_final=gfinal is not None),
        grid=(n // tm,),
        in_specs=specs,
        out_specs=row,
        out_shape=jax.ShapeDtypeStruct(x2d.shape, F32),
        compiler_params=_params(1),
        name="out_mlp",
    )(*args)


def _mamba_weights(wg, wdt, conv_w, conv_b, dt_bias, a_log, d_skip, norm_gated, w_out_bf16):
    pad = LANES - SSM_HEADS

    def grp(a, width):
        return a.reshape(a.shape[0], SSM_GROUPS, width)

    def conv_grp(a):
        return jnp.concatenate([grp(a[:, :D_INNER], GROUP_X), grp(a[:, D_INNER:D_INNER + SSM_GROUPS * D_STATE], D_STATE),
                                grp(a[:, D_INNER + SSM_GROUPS * D_STATE:], D_STATE)], axis=-1).transpose(1, 0, 2)

    cw = conv_grp(conv_w)
    cb = conv_grp(conv_b[None, :])
    dtb = jnp.pad(dt_bias, (0, pad))[None, :]
    alog = jnp.pad(a_log, (0, pad))[None, :]
    dsk = jnp.repeat(d_skip, SSM_HEADDIM).reshape(SSM_GROUPS, 1, GROUP_X)
    return (wg, wdt, cw, cb, dtb, alog, dsk, norm_gated.reshape(SSM_GROUPS, 1, GROUP_X), w_out_bf16)


def kernel(x_prompt, x_sample, cache_mem_k, cache_mem_v, state_ssm, state_conv, state_pool, mem_prompt, norm_mix, norm_xattn, norm_mem, norm_mlp, norm_final, w_in, conv_w, conv_b, dt_bias, a_log, d_skip, norm_gated, w_out, w_pool, pool_scale, w_xq, w_xk, w_xv, w_xo, w_up, w_down):
    bp, seq_p, _ = x_prompt.shape
    bs, seq_s, _ = x_sample.shape
    depth = w_xq.shape[0]
    per_layer = lambda a: a[:, None, :]

    wpool = w_pool.astype(BF16)
    g_mix, g_x, g_mlp, p_scale = (per_layer(a) for a in (norm_mix, norm_xattn, norm_mlp, pool_scale))
    g_final = norm_final[None, :]

    k_p, v_p, kb, vb, wg, wdt, wout = _kv_call(mem_prompt.reshape(bp * N_MEM, D_MODEL), per_layer(norm_mem),
                                               w_xk, w_xv, w_in[0].T, w_out[0])
    mw = _mamba_weights(wg, wdt, conv_w[0], conv_b[0], dt_bias[0], a_log[0], d_skip[0], norm_gated[0], wout)
    kv_shape = (depth, bp, N_MEM, XATTN_HEADS, XATTN_HEAD_DIM)
    kb = kb.reshape(depth, bp, N_MEM, D_MODEL)
    vb = vb.reshape(depth, bp, N_MEM, D_MODEL)

    kc, vc = _cache_rows(cache_mem_k), _cache_rows(cache_mem_v)
    xp, ssm_p, conv_p, wq, wo, wup, wdn = _mamba_prompt_call(
        x_prompt, g_mix, 0, mw, casts=(w_xq, w_xo, w_up, w_down))
    xs = x_sample.reshape(bs * seq_s, D_MODEL)
    ssm_in = state_ssm[0].reshape(bs, SSM_GROUPS, GROUP_X, D_STATE)
    xs, q, ssm_s, conv_s = _mamba_sample_call(xs, g_mix, mw, state_conv[0], ssm_in, g_x, wq, 0, seq_s)
    xp, o = _prompt_layer_call(xp, 0, kb, vb, g_x, wq, wo, g_mlp, wup, wdn, sample=(q, kc, vc))
    xs = _out_mlp_call(xs, o, 0, wo, g_mlp, wup, wdn)
    xs, q, pool_s = _pool_sample_call(xs.reshape(bs, seq_s, D_MODEL), state_pool[0], g_mix, wpool, p_scale,
                                      g_x, wq, 1, PAST_LEN)
    y_prompt, pool_p, o = _prompt_layer_call(
        xp, 1, kb, vb, g_x, wq, wo, g_mlp, wup, wdn,
        pool=(g_mix, wpool, p_scale), gfinal=g_final, sample=(q, kc, vc))
    y_sample = _out_mlp_call(xs.reshape(bs * seq_s, D_MODEL), o, 1, wo, g_mlp, wup, wdn, gfinal=g_final)

    ssm_shape = (1, -1, SSM_HEADS, SSM_HEADDIM, D_STATE)
    return (y_prompt, y_sample.reshape(bs, seq_s, D_MODEL),
            k_p.reshape(kv_shape), v_p.reshape(kv_shape),
            ssm_p.reshape(ssm_shape), conv_p[None], pool_p[None],
            ssm_s.reshape(ssm_shape), conv_s[None], pool_s[None])
```

```python
import functools

import jax
import jax.numpy as jnp
from jax import lax
from jax.experimental import pallas as pl
from jax.experimental.pallas import tpu as pltpu

F32 = jnp.float32
BF16 = jnp.bfloat16

D_MODEL = 1024
D_INNER = 2048
SSM_HEADS = 32
SSM_HEADDIM = 64
SSM_GROUPS = 8
HEADS_PER_GROUP = 4
GROUP_X = HEADS_PER_GROUP * SSM_HEADDIM
D_STATE = 128
D_CONV = 4
CONV_DIM = D_INNER + 2 * SSM_GROUPS * D_STATE
GROUP_CONV = GROUP_X + 2 * D_STATE
GROUP_IN = GROUP_X + GROUP_CONV
POOL_WINDOWS = (2, 4, 8, 16)
POOL_GROUP_DIM = 256
POOL_BUF = 15
N_MEM = 256
XATTN_HEADS = 4
XATTN_HEAD_DIM = 256
D_FF = 4096
EPS = 1e-5
PAST_LEN = 16384
LOG2_E = 1.4426950408889634

LANES = 128
SUBLANES = 8
Q = 128
SSD_STAGE = 2
FF_CHUNK = 1024
VMEM_LIMIT = 56 * 1024 * 1024

_NT = (((1,), (1,)), ((), ()))


def _params(n_grid):
    return pltpu.CompilerParams(dimension_semantics=("arbitrary",) * n_grid, vmem_limit_bytes=VMEM_LIMIT)


def _const_spec(shape):
    nd = len(shape)
    return pl.BlockSpec(shape, lambda *_: (0,) * nd, pipeline_mode=pl.Buffered(1))


def _layer_spec(a, layer):
    nd = a.ndim
    return pl.BlockSpec((None,) + a.shape[1:], lambda *_: (layer,) + (0,) * (nd - 1), pipeline_mode=pl.Buffered(1))


def _dot(a, b):
    return jnp.dot(a, b, preferred_element_type=F32)


def _dot_nt(a, b):
    return lax.dot_general(a, b, _NT, preferred_element_type=F32)


def _rms(x, g):
    return x * lax.rsqrt(jnp.mean(x * x, axis=-1, keepdims=True) + EPS) * g


def _silu(x):
    h = 0.5 * x
    return h + h * jnp.tanh(h)


def _kv_body(mem_ref, g_ref, wk_ref, wv_ref, wz_ref, wx_ref, wb_ref, wc_ref, wt_ref, wo_ref,
             k_ref, v_ref, kb_ref, vb_ref, wg_ref, wdt_ref, wob_ref):
    @pl.when(jnp.logical_and(pl.program_id(0) == 0, pl.program_id(1) == 0))
    def _():
        pad = jnp.zeros((LANES - SSM_HEADS, D_MODEL), F32)
        wdt_ref[...] = jnp.concatenate([wt_ref[0:SSM_HEADS, :], pad], axis=0).T.astype(BF16)

    wg_ref[0, :, 0:GROUP_X] = wz_ref[...].T.astype(BF16)
    wg_ref[0, :, GROUP_X:2 * GROUP_X] = wx_ref[...].T.astype(BF16)
    wg_ref[0, :, 2 * GROUP_X:2 * GROUP_X + D_STATE] = wb_ref[...].T.astype(BF16)
    wg_ref[0, :, 2 * GROUP_X + D_STATE:GROUP_IN] = wc_ref[...].T.astype(BF16)
    wob_ref[...] = wo_ref[...].astype(BF16)
    mn = _rms(mem_ref[...], g_ref[0]).astype(BF16)
    k = _dot(mn, wk_ref[0].astype(BF16))
    v = _dot(mn, wv_ref[0].astype(BF16))
    for h in range(XATTN_HEADS):
        sl = slice(h * XATTN_HEAD_DIM, (h + 1) * XATTN_HEAD_DIM)
        k_ref[0, :, h, :] = k[:, sl]
        v_ref[0, :, h, :] = v[:, sl]
    kb_ref[0] = k.astype(BF16)
    vb_ref[0] = v.astype(BF16)


def _kv_call(mem, g_mem, wk, wv, w_in_t, w_out):
    n = mem.shape[0]
    depth = wk.shape[0]
    nt = SSM_GROUPS // depth
    tm = n // nt
    assert depth * nt == SSM_GROUPS and tm * nt == n and tm % SUBLANES == 0
    row = pl.BlockSpec((tm, D_MODEL), lambda l, t: (t, 0))
    wspec = pl.BlockSpec((1, D_MODEL, D_MODEL), lambda l, t: (l, 0, 0))
    ospec = pl.BlockSpec((1, tm, D_MODEL), lambda l, t: (l, t, 0))
    hspec = pl.BlockSpec((1, tm, XATTN_HEADS, XATTN_HEAD_DIM), lambda l, t: (l, t, 0, 0))

    def rows(height, first_block):
        return pl.BlockSpec((height, D_MODEL), lambda l, t: (first_block + l * nt + t, 0))

    x0 = D_INNER // GROUP_X
    b0 = 2 * D_INNER // D_STATE
    wo_spec = pl.BlockSpec((w_out.shape[0] // SSM_GROUPS, D_MODEL), lambda l, t: (l * nt + t, 0))
    dt0 = D_INNER + CONV_DIM
    assert dt0 % LANES == 0 and w_in_t.shape[0] - dt0 == SSM_HEADS
    dt_spec = pl.BlockSpec((LANES, D_MODEL), lambda l, t: (dt0 // LANES, 0))
    return pl.pallas_call(
        _kv_body,
        grid=(depth, nt),
        in_specs=[row, pl.BlockSpec((1, 1, D_MODEL), lambda l, t: (l, 0, 0)), wspec, wspec,
                  rows(GROUP_X, 0), rows(GROUP_X, x0), rows(D_STATE, b0), rows(D_STATE, b0 + SSM_GROUPS),
                  dt_spec, wo_spec],
        out_specs=[hspec, hspec, ospec, ospec,
                   pl.BlockSpec((1, D_MODEL, GROUP_IN), lambda l, t: (l * nt + t, 0, 0)),
                   pl.BlockSpec((D_MODEL, LANES), lambda l, t: (0, 0)), wo_spec],
        out_shape=[jax.ShapeDtypeStruct((depth, n, XATTN_HEADS, XATTN_HEAD_DIM), F32)] * 2
        + [jax.ShapeDtypeStruct((depth, n, D_MODEL), BF16)] * 2
        + [jax.ShapeDtypeStruct((SSM_GROUPS, D_MODEL, GROUP_IN), BF16),
           jax.ShapeDtypeStruct((D_MODEL, LANES), BF16), jax.ShapeDtypeStruct(w_out.shape, BF16)],
        compiler_params=_params(2),
        name="mem_kv",
    )(mem, g_mem, wk, wv, w_in_t, w_in_t, w_in_t, w_in_t, w_in_t, w_out)


def _attn_heads(q, k_of, v_of):
    heads = range(XATTN_HEADS)
    ss = [_dot_nt(q[:, h * XATTN_HEAD_DIM:(h + 1) * XATTN_HEAD_DIM].astype(BF16), k_of(h)) * (XATTN_HEAD_DIM ** -0.5)
          for h in heads]
    es = [jnp.exp(s - jnp.max(s, axis=-1, keepdims=True)) for s in ss]
    ps = [e * (1.0 / jnp.sum(e, axis=-1, keepdims=True)) for e in es]
    return jnp.concatenate([_dot(ps[h].astype(BF16), v_of(h)) for h in heads], axis=-1)


def _mlp(x, g, wup_ref, wdn_ref):
    hn = _rms(x, g).astype(BF16)
    acc = x
    for c in range(D_FF // FF_CHUNK):
        a = jnp.maximum(_dot(hn, wup_ref[:, c * FF_CHUNK:(c + 1) * FF_CHUNK]), 0.0)
        acc = acc + _dot((a * a).astype(BF16), wdn_ref[c * FF_CHUNK:(c + 1) * FF_CHUNK, :])
    return acc


def _pool_windows(load, u, pos, n):
    outs = []
    for gi, w in enumerate(POOL_WINDOWS):
        cols = slice(gi * POOL_GROUP_DIM, (gi + 1) * POOL_GROUP_DIM)
        acc = load(0, cols)
        for k in range(1, w):
            acc = acc + load(k, cols)
        cnt = jnp.minimum(pos + 1, w).astype(F32)
        outs.append(acc / cnt - u[:, cols])
    return outs


def _pool_windows_rolled(ext_ref, u, pos, n):
    outs = []
    for gi, w in enumerate(POOL_WINDOWS):
        assert w & (w - 1) == 0 and w <= 2 * SUBLANES
        cols = slice(gi * POOL_GROUP_DIM, (gi + 1) * POOL_GROUP_DIM)
        acc = ext_ref[0:n + 2 * SUBLANES, cols]
        span = 1
        while span < w:
            acc = acc + pltpu.roll(acc, span, axis=0)
            span *= 2
        cnt = jnp.minimum(pos + 1, w).astype(F32)
        outs.append(acc[2 * SUBLANES:, :] / cnt - u[:, cols])
    return outs


def _pool_mix(pooled, wpool_ref, scale):
    outs = [_dot(p.astype(BF16), wpool_ref[gi]) for gi, p in enumerate(pooled)]
    return jnp.concatenate(outs, axis=-1) * scale


def _prompt_layer_body(*refs, tq, with_pool, with_final, with_sample):
    it = iter(refs)
    x_ref = next(it)
    if with_pool:
        gmix_ref, wpool_ref, pscale_ref = next(it), next(it), next(it)
    gx_ref, wq_ref, kb_ref, vb_ref, wo_ref = next(it), next(it), next(it), next(it), next(it)
    gm_ref, wup_ref, wdn_ref = next(it), next(it), next(it)
    if with_final:
        gf_ref = next(it)
    if with_sample:
        qs_ref, ks_ref, vs_ref = next(it), next(it), next(it)
    y_ref = next(it)
    if with_pool:
        pool_ref = next(it)
    if with_sample:
        os_ref = next(it)
    if with_pool:
        ext_ref = next(it)

    x = x_ref[0]
    if with_pool:
        j = pl.program_id(1)

        @pl.when(j == 0)
        def _():
            ext_ref[0:2 * SUBLANES, :] = jnp.zeros((2 * SUBLANES, D_MODEL), F32)

        u = _rms(x, gmix_ref[...])
        ext_ref[pl.ds(2 * SUBLANES, tq), :] = u
        pos = j * tq + lax.broadcasted_iota(jnp.int32, (tq, 1), 0)
        pooled = _pool_windows_rolled(ext_ref, u, pos, tq)
        x = x + _pool_mix(pooled, wpool_ref, pscale_ref[...])
        pool_ref[0] = ext_ref[pl.ds(tq + 1, POOL_BUF), :]
        ext_ref[0:2 * SUBLANES, :] = ext_ref[pl.ds(tq, 2 * SUBLANES), :]

    if with_sample:
        _attn_sample_stage(qs_ref, ks_ref, vs_ref, os_ref)

    q = _dot(_rms(x, gx_ref[...]).astype(BF16), wq_ref[...])
    o = _attn_heads(
        q,
        lambda h: kb_ref[0, 0, :, h * XATTN_HEAD_DIM:(h + 1) * XATTN_HEAD_DIM],
        lambda h: vb_ref[0, 0, :, h * XATTN_HEAD_DIM:(h + 1) * XATTN_HEAD_DIM],
    )
    x = x + _dot(o.astype(BF16), wo_ref[...])
    x = _mlp(x, gm_ref[...], wup_ref, wdn_ref)
    if with_final:
        x = _rms(x, gf_ref[...])
    y_ref[0] = x


def _prompt_layer_call(x, layer, kb, vb, gx, wq, wo, gm, wup, wdn, pool=None, gfinal=None, sample=None, tq=512):
    b, seq, _ = x.shape
    with_pool, with_final, with_sample = pool is not None, gfinal is not None, sample is not None
    nj = seq // tq
    xspec = pl.BlockSpec((1, tq, D_MODEL), lambda i, j: (i, j, 0))
    kvspec = pl.BlockSpec((1, 1, N_MEM, D_MODEL), lambda i, j: (layer, i, 0, 0))
    args, specs = [x], [xspec]
    if with_pool:
        gmix, wpool, pscale = pool
        args += [gmix, wpool, pscale]
        specs += [_layer_spec(gmix, layer), _layer_spec(wpool, layer // 2), _layer_spec(pscale, layer // 2)]
    args += [gx, wq, kb, vb, wo, gm, wup, wdn]
    specs += [_layer_spec(gx, layer), _layer_spec(wq, layer), kvspec, kvspec, _layer_spec(wo, layer),
              _layer_spec(gm, layer), _layer_spec(wup, layer), _layer_spec(wdn, layer)]
    if with_final:
        args.append(gfinal)
        specs.append(_const_spec(gfinal.shape))
    out_shape = [jax.ShapeDtypeStruct(x.shape, F32)]
    out_specs = [xspec]
    scratch = []
    if with_pool:
        out_shape.append(jax.ShapeDtypeStruct((b, POOL_BUF, D_MODEL), F32))
        out_specs.append(pl.BlockSpec((1, POOL_BUF, D_MODEL), lambda i, j: (i, 0, 0)))
        scratch.append(pltpu.VMEM((tq + 2 * SUBLANES, D_MODEL), F32))
    if with_sample:
        q_pieces, k_rows, v_rows = sample
        n_batch = k_rows.shape[1]
        nb = n_batch // (b * nj)
        rows = q_pieces.shape[2] // (b * nj)
        assert nb * b * nj == n_batch and rows == nb * SUBLANES
        qspec = pl.BlockSpec(q_pieces.shape[:2] + (rows, LANES), lambda i, j: (0, 0, i * nj + j, 0))
        cspec = pl.BlockSpec((1, nb, KV_ROWS, LANES), lambda i, j: (layer, i * nj + j, 0, 0))
        args += [q_pieces, k_rows, v_rows]
        specs += [qspec, cspec, cspec]
        out_shape.append(jax.ShapeDtypeStruct(q_pieces.shape, F32))
        out_specs.append(qspec)
    outs = pl.pallas_call(
        functools.partial(_prompt_layer_body, tq=tq, with_pool=with_pool, with_final=with_final,
                          with_sample=with_sample),
        grid=(b, nj),
        in_specs=specs,
        out_specs=out_specs,
        out_shape=out_shape,
        scratch_shapes=scratch,
        compiler_params=_params(2),
        name="prompt_layer_pool" if with_pool else "prompt_layer",
    )(*args)
    return outs if len(outs) > 1 else outs[0]


ROW_ACS, ROW_DECAY, ROW_END = 0, 1, 2


def _dt_stage(xn_ref, wdt_ref, dtb_ref, alog_ref, dt_ref, acs_ref, rowsT_ref, mask, n_chunks, seg_last):
    z = _dot(xn_ref[...], wdt_ref[...]) + dtb_ref[...]
    dt_ref[...] = jnp.maximum(z, 0.0) + jnp.log1p(jnp.exp(-jnp.abs(z)))
    a = -jnp.exp(alog_ref[...]) * LOG2_E
    tri = mask.astype(F32)
    for c in range(n_chunks):
        rows = slice(c * Q, (c + 1) * Q)
        dt_c = dt_ref[rows, :]
        acs = jnp.dot(tri, dt_c * a, precision=lax.Precision.HIGHEST, preferred_element_type=F32)
        acs_ref[rows, :] = acs
        acsT, dtT = acs.T, dt_c.T
        rowsT_ref[c, ROW_ACS] = acsT
        rowsT_ref[c, ROW_DECAY] = acsT - jnp.log2(dtT)
        rowsT_ref[c, ROW_END] = dtT * jnp.exp2(seg_last(acs).T - acsT)


def _conv_rows(load_shifted, cw_ref, cb_ref, g):
    acc = cb_ref[g]
    for k in range(D_CONV):
        acc = acc + cw_ref[g, k:k + 1, :] * load_shifted(k)
    return _silu(acc)


def _conv_rolled(exts, cw_ref, cb_ref, g):
    accs = [cb_ref[g] + cw_ref[g, D_CONV - 1:D_CONV, :] * e for e in exts]
    for s in range(1, D_CONV):
        accs = [a + cw_ref[g, D_CONV - 1 - s:D_CONV - s, :] * pltpu.roll(e, s, axis=0) for a, e in zip(accs, exts)]
    return [_silu(a[SUBLANES:, :]) for a in accs]


def _expand_heads(cols):
    half = lax.broadcasted_iota(jnp.int32, (Q, LANES), 1) < SSM_HEADDIM
    return jnp.concatenate([jnp.where(half, cols[0], cols[1]), jnp.where(half, cols[2], cols[3])], axis=1)


def _head_lane_masks():
    lane_head = _div_pow2(lax.broadcasted_iota(jnp.int32, (Q, GROUP_X), 1), SSM_HEADDIM)
    return [jnp.where(lane_head == r, 1.0, 0.0).astype(BF16) for r in range(HEADS_PER_GROUP)]


def _ssd_group(xgs, bgs, cgs, acs_cols, row_of, dskips, mask, head_masks):
    n = len(xgs)
    heads = range(HEADS_PER_GROUP)
    cbs = [_dot_nt(cg, bg) for cg, bg in zip(cgs, bgs)]
    xbs = [xg.astype(BF16) for xg in xgs]
    lcats = [jnp.concatenate(
        [(cbs[i] * jnp.exp2(jnp.where(mask, acs_cols[i][r] - row_of(i, ROW_DECAY, r), -jnp.inf))).astype(BF16)
         for r in heads], axis=1) for i in range(n)]
    xblks = [jnp.concatenate([xb * hm for hm in head_masks], axis=0) for xb in xbs]
    ydxs = [_dot(l, xb) + dsk * xg for l, xb, xg, dsk in zip(lcats, xblks, xgs, dskips)]
    xTs = [xg.T for xg in xgs]
    xddTs = [jnp.concatenate(
        [xTs[i][r * SSM_HEADDIM:(r + 1) * SSM_HEADDIM, :] * row_of(i, ROW_END, r) for r in heads], axis=0)
        for i in range(n)]
    eacss = [jnp.exp2(_expand_heads(cols)) for cols in acs_cols]
    return ydxs, eacss, xddTs


def _gate_norm(y, z, ng):
    yg = y * _silu(z)
    return yg * lax.rsqrt(jnp.mean(yg * yg, axis=-1, keepdims=True) + EPS) * ng


def _div_pow2(v, d):
    assert d & (d - 1) == 0
    return lax.shift_right_logical(v, d.bit_length() - 1)


def _causal_mask(block):
    li = lax.broadcasted_iota(jnp.int32, (Q, Q), 0)
    si = lax.broadcasted_iota(jnp.int32, (Q, Q), 1)
    m = li >= si
    if block is not None:
        m = jnp.logical_and(m, _div_pow2(li, block) == _div_pow2(si, block))
    return m


def _head_rows(vals):
    return jnp.concatenate([jnp.broadcast_to(v, (SSM_HEADDIM, LANES)) for v in vals], axis=0)


def _mamba_prompt_body(x_ref, gmix_ref, wg_ref, wdt_ref, cw_ref, cb_ref, dtb_ref, alog_ref, dsk_ref, ng_ref, wout_ref,
                       *rest, tq, n_cast):
    cast_in, rest = rest[:n_cast], rest[n_cast:]
    y_ref, ssm_ref, conv_ref = rest[:3]
    cast_out, rest = rest[3:3 + n_cast], rest[3 + n_cast:]
    xn_ref, dt_ref, acs_ref, rowsT_ref, zxg_ref, xpad_ref, act_ref, hist_ref, h_ref, yg_ref = rest
    j = pl.program_id(1)
    n_chunks = tq // Q

    for src, dst in zip(cast_in, cast_out):
        dst[...] = src[...].astype(BF16)

    @pl.when(j == 0)
    def _():
        hist_ref[...] = jnp.zeros_like(hist_ref)
        h_ref[...] = jnp.zeros_like(h_ref)

    mask = _causal_mask(None)
    head_masks = _head_lane_masks()
    xn_ref[...] = _rms(x_ref[0], gmix_ref[...]).astype(BF16)

    def in_proj(g):
        zxg_ref[g % 2] = _dot(xn_ref[...], wg_ref[g])

    in_proj(0)
    _dt_stage(xn_ref, wdt_ref, dtb_ref, alog_ref, dt_ref, acs_ref, rowsT_ref, mask, n_chunks,
              lambda v: jnp.broadcast_to(v[Q - 1:Q, :], v.shape))
    for g in range(SSM_GROUPS):
        if g + 1 < SSM_GROUPS:
            in_proj(g + 1)
        zxg, xpad, act = zxg_ref.at[g % 2], xpad_ref.at[g % 2], act_ref.at[g % 2]
        xpad[0:SUBLANES, :] = hist_ref[g]
        xpad[pl.ds(SUBLANES, tq), :] = zxg[:, GROUP_X:GROUP_IN]

        conv_out = _conv_rolled([xpad[c * Q:(c + 1) * Q + SUBLANES, :] for c in range(n_chunks)], cw_ref, cb_ref, g)
        for c, o in enumerate(conv_out):
            act[c * Q:(c + 1) * Q, :] = o
        hist_ref[g] = xpad[pl.ds(tq, SUBLANES), :]

        h_in = h_ref[g]
        for c0 in range(0, n_chunks, SSD_STAGE):
            chunks = list(range(c0, min(c0 + SSD_STAGE, n_chunks)))
            idx = range(len(chunks))
            rows = [slice(c * Q, (c + 1) * Q) for c in chunks]
            bgs = [act[r, GROUP_X:GROUP_X + D_STATE].astype(BF16) for r in rows]
            cgs = [act[r, GROUP_X + D_STATE:GROUP_CONV].astype(BF16) for r in rows]
            rowsT_cs = [rowsT_ref.at[c] for c in chunks]
            heads = range(HEADS_PER_GROUP * g, HEADS_PER_GROUP * (g + 1))
            acs_cols = [[jnp.broadcast_to(acs_ref[r, h:h + 1], (Q, LANES)) for h in heads] for r in rows]
            ydxs, eacss, xddTs = _ssd_group(
                [act[r, 0:GROUP_X] for r in rows], bgs, cgs, acs_cols,
                lambda i, slot, r: rowsT_cs[i][slot, heads[r]:heads[r] + 1, :], [dsk_ref[g]] * len(rows),
                mask, head_masks)
            updates = [_dot(xddTs[i].astype(BF16), bgs[i]) for i in idx]
            scales = [_head_rows([jnp.exp2(rowsT_cs[i][ROW_ACS, h:h + 1, Q - 1:Q]) for h in heads]) for i in idx]
            hs = [h_in]
            for i in idx:
                hs.append(scales[i] * hs[i] + updates[i])
            h_in = hs[-1]
            yoffs = [_dot_nt(cgs[i], hs[i].astype(BF16)) for i in idx]
            for i in idx:
                yg_ref[rows[i], g * GROUP_X:(g + 1) * GROUP_X] = _gate_norm(
                    ydxs[i] + eacss[i] * yoffs[i], zxg[rows[i], 0:GROUP_X], ng_ref[g]).astype(BF16)
        h_ref[g] = h_in

        hist = hist_ref[g]
        conv_ref[0, :, g * GROUP_X:(g + 1) * GROUP_X] = hist[SUBLANES - (D_CONV - 1):, 0:GROUP_X]
        conv_ref[0, :, D_INNER + g * D_STATE:D_INNER + (g + 1) * D_STATE] = (
            hist[SUBLANES - (D_CONV - 1):, GROUP_X:GROUP_X + D_STATE])
        conv_ref[0, :, D_INNER + (SSM_GROUPS + g) * D_STATE:D_INNER + (SSM_GROUPS + g + 1) * D_STATE] = (
            hist[SUBLANES - (D_CONV - 1):, GROUP_X + D_STATE:GROUP_CONV])

        if g % 2 == 1:
            cols = slice((g - 1) * GROUP_X, (g + 1) * GROUP_X)
            y_ref[0] = (x_ref[0] if g == 1 else y_ref[0]) + _dot(yg_ref[:, cols], wout_ref[cols, :])

    @pl.when(j == pl.num_programs(1) - 1)
    def _():
        ssm_ref[0] = h_ref[...]


def _mamba_weight_specs(w):
    return [_const_spec(a.shape) for a in w]


def _mamba_prompt_call(x, gmix, layer, w, casts=(), tq=512):
    b, seq, _ = x.shape
    nj = seq // tq
    xspec = pl.BlockSpec((1, tq, D_MODEL), lambda i, j: (i, j, 0))
    n_chunks = tq // Q
    cast2d = [a.reshape(-1, a.shape[-1]) for a in casts]
    cast_specs = []
    for a in cast2d:
        rows = a.shape[0] // (b * nj)
        assert rows * b * nj == a.shape[0] and rows % (2 * SUBLANES) == 0
        cast_specs.append(pl.BlockSpec((rows, a.shape[1]), lambda i, j: (i * nj + j, 0)))
    outs = pl.pallas_call(
        functools.partial(_mamba_prompt_body, tq=tq, n_cast=len(casts)),
        grid=(b, nj),
        in_specs=[xspec, _layer_spec(gmix, layer)] + _mamba_weight_specs(w) + cast_specs,
        out_specs=[xspec,
                   pl.BlockSpec((1, SSM_GROUPS, GROUP_X, D_STATE), lambda i, j: (i, 0, 0, 0)),
                   pl.BlockSpec((1, D_CONV - 1, CONV_DIM), lambda i, j: (i, 0, 0))] + cast_specs,
        out_shape=[jax.ShapeDtypeStruct(x.shape, F32),
                   jax.ShapeDtypeStruct((b, SSM_GROUPS, GROUP_X, D_STATE), F32),
                   jax.ShapeDtypeStruct((b, D_CONV - 1, CONV_DIM), F32)]
        + [jax.ShapeDtypeStruct(a.shape, BF16) for a in cast2d],
        scratch_shapes=[
            pltpu.VMEM((tq, D_MODEL), BF16),
            pltpu.VMEM((tq, LANES), F32),
            pltpu.VMEM((tq, LANES), F32),
            pltpu.VMEM((n_chunks, 3, LANES, Q), F32),
            pltpu.VMEM((2, tq, GROUP_IN), F32),
            pltpu.VMEM((2, tq + SUBLANES, GROUP_CONV), F32),
            pltpu.VMEM((2, tq, GROUP_CONV), F32),
            pltpu.VMEM((SSM_GROUPS, SUBLANES, GROUP_CONV), F32),
            pltpu.VMEM((SSM_GROUPS, GROUP_X, D_STATE), F32),
            pltpu.VMEM((tq, D_INNER), BF16),
        ],
        compiler_params=_params(2),
        name="mamba_prompt",
    )(x, gmix, *w, *cast2d)
    return list(outs[:3]) + [o.reshape(a.shape) for o, a in zip(outs[3:], casts)]


def _mamba_sample_body(x_ref, gmix_ref, wg_ref, wdt_ref, cw_ref, cb_ref, dtb_ref, alog_ref, dsk_ref, ng_ref, wout_ref,
                       cx_ref, cbm_ref, ccm_ref, sst_ref, gx_ref, wq_ref,
                       y_ref, q_ref, ssm_ref, ox_ref, obm_ref, ocm_ref,
                       xn_ref, dt_ref, acs_ref, rowsT_ref, colb_ref, scale_ref, zxg_ref, xpad_ref, yg_ref, *, seq, gp):
    s = pl.program_id(1)
    nb = Q // seq
    tail = SUBLANES - (D_CONV - 1)
    mask = _causal_mask(seq)
    head_masks = _head_lane_masks()

    @pl.when(s == 0)
    def _():
        def seg_last(v):
            v3 = v.reshape(nb, seq, v.shape[-1])
            return jnp.broadcast_to(v3[:, seq - 1:seq, :], v3.shape).reshape(v.shape)

        xn_ref[...] = _rms(x_ref[...], gmix_ref[...]).astype(BF16)
        _dt_stage(xn_ref, wdt_ref, dtb_ref, alog_ref, dt_ref, acs_ref, rowsT_ref, mask, 1, seg_last)
        acs = acs_ref[...]
        for h in range(SSM_HEADS):
            colb_ref[h] = jnp.broadcast_to(acs[:, h:h + 1], (Q, LANES))
        acsT = rowsT_ref[0, ROW_ACS]
        for bi in range(nb):
            scale_ref[bi] = jnp.broadcast_to(jnp.exp2(acsT[:, (bi + 1) * seq - 1:(bi + 1) * seq]), (LANES, LANES))

    pair_b = _div_pow2(lax.broadcasted_iota(jnp.int32, (2 * seq, GROUP_X), 0), seq)
    row_b = _div_pow2(lax.broadcasted_iota(jnp.int32, (Q, D_STATE), 0), seq)
    row_masks = [jnp.where(row_b == bi, 1.0, 0.0).astype(BF16) for bi in range(nb)]
    gis = range(gp)
    gs = [s * gp + gi for gi in gis]
    h0s = [g * HEADS_PER_GROUP for g in gs]
    parts = ((cx_ref, ox_ref, GROUP_X, slice(0, GROUP_X)),
             (cbm_ref, obm_ref, D_STATE, slice(GROUP_X, GROUP_X + D_STATE)),
             (ccm_ref, ocm_ref, D_STATE, slice(GROUP_X + D_STATE, GROUP_CONV)))
    for gi in gis:
        zxg_ref[gi] = _dot(xn_ref[...], wg_ref[gs[gi]])
    for gi in gis:
        for src, _, w, dst in parts:
            xpad_ref[gi, :, tail:SUBLANES, dst] = src[:, :, gi * w:(gi + 1) * w]
        xpad_ref[gi, :, SUBLANES:SUBLANES + seq, :] = zxg_ref[gi, :, GROUP_X:GROUP_IN].reshape(nb, seq, GROUP_CONV)
    acts = [_conv_rows(lambda k: xpad_ref[gi, :, pl.ds(tail + k, seq), :].reshape(Q, GROUP_CONV),
                       cw_ref, cb_ref, gs[gi]) for gi in gis]
    for gi in gis:
        for _, out, w, dst in parts:
            out[:, :, gi * w:(gi + 1) * w] = xpad_ref[gi, :, pl.ds(tail + seq, D_CONV - 1), dst]
    bgs = [a[:, GROUP_X:GROUP_X + D_STATE].astype(BF16) for a in acts]
    cgs = [a[:, GROUP_X + D_STATE:GROUP_CONV].astype(BF16) for a in acts]
    ydxs, eacss, xddTs = _ssd_group(
        [a[:, 0:GROUP_X] for a in acts], bgs, cgs,
        [[colb_ref[h0 + r] for r in range(HEADS_PER_GROUP)] for h0 in h0s],
        lambda i, slot, r: rowsT_ref[0, slot, pl.ds(h0s[i] + r, 1), :], [dsk_ref[g] for g in gs], mask, head_masks)
    yoffs = [jnp.concatenate(
        [jnp.where(pair_b == 0,
                   _dot_nt(cgs[gi][2 * p * seq:2 * (p + 1) * seq, :], sst_ref[2 * p, gi].astype(BF16)),
                   _dot_nt(cgs[gi][2 * p * seq:2 * (p + 1) * seq, :], sst_ref[2 * p + 1, gi].astype(BF16)))
         for p in range(nb // 2)], axis=0) for gi in gis]
    updates = [_dot(xddTs[gi].astype(BF16), jnp.concatenate([bgs[gi] * row_masks[bi] for bi in range(nb)], axis=1))
               for gi in gis]
    for gi in gis:
        for bi in range(nb):
            scale = _head_rows([scale_ref[bi, pl.ds(h0s[gi] + r, 1), :] for r in range(HEADS_PER_GROUP)])
            ssm_ref[bi, gi] = scale * sst_ref[bi, gi] + updates[gi][:, bi * D_STATE:(bi + 1) * D_STATE]
    for gi in gis:
        yg_ref[gs[gi]] = _gate_norm(ydxs[gi] + eacss[gi] * yoffs[gi], zxg_ref[gi, :, 0:GROUP_X],
                                    ng_ref[gs[gi]]).astype(BF16)

    @pl.when(s == pl.num_programs(1) - 1)
    def _():
        x1 = x_ref[...]
        for g in range(SSM_GROUPS):
            x1 = x1 + _dot(yg_ref[g], wout_ref[g * GROUP_X:(g + 1) * GROUP_X, :])
        y_ref[...] = x1
        _store_pieces(q_ref, _dot(_rms(x1, gx_ref[...]).astype(BF16), wq_ref[...]))


def _mamba_sample_call(x2d, gmix, w, conv_state, ssm_state, gx, wq, layer, seq, gp=2):
    n = x2d.shape[0]
    batch = n // seq
    nb = Q // seq
    bc_cols = SSM_GROUPS * D_STATE
    xspec = pl.BlockSpec((Q, D_MODEL), lambda i, s: (i, 0))
    stspec = pl.BlockSpec((nb, gp, GROUP_X, D_STATE), lambda i, s: (i, s, 0, 0))

    def cols(width, first):
        assert first % (gp * width) == 0
        return pl.BlockSpec((nb, D_CONV - 1, gp * width), lambda i, s: (i, 0, first // (gp * width) + s))

    conv_in = [cols(GROUP_X, 0), cols(D_STATE, D_INNER), cols(D_STATE, D_INNER + bc_cols)]
    conv_out = [cols(GROUP_X, 0), cols(D_STATE, 0), cols(D_STATE, 0)]
    y, q, ssm, ox, obm, ocm = pl.pallas_call(
        functools.partial(_mamba_sample_body, seq=seq, gp=gp),
        grid=(n // Q, SSM_GROUPS // gp),
        in_specs=[xspec, _layer_spec(gmix, layer)] + _mamba_weight_specs(w)
        + conv_in + [stspec, _layer_spec(gx, layer), _layer_spec(wq, layer)],
        out_specs=[xspec, _pieces_spec(Q, lambda i, s: (0, 0, i, 0)), stspec] + conv_out,
        out_shape=[jax.ShapeDtypeStruct(x2d.shape, F32), _pieces_shape(n),
                   jax.ShapeDtypeStruct((batch, SSM_GROUPS, GROUP_X, D_STATE), F32),
                   jax.ShapeDtypeStruct((batch, D_CONV - 1, D_INNER), F32),
                   jax.ShapeDtypeStruct((batch, D_CONV - 1, bc_cols), F32),
                   jax.ShapeDtypeStruct((batch, D_CONV - 1, bc_cols), F32)],
        scratch_shapes=[
            pltpu.VMEM((Q, D_MODEL), BF16),
            pltpu.VMEM((Q, LANES), F32),
            pltpu.VMEM((Q, LANES), F32),
            pltpu.VMEM((1, 3, LANES, Q), F32),
            pltpu.VMEM((SSM_HEADS, Q, LANES), F32),
            pltpu.VMEM((nb, LANES, LANES), F32),
            pltpu.VMEM((gp, Q, GROUP_IN), F32),
            pltpu.VMEM((gp, nb, SUBLANES + seq, GROUP_CONV), F32),
            pltpu.VMEM((SSM_GROUPS, Q, GROUP_X), BF16),
        ],
        compiler_params=_params(2),
        name="mamba_sample",
    )(x2d, gmix, *w, conv_state, conv_state, conv_state, ssm_state, gx, wq)
    return y, q, ssm, jnp.concatenate([ox, obm, ocm], axis=-1)


def _pool_sample_body(x_ref, st_ref, gmix_ref, wpool_ref, pscale_ref, gx_ref, wq_ref,
                      y_ref, q_ref, pool_ref, ext_ref, *, seq, pos0):
    nb = x_ref.shape[0]
    n = nb * seq
    x = x_ref[...].reshape(n, D_MODEL)
    u = _rms(x, gmix_ref[...])
    hist0 = 2 * SUBLANES - POOL_BUF
    ext_ref[:, pl.ds(hist0, POOL_BUF), :] = st_ref[...]
    ext_ref[:, 2 * SUBLANES:2 * SUBLANES + seq, :] = u.reshape(nb, seq, D_MODEL)
    pos = pos0 + lax.broadcasted_iota(jnp.int32, (nb, seq, 1), 1).reshape(n, 1)
    pooled = _pool_windows(
        lambda k, cols: ext_ref[:, pl.ds(2 * SUBLANES - k, seq), cols].reshape(n, POOL_GROUP_DIM), u, pos, n)
    x1 = x + _pool_mix(pooled, wpool_ref, pscale_ref[...])
    pool_ref[...] = ext_ref[:, pl.ds(hist0 + seq, POOL_BUF), :]
    y_ref[...] = x1.reshape(nb, seq, D_MODEL)
    _store_pieces(q_ref, _dot(_rms(x1, gx_ref[...]).astype(BF16), wq_ref[...]))


def _pool_sample_call(x, pool_state, gmix, wpool, pscale, gx, wq, layer, pos0, nb=16):
    batch, seq, _ = x.shape
    xspec = pl.BlockSpec((nb, seq, D_MODEL), lambda i: (i, 0, 0))
    pspec = pl.BlockSpec((nb, POOL_BUF, D_MODEL), lambda i: (i, 0, 0))
    return pl.pallas_call(
        functools.partial(_pool_sample_body, seq=seq, pos0=pos0),
        grid=(batch // nb,),
        in_specs=[xspec, pspec, _layer_spec(gmix, layer), _layer_spec(wpool, layer // 2),
                  _layer_spec(pscale, layer // 2), _layer_spec(gx, layer), _layer_spec(wq, layer)],
        out_specs=[xspec, _pieces_spec(nb * seq, lambda i: (0, 0, i, 0)), pspec],
        out_shape=[jax.ShapeDtypeStruct(x.shape, F32), _pieces_shape(batch * seq),
                   jax.ShapeDtypeStruct(pool_state.shape, F32)],
        scratch_shapes=[pltpu.VMEM((nb, 2 * SUBLANES + seq, D_MODEL), F32)],
        compiler_params=_params(1),
        name="pool_sample",
    )(x, pool_state, gmix, wpool, pscale, gx, wq)


HEAD_HALVES = XATTN_HEAD_DIM // LANES
KV_ROWS = N_MEM * XATTN_HEADS * HEAD_HALVES


def _attn_sample_stage(q_ref, k_ref, v_ref, o_ref):
    nb = k_ref.shape[1]
    seq = q_ref.shape[2] // nb
    assert seq == SUBLANES
    pieces = [(j, h) for j in range(HEAD_HALVES) for h in range(XATTN_HEADS)]
    half = XATTN_HEADS * seq
    grp = XATTN_HEADS * HEAD_HALVES
    lane = lax.broadcasted_iota(jnp.int32, (half, KV_ROWS), 1)
    rowi = lax.broadcasted_iota(jnp.int32, (half, KV_ROWS), 0)
    valid = (lane & (grp - 1)) == _div_pow2(rowi, seq)
    qs = [jnp.concatenate([q_ref[j, h, bi * seq:(bi + 1) * seq, :] for j, h in pieces], axis=0) for bi in range(nb)]
    prods = [_dot_nt((qs[bi] * (XATTN_HEAD_DIM ** -0.5)).astype(BF16), k_ref[0, bi].astype(BF16))
             for bi in range(nb)]
    ss = [jnp.where(valid, pr[0:half] + pltpu.roll(pr[half:], KV_ROWS - XATTN_HEADS, axis=1), -jnp.inf)
          for pr in prods]
    es = [jnp.exp(s - jnp.max(s, axis=-1, keepdims=True)) for s in ss]
    ps = [e * (1.0 / jnp.sum(e, axis=-1, keepdims=True)) for e in es]
    pes = [jnp.concatenate([p, pltpu.roll(p, XATTN_HEADS, axis=1)], axis=0).astype(BF16) for p in ps]
    for bi in range(nb):
        o = _dot(pes[bi], v_ref[0, bi].astype(BF16))
        for n, (j, h) in enumerate(pieces):
            o_ref[j, h, bi * seq:(bi + 1) * seq, :] = o[n * seq:(n + 1) * seq, :]


def _cache_rows(cache):
    d, b = cache.shape[:2]
    c = cache.reshape(d, b, N_MEM, XATTN_HEADS, HEAD_HALVES, LANES)
    return c.transpose(0, 1, 2, 4, 3, 5).reshape(d, b, KV_ROWS, LANES)


def _store_pieces(ref, v):
    for h in range(XATTN_HEADS):
        for j in range(HEAD_HALVES):
            c0 = (h * HEAD_HALVES + j) * LANES
            ref[j, h] = v[:, c0:c0 + LANES]


def _load_pieces(ref):
    return jnp.concatenate([ref[j, h] for h in range(XATTN_HEADS) for j in range(HEAD_HALVES)], axis=-1)


def _pieces_spec(rows, index_map):
    return pl.BlockSpec((HEAD_HALVES, XATTN_HEADS, rows, LANES), index_map)


def _pieces_shape(n_rows):
    return jax.ShapeDtypeStruct((HEAD_HALVES, XATTN_HEADS, n_rows, LANES), F32)


def _out_mlp_body(*refs, with_final):
    it = iter(refs)
    x_ref, o_ref, wo_ref, gm_ref, wup_ref, wdn_ref = (next(it) for _ in range(6))
    if with_final:
        gf_ref = next(it)
    y_ref = next(it)
    x = x_ref[...] + _dot(_load_pieces(o_ref).astype(BF16), wo_ref[...])
    x = _mlp(x, gm_ref[...], wup_ref, wdn_ref)
    if with_final:
        x = _rms(x, gf_ref[...])
    y_ref[...] = x


def _out_mlp_call(x2d, o_pieces, layer, wo, gm, wup, wdn, gfinal=None, tm=512):
    n = x2d.shape[0]
    tm = min(tm, n)
    row = pl.BlockSpec((tm, D_MODEL), lambda i: (i, 0))
    args = [x2d, o_pieces, wo, gm, wup, wdn]
    specs = [row, _pieces_spec(tm, lambda i: (0, 0, i, 0)), _layer_spec(wo, layer), _layer_spec(gm, layer),
             _layer_spec(wup, layer), _layer_spec(wdn, layer)]
    if gfinal is not None:
        args.append(gfinal)
        specs.append(_const_spec(gfinal.shape))
    return pl.pallas_call(
        functools.partial(_out_mlp_body, with_final=gfinal is not None),
        grid=(n // tm,),
        in_specs=specs,
        out_specs=row,
        out_shape=jax.ShapeDtypeStruct(x2d.shape, F32),
        compiler_params=_params(1),
        name="out_mlp",
    )(*args)


def _mamba_weights(wg, wdt, conv_w, conv_b, dt_bias, a_log, d_skip, norm_gated, w_out_bf16):
    pad = LANES - SSM_HEADS

    def grp(a, width):
        return a.reshape(a.shape[0], SSM_GROUPS, width)

    def conv_grp(a):
        return jnp.concatenate([grp(a[:, :D_INNER], GROUP_X), grp(a[:, D_INNER:D_INNER + SSM_GROUPS * D_STATE], D_STATE),
                                grp(a[:, D_INNER + SSM_GROUPS * D_STATE:], D_STATE)], axis=-1).transpose(1, 0, 2)

    cw = conv_grp(conv_w)
    cb = conv_grp(conv_b[None, :])
    dtb = jnp.pad(dt_bias, (0, pad))[None, :]
    alog = jnp.pad(a_log, (0, pad))[None, :]
    dsk = jnp.repeat(d_skip, SSM_HEADDIM).reshape(SSM_GROUPS, 1, GROUP_X)
    return (wg, wdt, cw, cb, dtb, alog, dsk, norm_gated.reshape(SSM_GROUPS, 1, GROUP_X), w_out_bf16)


def kernel(x_prompt, x_sample, cache_mem_k, cache_mem_v, state_ssm, state_conv, state_pool, mem_prompt, norm_mix, norm_xattn, norm_mem, norm_mlp, norm_final, w_in, conv_w, conv_b, dt_bias, a_log, d_skip, norm_gated, w_out, w_pool, pool_scale, w_xq, w_xk, w_xv, w_xo, w_up, w_down):
    bp, seq_p, _ = x_prompt.shape
    bs, seq_s, _ = x_sample.shape
    depth = w_xq.shape[0]
    per_layer = lambda a: a[:, None, :]

    wpool = w_pool.astype(BF16)
    g_mix, g_x, g_mlp, p_scale = (per_layer(a) for a in (norm_mix, norm_xattn, norm_mlp, pool_scale))
    g_final = norm_final[None, :]

    k_p, v_p, kb, vb, wg, wdt, wout = _kv_call(mem_prompt.reshape(bp * N_MEM, D_MODEL), per_layer(norm_mem),
                                               w_xk, w_xv, w_in[0].T, w_out[0])
    mw = _mamba_weights(wg, wdt, conv_w[0], conv_b[0], dt_bias[0], a_log[0], d_skip[0], norm_gated[0], wout)
    kv_shape = (depth, bp, N_MEM, XATTN_HEADS, XATTN_HEAD_DIM)
    kb = kb.reshape(depth, bp, N_MEM, D_MODEL)
    vb = vb.reshape(depth, bp, N_MEM, D_MODEL)

    kc, vc = _cache_rows(cache_mem_k), _cache_rows(cache_mem_v)
    xp, ssm_p, conv_p, wq, wo, wup, wdn = _mamba_prompt_call(
        x_prompt, g_mix, 0, mw, casts=(w_xq, w_xo, w_up, w_down))
    xs = x_sample.reshape(bs * seq_s, D_MODEL)
    ssm_in = state_ssm[0].reshape(bs, SSM_GROUPS, GROUP_X, D_STATE)
    xs, q, ssm_s, conv_s = _mamba_sample_call(xs, g_mix, mw, state_conv[0], ssm_in, g_x, wq, 0, seq_s)
    xp, o = _prompt_layer_call(xp, 0, kb, vb, g_x, wq, wo, g_mlp, wup, wdn, sample=(q, kc, vc))
    xs = _out_mlp_call(xs, o, 0, wo, g_mlp, wup, wdn)
    xs, q, pool_s = _pool_sample_call(xs.reshape(bs, seq_s, D_MODEL), state_pool[0], g_mix, wpool, p_scale,
                                      g_x, wq, 1, PAST_LEN)
    y_prompt, pool_p, o = _prompt_layer_call(
        xp, 1, kb, vb, g_x, wq, wo, g_mlp, wup, wdn,
        pool=(g_mix, wpool, p_scale), gfinal=g_final, sample=(q, kc, vc))
    y_sample = _out_mlp_call(xs.reshape(bs * seq_s, D_MODEL), o, 1, wo, g_mlp, wup, wdn, gfinal=g_final)

    ssm_shape = (1, -1, SSM_HEADS, SSM_HEADDIM, D_STATE)
    return (y_prompt, y_sample.reshape(bs, seq_s, D_MODEL),
            k_p.reshape(kv_shape), v_p.reshape(kv_shape),
            ssm_p.reshape(ssm_shape), conv_p[None], pool_p[None],
            ssm_s.reshape(ssm_shape), conv_s[None], pool_s[None])
```

```python
import functools

import jax
import jax.numpy as jnp
from jax import lax
from jax.experimental import pallas as pl
from jax.experimental.pallas import tpu as pltpu

F32 = jnp.float32
BF16 = jnp.bfloat16

D_MODEL = 1024
D_INNER = 2048
SSM_HEADS = 32
SSM_HEADDIM = 64
SSM_GROUPS = 8
HEADS_PER_GROUP = 4
GROUP_X = HEADS_PER_GROUP * SSM_HEADDIM
D_STATE = 128
D_CONV = 4
CONV_DIM = D_INNER + 2 * SSM_GROUPS * D_STATE
GROUP_CONV = GROUP_X + 2 * D_STATE
GROUP_IN = GROUP_X + GROUP_CONV
POOL_WINDOWS = (2, 4, 8, 16)
POOL_GROUP_DIM = 256
POOL_BUF = 15
N_MEM = 256
XATTN_HEADS = 4
XATTN_HEAD_DIM = 256
D_FF = 4096
EPS = 1e-5
PAST_LEN = 16384
LOG2_E = 1.4426950408889634

LANES = 128
SUBLANES = 8
Q = 128
SSD_STAGE = 2
FF_CHUNK = 1024
VMEM_LIMIT = 56 * 1024 * 1024

_NT = (((1,), (1,)), ((), ()))


def _params(n_grid):
    return pltpu.CompilerParams(dimension_semantics=("arbitrary",) * n_grid, vmem_limit_bytes=VMEM_LIMIT)


def _const_spec(shape):
    nd = len(shape)
    return pl.BlockSpec(shape, lambda *_: (0,) * nd, pipeline_mode=pl.Buffered(1))


def _layer_spec(a, layer):
    nd = a.ndim
    return pl.BlockSpec((None,) + a.shape[1:], lambda *_: (layer,) + (0,) * (nd - 1), pipeline_mode=pl.Buffered(1))


def _dot(a, b):
    return jnp.dot(a, b, preferred_element_type=F32)


def _dot_nt(a, b):
    return lax.dot_general(a, b, _NT, preferred_element_type=F32)


def _rms(x, g):
    return x * lax.rsqrt(jnp.mean(x * x, axis=-1, keepdims=True) + EPS) * g


def _silu(x):
    h = 0.5 * x
    return h + h * jnp.tanh(h)


def _kv_body(mem_ref, g_ref, wk_ref, wv_ref, wz_ref, wx_ref, wb_ref, wc_ref, wt_ref, wo_ref,
             k_ref, v_ref, kb_ref, vb_ref, wg_ref, wdt_ref, wob_ref):
    @pl.when(jnp.logical_and(pl.program_id(0) == 0, pl.program_id(1) == 0))
    def _():
        pad = jnp.zeros((LANES - SSM_HEADS, D_MODEL), F32)
        wdt_ref[...] = jnp.concatenate([wt_ref[0:SSM_HEADS, :], pad], axis=0).T.astype(BF16)

    wg_ref[0, :, 0:GROUP_X] = wz_ref[...].T.astype(BF16)
    wg_ref[0, :, GROUP_X:2 * GROUP_X] = wx_ref[...].T.astype(BF16)
    wg_ref[0, :, 2 * GROUP_X:2 * GROUP_X + D_STATE] = wb_ref[...].T.astype(BF16)
    wg_ref[0, :, 2 * GROUP_X + D_STATE:GROUP_IN] = wc_ref[...].T.astype(BF16)
    wob_ref[...] = wo_ref[...].astype(BF16)
    mn = _rms(mem_ref[...], g_ref[0]).astype(BF16)
    k = _dot(mn, wk_ref[0].astype(BF16))
    v = _dot(mn, wv_ref[0].astype(BF16))
    for h in range(XATTN_HEADS):
        sl = slice(h * XATTN_HEAD_DIM, (h + 1) * XATTN_HEAD_DIM)
        k_ref[0, :, h, :] = k[:, sl]
        v_ref[0, :, h, :] = v[:, sl]
    kb_ref[0] = k.astype(BF16)
    vb_ref[0] = v.astype(BF16)


def _kv_call(mem, g_mem, wk, wv, w_in_t, w_out):
    n = mem.shape[0]
    depth = wk.shape[0]
    nt = SSM_GROUPS // depth
    tm = n // nt
    assert depth * nt == SSM_GROUPS and tm * nt == n and tm % SUBLANES == 0
    row = pl.BlockSpec((tm, D_MODEL), lambda l, t: (t, 0))
    wspec = pl.BlockSpec((1, D_MODEL, D_MODEL), lambda l, t: (l, 0, 0))
    ospec = pl.BlockSpec((1, tm, D_MODEL), lambda l, t: (l, t, 0))
    hspec = pl.BlockSpec((1, tm, XATTN_HEADS, XATTN_HEAD_DIM), lambda l, t: (l, t, 0, 0))

    def rows(height, first_block):
        return pl.BlockSpec((height, D_MODEL), lambda l, t: (first_block + l * nt + t, 0))

    x0 = D_INNER // GROUP_X
    b0 = 2 * D_INNER // D_STATE
    wo_spec = pl.BlockSpec((w_out.shape[0] // SSM_GROUPS, D_MODEL), lambda l, t: (l * nt + t, 0))
    dt0 = D_INNER + CONV_DIM
    assert dt0 % LANES == 0 and w_in_t.shape[0] - dt0 == SSM_HEADS
    dt_spec = pl.BlockSpec((LANES, D_MODEL), lambda l, t: (dt0 // LANES, 0))
    return pl.pallas_call(
        _kv_body,
        grid=(depth, nt),
        in_specs=[row, pl.BlockSpec((1, 1, D_MODEL), lambda l, t: (l, 0, 0)), wspec, wspec,
                  rows(GROUP_X, 0), rows(GROUP_X, x0), rows(D_STATE, b0), rows(D_STATE, b0 + SSM_GROUPS),
                  dt_spec, wo_spec],
        out_specs=[hspec, hspec, ospec, ospec,
                   pl.BlockSpec((1, D_MODEL, GROUP_IN), lambda l, t: (l * nt + t, 0, 0)),
                   pl.BlockSpec((D_MODEL, LANES), lambda l, t: (0, 0)), wo_spec],
        out_shape=[jax.ShapeDtypeStruct((depth, n, XATTN_HEADS, XATTN_HEAD_DIM), F32)] * 2
        + [jax.ShapeDtypeStruct((depth, n, D_MODEL), BF16)] * 2
        + [jax.ShapeDtypeStruct((SSM_GROUPS, D_MODEL, GROUP_IN), BF16),
           jax.ShapeDtypeStruct((D_MODEL, LANES), BF16), jax.ShapeDtypeStruct(w_out.shape, BF16)],
        compiler_params=_params(2),
        name="mem_kv",
    )(mem, g_mem, wk, wv, w_in_t, w_in_t, w_in_t, w_in_t, w_in_t, w_out)


def _attn_heads(q, k_of, v_of):
    heads = range(XATTN_HEADS)
    ss = [_dot_nt(q[:, h * XATTN_HEAD_DIM:(h + 1) * XATTN_HEAD_DIM].astype(BF16), k_of(h)) * (XATTN_HEAD_DIM ** -0.5)
          for h in heads]
    es = [jnp.exp(s - jnp.max(s, axis=-1, keepdims=True)) for s in ss]
    ps = [e * (1.0 / jnp.sum(e, axis=-1, keepdims=True)) for e in es]
    return jnp.concatenate([_dot(ps[h].astype(BF16), v_of(h)) for h in heads], axis=-1)


def _mlp(x, g, wup_ref, wdn_ref):
    hn = _rms(x, g).astype(BF16)
    acc = x
    for c in range(D_FF // FF_CHUNK):
        a = jnp.maximum(_dot(hn, wup_ref[:, c * FF_CHUNK:(c + 1) * FF_CHUNK]), 0.0)
        acc = acc + _dot((a * a).astype(BF16), wdn_ref[c * FF_CHUNK:(c + 1) * FF_CHUNK, :])
    return acc


def _pool_windows(load, u, pos, n):
    outs = []
    for gi, w in enumerate(POOL_WINDOWS):
        cols = slice(gi * POOL_GROUP_DIM, (gi + 1) * POOL_GROUP_DIM)
        acc = load(0, cols)
        for k in range(1, w):
            acc = acc + load(k, cols)
        cnt = jnp.minimum(pos + 1, w).astype(F32)
        outs.append(acc / cnt - u[:, cols])
    return outs


def _pool_windows_rolled(ext_ref, u, pos, n):
    outs = []
    for gi, w in enumerate(POOL_WINDOWS):
        assert w & (w - 1) == 0 and w <= 2 * SUBLANES
        cols = slice(gi * POOL_GROUP_DIM, (gi + 1) * POOL_GROUP_DIM)
        acc = ext_ref[0:n + 2 * SUBLANES, cols]
        span = 1
        while span < w:
            acc = acc + pltpu.roll(acc, span, axis=0)
            span *= 2
        cnt = jnp.minimum(pos + 1, w).astype(F32)
        outs.append(acc[2 * SUBLANES:, :] / cnt - u[:, cols])
    return outs


def _pool_mix(pooled, wpool_ref, scale):
    outs = [_dot(p.astype(BF16), wpool_ref[gi]) for gi, p in enumerate(pooled)]
    return jnp.concatenate(outs, axis=-1) * scale


def _prompt_layer_body(*refs, tq, with_pool, with_final, with_sample):
    it = iter(refs)
    x_ref = next(it)
    if with_pool:
        gmix_ref, wpool_ref, pscale_ref = next(it), next(it), next(it)
    gx_ref, wq_ref, kb_ref, vb_ref, wo_ref = next(it), next(it), next(it), next(it), next(it)
    gm_ref, wup_ref, wdn_ref = next(it), next(it), next(it)
    if with_final:
        gf_ref = next(it)
    if with_sample:
        qs_ref, ks_ref, vs_ref = next(it), next(it), next(it)
    y_ref = next(it)
    if with_pool:
        pool_ref = next(it)
    if with_sample:
        os_ref = next(it)
    if with_pool:
        ext_ref = next(it)

    x = x_ref[0]
    if with_pool:
        j = pl.program_id(1)

        @pl.when(j == 0)
        def _():
            ext_ref[0:2 * SUBLANES, :] = jnp.zeros((2 * SUBLANES, D_MODEL), F32)

        u = _rms(x, gmix_ref[...])
        ext_ref[pl.ds(2 * SUBLANES, tq), :] = u
        pos = j * tq + lax.broadcasted_iota(jnp.int32, (tq, 1), 0)
        pooled = _pool_windows_rolled(ext_ref, u, pos, tq)
        x = x + _pool_mix(pooled, wpool_ref, pscale_ref[...])
        pool_ref[0] = ext_ref[pl.ds(tq + 1, POOL_BUF), :]
        ext_ref[0:2 * SUBLANES, :] = ext_ref[pl.ds(tq, 2 * SUBLANES), :]

    if with_sample:
        _attn_sample_stage(qs_ref, ks_ref, vs_ref, os_ref)

    q = _dot(_rms(x, gx_ref[...]).astype(BF16), wq_ref[...])
    o = _attn_heads(
        q,
        lambda h: kb_ref[0, 0, :, h * XATTN_HEAD_DIM:(h + 1) * XATTN_HEAD_DIM],
        lambda h: vb_ref[0, 0, :, h * XATTN_HEAD_DIM:(h + 1) * XATTN_HEAD_DIM],
    )
    y_ref[0] = x + _dot(o.astype(BF16), wo_ref[...])
    hn = _rms(y_ref[0], gm_ref[...]).astype(BF16)
    for c in range(D_FF // FF_CHUNK):
        a = jnp.maximum(_dot(hn, wup_ref[:, c * FF_CHUNK:(c + 1) * FF_CHUNK]), 0.0)
        y_ref[0] += _dot((a * a).astype(BF16), wdn_ref[c * FF_CHUNK:(c + 1) * FF_CHUNK, :])
    if with_final:
        y_ref[0] = _rms(y_ref[0], gf_ref[...])


def _prompt_layer_call(x, layer, kb, vb, gx, wq, wo, gm, wup, wdn, pool=None, gfinal=None, sample=None, tq=512):
    b, seq, _ = x.shape
    with_pool, with_final, with_sample = pool is not None, gfinal is not None, sample is not None
    nj = seq // tq
    xspec = pl.BlockSpec((1, tq, D_MODEL), lambda i, j: (i, j, 0))
    kvspec = pl.BlockSpec((1, 1, N_MEM, D_MODEL), lambda i, j: (layer, i, 0, 0))
    args, specs = [x], [xspec]
    if with_pool:
        gmix, wpool, pscale = pool
        args += [gmix, wpool, pscale]
        specs += [_layer_spec(gmix, layer), _layer_spec(wpool, layer // 2), _layer_spec(pscale, layer // 2)]
    args += [gx, wq, kb, vb, wo, gm, wup, wdn]
    specs += [_layer_spec(gx, layer), _layer_spec(wq, layer), kvspec, kvspec, _layer_spec(wo, layer),
              _layer_spec(gm, layer), _layer_spec(wup, layer), _layer_spec(wdn, layer)]
    if with_final:
        args.append(gfinal)
        specs.append(_const_spec(gfinal.shape))
    out_shape = [jax.ShapeDtypeStruct(x.shape, F32)]
    out_specs = [xspec]
    scratch = []
    if with_pool:
        out_shape.append(jax.ShapeDtypeStruct((b, POOL_BUF, D_MODEL), F32))
        out_specs.append(pl.BlockSpec((1, POOL_BUF, D_MODEL), lambda i, j: (i, 0, 0)))
        scratch.append(pltpu.VMEM((tq + 2 * SUBLANES, D_MODEL), F32))
    if with_sample:
        q_pieces, k_rows, v_rows = sample
        n_batch = k_rows.shape[1]
        nb = n_batch // (b * nj)
        rows = q_pieces.shape[2] // (b * nj)
        assert nb * b * nj == n_batch and rows == nb * SUBLANES
        qspec = pl.BlockSpec(q_pieces.shape[:2] + (rows, LANES), lambda i, j: (0, 0, i * nj + j, 0))
        cspec = pl.BlockSpec((1, nb, KV_ROWS, LANES), lambda i, j: (layer, i * nj + j, 0, 0))
        args += [q_pieces, k_rows, v_rows]
        specs += [qspec, cspec, cspec]
        out_shape.append(jax.ShapeDtypeStruct(q_pieces.shape, F32))
        out_specs.append(qspec)
    outs = pl.pallas_call(
        functools.partial(_prompt_layer_body, tq=tq, with_pool=with_pool, with_final=with_final,
                          with_sample=with_sample),
        grid=(b, nj),
        in_specs=specs,
        out_specs=out_specs,
        out_shape=out_shape,
        scratch_shapes=scratch,
        compiler_params=_params(2),
        name="prompt_layer_pool" if with_pool else "prompt_layer",
    )(*args)
    return outs if len(outs) > 1 else outs[0]


ROW_ACS, ROW_DECAY, ROW_END = 0, 1, 2


def _dt_stage(xn_ref, wdt_ref, dtb_ref, alog_ref, dt_ref, acs_ref, rowsT_ref, mask, n_chunks, seg_last):
    z = _dot(xn_ref[...], wdt_ref[...]) + dtb_ref[...]
    dt_ref[...] = jnp.maximum(z, 0.0) + jnp.log1p(jnp.exp(-jnp.abs(z)))
    a = -jnp.exp(alog_ref[...]) * LOG2_E
    tri = mask.astype(F32)
    for c in range(n_chunks):
        rows = slice(c * Q, (c + 1) * Q)
        dt_c = dt_ref[rows, :]
        acs = jnp.dot(tri, dt_c * a, precision=lax.Precision.HIGHEST, preferred_element_type=F32)
        acs_ref[rows, :] = acs
        acsT, dtT = acs.T, dt_c.T
        rowsT_ref[c, ROW_ACS] = acsT
        rowsT_ref[c, ROW_DECAY] = acsT - jnp.log2(dtT)
        rowsT_ref[c, ROW_END] = dtT * jnp.exp2(seg_last(acs).T - acsT)


def _conv_rows(load_shifted, cw_ref, cb_ref, g):
    acc = cb_ref[g]
    for k in range(D_CONV):
        acc = acc + cw_ref[g, k:k + 1, :] * load_shifted(k)
    return _silu(acc)


def _conv_rolled(exts, cw_ref, cb_ref, g):
    accs = [cb_ref[g] + cw_ref[g, D_CONV - 1:D_CONV, :] * e for e in exts]
    for s in range(1, D_CONV):
        accs = [a + cw_ref[g, D_CONV - 1 - s:D_CONV - s, :] * pltpu.roll(e, s, axis=0) for a, e in zip(accs, exts)]
    return [_silu(a[SUBLANES:, :]) for a in accs]


def _expand_heads(cols):
    half = lax.broadcasted_iota(jnp.int32, (Q, LANES), 1) < SSM_HEADDIM
    return jnp.concatenate([jnp.where(half, cols[0], cols[1]), jnp.where(half, cols[2], cols[3])], axis=1)


def _head_lane_masks():
    lane_head = _div_pow2(lax.broadcasted_iota(jnp.int32, (Q, GROUP_X), 1), SSM_HEADDIM)
    return [jnp.where(lane_head == r, 1.0, 0.0).astype(BF16) for r in range(HEADS_PER_GROUP)]


def _ssd_group(xgs, bgs, cgs, acs_cols, row_of, dskips, mask, head_masks):
    n = len(xgs)
    heads = range(HEADS_PER_GROUP)
    cbs = [_dot_nt(cg, bg) for cg, bg in zip(cgs, bgs)]
    xbs = [xg.astype(BF16) for xg in xgs]
    lcats = [jnp.concatenate(
        [(cbs[i] * jnp.exp2(jnp.where(mask, acs_cols[i][r] - row_of(i, ROW_DECAY, r), -jnp.inf))).astype(BF16)
         for r in heads], axis=1) for i in range(n)]
    xblks = [jnp.concatenate([xb * hm for hm in head_masks], axis=0) for xb in xbs]
    ydxs = [_dot(l, xb) + dsk * xg for l, xb, xg, dsk in zip(lcats, xblks, xgs, dskips)]
    xTs = [xg.T for xg in xgs]
    xddTs = [jnp.concatenate(
        [xTs[i][r * SSM_HEADDIM:(r + 1) * SSM_HEADDIM, :] * row_of(i, ROW_END, r) for r in heads], axis=0)
        for i in range(n)]
    eacss = [jnp.exp2(_expand_heads(cols)) for cols in acs_cols]
    return ydxs, eacss, xddTs


def _gate_norm(y, z, ng):
    yg = y * _silu(z)
    return yg * lax.rsqrt(jnp.mean(yg * yg, axis=-1, keepdims=True) + EPS) * ng


def _div_pow2(v, d):
    assert d & (d - 1) == 0
    return lax.shift_right_logical(v, d.bit_length() - 1)


def _causal_mask(block):
    li = lax.broadcasted_iota(jnp.int32, (Q, Q), 0)
    si = lax.broadcasted_iota(jnp.int32, (Q, Q), 1)
    m = li >= si
    if block is not None:
        m = jnp.logical_and(m, _div_pow2(li, block) == _div_pow2(si, block))
    return m


def _head_rows(vals):
    return jnp.concatenate([jnp.broadcast_to(v, (SSM_HEADDIM, LANES)) for v in vals], axis=0)


def _mamba_prompt_body(x_ref, gmix_ref, wg_ref, wdt_ref, cw_ref, cb_ref, dtb_ref, alog_ref, dsk_ref, ng_ref, wout_ref,
                       *rest, tq, n_cast):
    cast_in, rest = rest[:n_cast], rest[n_cast:]
    y_ref, ssm_ref, conv_ref = rest[:3]
    cast_out, rest = rest[3:3 + n_cast], rest[3 + n_cast:]
    xn_ref, dt_ref, acs_ref, rowsT_ref, zxg_ref, xpad_ref, act_ref, hist_ref, h_ref, yg_ref = rest
    j = pl.program_id(1)
    n_chunks = tq // Q

    for src, dst in zip(cast_in, cast_out):
        dst[...] = src[...].astype(BF16)

    @pl.when(j == 0)
    def _():
        hist_ref[...] = jnp.zeros_like(hist_ref)
        h_ref[...] = jnp.zeros_like(h_ref)

    mask = _causal_mask(None)
    head_masks = _head_lane_masks()
    xn_ref[...] = _rms(x_ref[0], gmix_ref[...]).astype(BF16)

    def in_proj(g):
        zxg_ref[g % 2] = _dot(xn_ref[...], wg_ref[g])

    in_proj(0)
    _dt_stage(xn_ref, wdt_ref, dtb_ref, alog_ref, dt_ref, acs_ref, rowsT_ref, mask, n_chunks,
              lambda v: jnp.broadcast_to(v[Q - 1:Q, :], v.shape))
    for g in range(SSM_GROUPS):
        if g + 1 < SSM_GROUPS:
            in_proj(g + 1)
        zxg, xpad, act = zxg_ref.at[g % 2], xpad_ref.at[g % 2], act_ref.at[g % 2]
        xpad[0:SUBLANES, :] = hist_ref[g]
        xpad[pl.ds(SUBLANES, tq), :] = zxg[:, GROUP_X:GROUP_IN]

        conv_out = _conv_rolled([xpad[c * Q:(c + 1) * Q + SUBLANES, :] for c in range(n_chunks)], cw_ref, cb_ref, g)
        for c, o in enumerate(conv_out):
            act[c * Q:(c + 1) * Q, :] = o
        hist_ref[g] = xpad[pl.ds(tq, SUBLANES), :]

        h_in = h_ref[g]
        for c0 in range(0, n_chunks, SSD_STAGE):
            chunks = list(range(c0, min(c0 + SSD_STAGE, n_chunks)))
            idx = range(len(chunks))
            rows = [slice(c * Q, (c + 1) * Q) for c in chunks]
            bgs = [act[r, GROUP_X:GROUP_X + D_STATE].astype(BF16) for r in rows]
            cgs = [act[r, GROUP_X + D_STATE:GROUP_CONV].astype(BF16) for r in rows]
            rowsT_cs = [rowsT_ref.at[c] for c in chunks]
            heads = range(HEADS_PER_GROUP * g, HEADS_PER_GROUP * (g + 1))
            acs_cols = [[jnp.broadcast_to(acs_ref[r, h:h + 1], (Q, LANES)) for h in heads] for r in rows]
            ydxs, eacss, xddTs = _ssd_group(
                [act[r, 0:GROUP_X] for r in rows], bgs, cgs, acs_cols,
                lambda i, slot, r: rowsT_cs[i][slot, heads[r]:heads[r] + 1, :], [dsk_ref[g]] * len(rows),
                mask, head_masks)
            updates = [_dot(xddTs[i].astype(BF16), bgs[i]) for i in idx]
            scales = [_head_rows([jnp.exp2(rowsT_cs[i][ROW_ACS, h:h + 1, Q - 1:Q]) for h in heads]) for i in idx]
            hs = [h_in]
            for i in idx:
                hs.append(scales[i] * hs[i] + updates[i])
            h_in = hs[-1]
            yoffs = [_dot_nt(cgs[i], hs[i].astype(BF16)) for i in idx]
            for i in idx:
                yg_ref[rows[i], g * GROUP_X:(g + 1) * GROUP_X] = _gate_norm(
                    ydxs[i] + eacss[i] * yoffs[i], zxg[rows[i], 0:GROUP_X], ng_ref[g]).astype(BF16)
        h_ref[g] = h_in

        hist = hist_ref[g]
        conv_ref[0, :, g * GROUP_X:(g + 1) * GROUP_X] = hist[SUBLANES - (D_CONV - 1):, 0:GROUP_X]
        conv_ref[0, :, D_INNER + g * D_STATE:D_INNER + (g + 1) * D_STATE] = (
            hist[SUBLANES - (D_CONV - 1):, GROUP_X:GROUP_X + D_STATE])
        conv_ref[0, :, D_INNER + (SSM_GROUPS + g) * D_STATE:D_INNER + (SSM_GROUPS + g + 1) * D_STATE] = (
            hist[SUBLANES - (D_CONV - 1):, GROUP_X + D_STATE:GROUP_CONV])

        if g % 2 == 1:
            cols = slice((g - 1) * GROUP_X, (g + 1) * GROUP_X)
            y_ref[0] = (x_ref[0] if g == 1 else y_ref[0]) + _dot(yg_ref[:, cols], wout_ref[cols, :])

    @pl.when(j == pl.num_programs(1) - 1)
    def _():
        ssm_ref[0] = h_ref[...]


def _mamba_weight_specs(w):
    return [_const_spec(a.shape) for a in w]


def _mamba_prompt_call(x, gmix, layer, w, casts=(), tq=512):
    b, seq, _ = x.shape
    nj = seq // tq
    xspec = pl.BlockSpec((1, tq, D_MODEL), lambda i, j: (i, j, 0))
    n_chunks = tq // Q
    cast2d = [a.reshape(-1, a.shape[-1]) for a in casts]
    cast_specs = []
    for a in cast2d:
        rows = a.shape[0] // (b * nj)
        assert rows * b * nj == a.shape[0] and rows % (2 * SUBLANES) == 0
        cast_specs.append(pl.BlockSpec((rows, a.shape[1]), lambda i, j: (i * nj + j, 0)))
    outs = pl.pallas_call(
        functools.partial(_mamba_prompt_body, tq=tq, n_cast=len(casts)),
        grid=(b, nj),
        in_specs=[xspec, _layer_spec(gmix, layer)] + _mamba_weight_specs(w) + cast_specs,
        out_specs=[xspec,
                   pl.BlockSpec((1, SSM_GROUPS, GROUP_X, D_STATE), lambda i, j: (i, 0, 0, 0)),
                   pl.BlockSpec((1, D_CONV - 1, CONV_DIM), lambda i, j: (i, 0, 0))] + cast_specs,
        out_shape=[jax.ShapeDtypeStruct(x.shape, F32),
                   jax.ShapeDtypeStruct((b, SSM_GROUPS, GROUP_X, D_STATE), F32),
                   jax.ShapeDtypeStruct((b, D_CONV - 1, CONV_DIM), F32)]
        + [jax.ShapeDtypeStruct(a.shape, BF16) for a in cast2d],
        scratch_shapes=[
            pltpu.VMEM((tq, D_MODEL), BF16),
            pltpu.VMEM((tq, LANES), F32),
            pltpu.VMEM((tq, LANES), F32),
            pltpu.VMEM((n_chunks, 3, LANES, Q), F32),
            pltpu.VMEM((2, tq, GROUP_IN), F32),
            pltpu.VMEM((2, tq + SUBLANES, GROUP_CONV), F32),
            pltpu.VMEM((2, tq, GROUP_CONV), F32),
            pltpu.VMEM((SSM_GROUPS, SUBLANES, GROUP_CONV), F32),
            pltpu.VMEM((SSM_GROUPS, GROUP_X, D_STATE), F32),
            pltpu.VMEM((tq, D_INNER), BF16),
        ],
        compiler_params=_params(2),
        name="mamba_prompt",
    )(x, gmix, *w, *cast2d)
    return list(outs[:3]) + [o.reshape(a.shape) for o, a in zip(outs[3:], casts)]


def _mamba_sample_body(x_ref, gmix_ref, wg_ref, wdt_ref, cw_ref, cb_ref, dtb_ref, alog_ref, dsk_ref, ng_ref, wout_ref,
                       cx_ref, cbm_ref, ccm_ref, sst_ref, gx_ref, wq_ref,
                       y_ref, q_ref, ssm_ref, ox_ref, obm_ref, ocm_ref,
                       xn_ref, dt_ref, acs_ref, rowsT_ref, colb_ref, scale_ref, zxg_ref, xpad_ref, yg_ref, *, seq, gp):
    s = pl.program_id(1)
    nb = Q // seq
    tail = SUBLANES - (D_CONV - 1)
    mask = _causal_mask(seq)
    head_masks = _head_lane_masks()

    @pl.when(s == 0)
    def _():
        def seg_last(v):
            v3 = v.reshape(nb, seq, v.shape[-1])
            return jnp.broadcast_to(v3[:, seq - 1:seq, :], v3.shape).reshape(v.shape)

        xn_ref[...] = _rms(x_ref[...], gmix_ref[...]).astype(BF16)
        _dt_stage(xn_ref, wdt_ref, dtb_ref, alog_ref, dt_ref, acs_ref, rowsT_ref, mask, 1, seg_last)
        acs = acs_ref[...]
        for h in range(SSM_HEADS):
            colb_ref[h] = jnp.broadcast_to(acs[:, h:h + 1], (Q, LANES))
        acsT = rowsT_ref[0, ROW_ACS]
        for bi in range(nb):
            scale_ref[bi] = jnp.broadcast_to(jnp.exp2(acsT[:, (bi + 1) * seq - 1:(bi + 1) * seq]), (LANES, LANES))

    pair_b = _div_pow2(lax.broadcasted_iota(jnp.int32, (2 * seq, GROUP_X), 0), seq)
    row_b = _div_pow2(lax.broadcasted_iota(jnp.int32, (Q, D_STATE), 0), seq)
    row_masks = [jnp.where(row_b == bi, 1.0, 0.0).astype(BF16) for bi in range(nb)]
    gis = range(gp)
    gs = [s * gp + gi for gi in gis]
    h0s = [g * HEADS_PER_GROUP for g in gs]
    parts = ((cx_ref, ox_ref, GROUP_X, slice(0, GROUP_X)),
             (cbm_ref, obm_ref, D_STATE, slice(GROUP_X, GROUP_X + D_STATE)),
             (ccm_ref, ocm_ref, D_STATE, slice(GROUP_X + D_STATE, GROUP_CONV)))
    for gi in gis:
        zxg_ref[gi] = _dot(xn_ref[...], wg_ref[gs[gi]])
    for gi in gis:
        for src, _, w, dst in parts:
            xpad_ref[gi, :, tail:SUBLANES, dst] = src[:, :, gi * w:(gi + 1) * w]
        xpad_ref[gi, :, SUBLANES:SUBLANES + seq, :] = zxg_ref[gi, :, GROUP_X:GROUP_IN].reshape(nb, seq, GROUP_CONV)
    acts = [_conv_rows(lambda k: xpad_ref[gi, :, pl.ds(tail + k, seq), :].reshape(Q, GROUP_CONV),
                       cw_ref, cb_ref, gs[gi]) for gi in gis]
    for gi in gis:
        for _, out, w, dst in parts:
            out[:, :, gi * w:(gi + 1) * w] = xpad_ref[gi, :, pl.ds(tail + seq, D_CONV - 1), dst]
    bgs = [a[:, GROUP_X:GROUP_X + D_STATE].astype(BF16) for a in acts]
    cgs = [a[:, GROUP_X + D_STATE:GROUP_CONV].astype(BF16) for a in acts]
    ydxs, eacss, xddTs = _ssd_group(
        [a[:, 0:GROUP_X] for a in acts], bgs, cgs,
        [[colb_ref[h0 + r] for r in range(HEADS_PER_GROUP)] for h0 in h0s],
        lambda i, slot, r: rowsT_ref[0, slot, pl.ds(h0s[i] + r, 1), :], [dsk_ref[g] for g in gs], mask, head_masks)
    yoffs = [jnp.concatenate(
        [jnp.where(pair_b == 0,
                   _dot_nt(cgs[gi][2 * p * seq:2 * (p + 1) * seq, :], sst_ref[2 * p, gi].astype(BF16)),
                   _dot_nt(cgs[gi][2 * p * seq:2 * (p + 1) * seq, :], sst_ref[2 * p + 1, gi].astype(BF16)))
         for p in range(nb // 2)], axis=0) for gi in gis]
    updates = [_dot(xddTs[gi].astype(BF16), jnp.concatenate([bgs[gi] * row_masks[bi] for bi in range(nb)], axis=1))
               for gi in gis]
    for gi in gis:
        for bi in range(nb):
            scale = _head_rows([scale_ref[bi, pl.ds(h0s[gi] + r, 1), :] for r in range(HEADS_PER_GROUP)])
            ssm_ref[bi, gi] = scale * sst_ref[bi, gi] + updates[gi][:, bi * D_STATE:(bi + 1) * D_STATE]
    for gi in gis:
        yg_ref[gs[gi]] = _gate_norm(ydxs[gi] + eacss[gi] * yoffs[gi], zxg_ref[gi, :, 0:GROUP_X],
                                    ng_ref[gs[gi]]).astype(BF16)

    @pl.when(s == pl.num_programs(1) - 1)
    def _():
        x1 = x_ref[...]
        for g in range(SSM_GROUPS):
            x1 = x1 + _dot(yg_ref[g], wout_ref[g * GROUP_X:(g + 1) * GROUP_X, :])
        y_ref[...] = x1
        _store_pieces(q_ref, _dot(_rms(x1, gx_ref[...]).astype(BF16), wq_ref[...]))


def _mamba_sample_call(x2d, gmix, w, conv_state, ssm_state, gx, wq, layer, seq, gp=2):
    n = x2d.shape[0]
    batch = n // seq
    nb = Q // seq
    bc_cols = SSM_GROUPS * D_STATE
    xspec = pl.BlockSpec((Q, D_MODEL), lambda i, s: (i, 0))
    stspec = pl.BlockSpec((nb, gp, GROUP_X, D_STATE), lambda i, s: (i, s, 0, 0))

    def cols(width, first):
        assert first % (gp * width) == 0
        return pl.BlockSpec((nb, D_CONV - 1, gp * width), lambda i, s: (i, 0, first // (gp * width) + s))

    conv_in = [cols(GROUP_X, 0), cols(D_STATE, D_INNER), cols(D_STATE, D_INNER + bc_cols)]
    conv_out = [cols(GROUP_X, 0), cols(D_STATE, 0), cols(D_STATE, 0)]
    y, q, ssm, ox, obm, ocm = pl.pallas_call(
        functools.partial(_mamba_sample_body, seq=seq, gp=gp),
        grid=(n // Q, SSM_GROUPS // gp),
        in_specs=[xspec, _layer_spec(gmix, layer)] + _mamba_weight_specs(w)
        + conv_in + [stspec, _layer_spec(gx, layer), _layer_spec(wq, layer)],
        out_specs=[xspec, _pieces_spec(Q, lambda i, s: (0, 0, i, 0)), stspec] + conv_out,
        out_shape=[jax.ShapeDtypeStruct(x2d.shape, F32), _pieces_shape(n),
                   jax.ShapeDtypeStruct((batch, SSM_GROUPS, GROUP_X, D_STATE), F32),
                   jax.ShapeDtypeStruct((batch, D_CONV - 1, D_INNER), F32),
                   jax.ShapeDtypeStruct((batch, D_CONV - 1, bc_cols), F32),
                   jax.ShapeDtypeStruct((batch, D_CONV - 1, bc_cols), F32)],
        scratch_shapes=[
            pltpu.VMEM((Q, D_MODEL), BF16),
            pltpu.VMEM((Q, LANES), F32),
            pltpu.VMEM((Q, LANES), F32),
            pltpu.VMEM((1, 3, LANES, Q), F32),
            pltpu.VMEM((SSM_HEADS, Q, LANES), F32),
            pltpu.VMEM((nb, LANES, LANES), F32),
            pltpu.VMEM((gp, Q, GROUP_IN), F32),
            pltpu.VMEM((gp, nb, SUBLANES + seq, GROUP_CONV), F32),
            pltpu.VMEM((SSM_GROUPS, Q, GROUP_X), BF16),
        ],
        compiler_params=_params(2),
        name="mamba_sample",
    )(x2d, gmix, *w, conv_state, conv_state, conv_state, ssm_state, gx, wq)
    return y, q, ssm, jnp.concatenate([ox, obm, ocm], axis=-1)


def _pool_sample_body(x_ref, st_ref, gmix_ref, wpool_ref, pscale_ref, gx_ref, wq_ref,
                      y_ref, q_ref, pool_ref, ext_ref, *, seq, pos0):
    nb = x_ref.shape[0]
    n = nb * seq
    x = x_ref[...].reshape(n, D_MODEL)
    u = _rms(x, gmix_ref[...])
    hist0 = 2 * SUBLANES - POOL_BUF
    ext_ref[:, pl.ds(hist0, POOL_BUF), :] = st_ref[...]
    ext_ref[:, 2 * SUBLANES:2 * SUBLANES + seq, :] = u.reshape(nb, seq, D_MODEL)
    pos = pos0 + lax.broadcasted_iota(jnp.int32, (nb, seq, 1), 1).reshape(n, 1)
    pooled = _pool_windows(
        lambda k, cols: ext_ref[:, pl.ds(2 * SUBLANES - k, seq), cols].reshape(n, POOL_GROUP_DIM), u, pos, n)
    x1 = x + _pool_mix(pooled, wpool_ref, pscale_ref[...])
    pool_ref[...] = ext_ref[:, pl.ds(hist0 + seq, POOL_BUF), :]
    y_ref[...] = x1.reshape(nb, seq, D_MODEL)
    _store_pieces(q_ref, _dot(_rms(x1, gx_ref[...]).astype(BF16), wq_ref[...]))


def _pool_sample_call(x, pool_state, gmix, wpool, pscale, gx, wq, layer, pos0, nb=16):
    batch, seq, _ = x.shape
    xspec = pl.BlockSpec((nb, seq, D_MODEL), lambda i: (i, 0, 0))
    pspec = pl.BlockSpec((nb, POOL_BUF, D_MODEL), lambda i: (i, 0, 0))
    return pl.pallas_call(
        functools.partial(_pool_sample_body, seq=seq, pos0=pos0),
        grid=(batch // nb,),
        in_specs=[xspec, pspec, _layer_spec(gmix, layer), _layer_spec(wpool, layer // 2),
                  _layer_spec(pscale, layer // 2), _layer_spec(gx, layer), _layer_spec(wq, layer)],
        out_specs=[xspec, _pieces_spec(nb * seq, lambda i: (0, 0, i, 0)), pspec],
        out_shape=[jax.ShapeDtypeStruct(x.shape, F32), _pieces_shape(batch * seq),
                   jax.ShapeDtypeStruct(pool_state.shape, F32)],
        scratch_shapes=[pltpu.VMEM((nb, 2 * SUBLANES + seq, D_MODEL), F32)],
        compiler_params=_params(1),
        name="pool_sample",
    )(x, pool_state, gmix, wpool, pscale, gx, wq)


HEAD_HALVES = XATTN_HEAD_DIM // LANES
KV_ROWS = N_MEM * XATTN_HEADS * HEAD_HALVES


def _attn_sample_stage(q_ref, k_ref, v_ref, o_ref):
    nb = k_ref.shape[1]
    seq = q_ref.shape[2] // nb
    assert seq == SUBLANES
    pieces = [(j, h) for j in range(HEAD_HALVES) for h in range(XATTN_HEADS)]
    half = XATTN_HEADS * seq
    grp = XATTN_HEADS * HEAD_HALVES
    lane = lax.broadcasted_iota(jnp.int32, (half, KV_ROWS), 1)
    rowi = lax.broadcasted_iota(jnp.int32, (half, KV_ROWS), 0)
    valid = (lane & (grp - 1)) == _div_pow2(rowi, seq)
    qs = [jnp.concatenate([q_ref[j, h, bi * seq:(bi + 1) * seq, :] for j, h in pieces], axis=0) for bi in range(nb)]
    prods = [_dot_nt((qs[bi] * (XATTN_HEAD_DIM ** -0.5)).astype(BF16), k_ref[0, bi].astype(BF16))
             for bi in range(nb)]
    ss = [jnp.where(valid, pr[0:half] + pltpu.roll(pr[half:], KV_ROWS - XATTN_HEADS, axis=1), -jnp.inf)
          for pr in prods]
    es = [jnp.exp(s - jnp.max(s, axis=-1, keepdims=True)) for s in ss]
    ps = [e * (1.0 / jnp.sum(e, axis=-1, keepdims=True)) for e in es]
    pes = [jnp.concatenate([p, pltpu.roll(p, XATTN_HEADS, axis=1)], axis=0).astype(BF16) for p in ps]
    for bi in range(nb):
        o = _dot(pes[bi], v_ref[0, bi].astype(BF16))
        for n, (j, h) in enumerate(pieces):
            o_ref[j, h, bi * seq:(bi + 1) * seq, :] = o[n * seq:(n + 1) * seq, :]


def _cache_rows(cache):
    d, b = cache.shape[:2]
    c = cache.reshape(d, b, N_MEM, XATTN_HEADS, HEAD_HALVES, LANES)
    return c.transpose(0, 1, 2, 4, 3, 5).reshape(d, b, KV_ROWS, LANES)


def _store_pieces(ref, v):
    for h in range(XATTN_HEADS):
        for j in range(HEAD_HALVES):
            c0 = (h * HEAD_HALVES + j) * LANES
            ref[j, h] = v[:, c0:c0 + LANES]


def _load_pieces(ref):
    return jnp.concatenate([ref[j, h] for h in range(XATTN_HEADS) for j in range(HEAD_HALVES)], axis=-1)


def _pieces_spec(rows, index_map):
    return pl.BlockSpec((HEAD_HALVES, XATTN_HEADS, rows, LANES), index_map)


def _pieces_shape(n_rows):
    return jax.ShapeDtypeStruct((HEAD_HALVES, XATTN_HEADS, n_rows, LANES), F32)


def _out_mlp_body(*refs, with_final):
    it = iter(refs)
    x_ref, o_ref, wo_ref, gm_ref, wup_ref, wdn_ref = (next(it) for _ in range(6))
    if with_final:
        gf_ref = next(it)
    y_ref = next(it)
    x = x_ref[...] + _dot(_load_pieces(o_ref).astype(BF16), wo_ref[...])
    x = _mlp(x, gm_ref[...], wup_ref, wdn_ref)
    if with_final:
        x = _rms(x, gf_ref[...])
    y_ref[...] = x


def _out_mlp_call(x2d, o_pieces, layer, wo, gm, wup, wdn, gfinal=None, tm=512):
    n = x2d.shape[0]
    tm = min(tm, n)
    row = pl.BlockSpec((tm, D_MODEL), lambda i: (i, 0))
    args = [x2d, o_pieces, wo, gm, wup, wdn]
    specs = [row, _pieces_spec(tm, lambda i: (0, 0, i, 0)), _layer_spec(wo, layer), _layer_spec(gm, layer),
             _layer_spec(wup, layer), _layer_spec(wdn, layer)]
    if gfinal is not None:
        args.append(gfinal)
        specs.append(_const_spec(gfinal.shape))
    return pl.pallas_call(
        functools.partial(_out_mlp_body, with_final=gfinal is not None),
        grid=(n // tm,),
        in_specs=specs,
        out_specs=row,
        out_shape=jax.ShapeDtypeStruct(x2d.shape, F32),
        compiler_params=_params(1),
        name="out_mlp",
    )(*args)


def _mamba_weights(wg, wdt, conv_w, conv_b, dt_bias, a_log, d_skip, norm_gated, w_out_bf16):
    pad = LANES - SSM_HEADS

    def grp(a, width):
        return a.reshape(a.shape[0], SSM_GROUPS, width)

    def conv_grp(a):
        return jnp.concatenate([grp(a[:, :D_INNER], GROUP_X), grp(a[:, D_INNER:D_INNER + SSM_GROUPS * D_STATE], D_STATE),
                                grp(a[:, D_INNER + SSM_GROUPS * D_STATE:], D_STATE)], axis=-1).transpose(1, 0, 2)

    cw = conv_grp(conv_w)
    cb = conv_grp(conv_b[None, :])
    dtb = jnp.pad(dt_bias, (0, pad))[None, :]
    alog = jnp.pad(a_log, (0, pad))[None, :]
    dsk = jnp.repeat(d_skip, SSM_HEADDIM).reshape(SSM_GROUPS, 1, GROUP_X)
    return (wg, wdt, cw, cb, dtb, alog, dsk, norm_gated.reshape(SSM_GROUPS, 1, GROUP_X), w_out_bf16)


def kernel(x_prompt, x_sample, cache_mem_k, cache_mem_v, state_ssm, state_conv, state_pool, mem_prompt, norm_mix, norm_xattn, norm_mem, norm_mlp, norm_final, w_in, conv_w, conv_b, dt_bias, a_log, d_skip, norm_gated, w_out, w_pool, pool_scale, w_xq, w_xk, w_xv, w_xo, w_up, w_down):
    bp, seq_p, _ = x_prompt.shape
    bs, seq_s, _ = x_sample.shape
    depth = w_xq.shape[0]
    per_layer = lambda a: a[:, None, :]

    wpool = w_pool.astype(BF16)
    g_mix, g_x, g_mlp, p_scale = (per_layer(a) for a in (norm_mix, norm_xattn, norm_mlp, pool_scale))
    g_final = norm_final[None, :]

    k_p, v_p, kb, vb, wg, wdt, wout = _kv_call(mem_prompt.reshape(bp * N_MEM, D_MODEL), per_layer(norm_mem),
                                               w_xk, w_xv, w_in[0].T, w_out[0])
    mw = _mamba_weights(wg, wdt, conv_w[0], conv_b[0], dt_bias[0], a_log[0], d_skip[0], norm_gated[0], wout)
    kv_shape = (depth, bp, N_MEM, XATTN_HEADS, XATTN_HEAD_DIM)
    kb = kb.reshape(depth, bp, N_MEM, D_MODEL)
    vb = vb.reshape(depth, bp, N_MEM, D_MODEL)

    kc, vc = _cache_rows(cache_mem_k), _cache_rows(cache_mem_v)
    xp, ssm_p, conv_p, wq, wo, wup, wdn = _mamba_prompt_call(
        x_prompt, g_mix, 0, mw, casts=(w_xq, w_xo, w_up, w_down))
    xs = x_sample.reshape(bs * seq_s, D_MODEL)
    ssm_in = state_ssm[0].reshape(bs, SSM_GROUPS, GROUP_X, D_STATE)
    xs, q, ssm_s, conv_s = _mamba_sample_call(xs, g_mix, mw, state_conv[0], ssm_in, g_x, wq, 0, seq_s)
    xp, o = _prompt_layer_call(xp, 0, kb, vb, g_x, wq, wo, g_mlp, wup, wdn, sample=(q, kc, vc))
    xs = _out_mlp_call(xs, o, 0, wo, g_mlp, wup, wdn)
    xs, q, pool_s = _pool_sample_call(xs.reshape(bs, seq_s, D_MODEL), state_pool[0], g_mix, wpool, p_scale,
                                      g_x, wq, 1, PAST_LEN)
    y_prompt, pool_p, o = _prompt_layer_call(
        xp, 1, kb, vb, g_x, wq, wo, g_mlp, wup, wdn,
        pool=(g_mix, wpool, p_scale), gfinal=g_final, sample=(q, kc, vc))
    y_sample = _out_mlp_call(xs.reshape(bs * seq_s, D_MODEL), o, 1, wo, g_mlp, wup, wdn, gfinal=g_final)

    ssm_shape = (1, -1, SSM_HEADS, SSM_HEADDIM, D_STATE)
    return (y_prompt, y_sample.reshape(bs, seq_s, D_MODEL),
            k_p.reshape(kv_shape), v_p.reshape(kv_shape),
            ssm_p.reshape(ssm_shape), conv_p[None], pool_p[None],
            ssm_s.reshape(ssm_shape), conv_s[None], pool_s[None])
```
